```python
import jax
import jax.numpy as jnp
from jax import lax
import numpy as np

D_MODEL = 2048
BATCH = 2
SEQ = 4096
DEPTH = 2
DEC_BATCH = 32
DEC_SEQ = 4
PAST_LEN = 8192
PAGE_SIZE = 128

HEAD_DIM = 128
N_MIX_HEADS = D_MODEL // HEAD_DIM
H_A = N_MIX_HEADS // 2
HKV_A = max(1, H_A // 2)
H_B = N_MIX_HEADS - H_A
H_C = N_MIX_HEADS
H_IDX = 16
IDX_DIM = 64
TOPK_MAX = 256
N_META = 16
BLOCK_Q = 128
ROPE_THETA = 10000.0
N_EXPERTS = 32
N_GROUPS = 8
EXPERTS_PER_GROUP = N_EXPERTS // N_GROUPS
TOPK_EXPERTS = 2
D_EXPERT = D_MODEL // 4
LN_EPS = 1e-5
DEEPNORM_ALPHA = (2 * DEPTH) ** 0.25
DEEPNORM_BETA = (8 * DEPTH) ** -0.25
L0_WIDTHS = (H_A * HEAD_DIM, HKV_A * HEAD_DIM, HKV_A * HEAD_DIM,
             H_B * HEAD_DIM, H_B * HEAD_DIM, H_B * HEAD_DIM,
             H_IDX * IDX_DIM, IDX_DIM, H_IDX)
L1_WIDTHS = (H_C * HEAD_DIM, H_C * HEAD_DIM, H_C * HEAD_DIM, H_C)

kernel_name = 'hybrid_dsa_stickbreak_fox_moe_step'


def _split(p, widths):
    points = [int(v) for v in np.cumsum(widths)[:-1]]
    return jnp.split(p, points, axis=-1)


def _pad_seq(a, pad):
    return jnp.pad(a, [(0, 0), (0, pad)] + [(0, 0)] * (a.ndim - 2))


def _sweep(block_fn, n_blocks):
    out = lax.map(block_fn, jnp.arange(n_blocks))
    nb, b, blk, f = out.shape
    return jnp.moveaxis(out, 0, 1).reshape(b, nb * blk, f)


def rope(x, pos):
    half = x.shape[-1] // 2
    inv = ROPE_THETA ** (-jnp.arange(half, dtype=jnp.float32) / half)
    ang = pos.astype(jnp.float32)[:, None] * inv[None, :]
    cos = jnp.cos(ang)[:, None, :].astype(x.dtype)
    sin = jnp.sin(ang)[:, None, :].astype(x.dtype)
    x1, x2 = x[..., :half], x[..., half:]
    return jnp.concatenate([x1 * cos - x2 * sin, x2 * cos + x1 * sin], axis=-1)


def layer_norm(x, g, b):
    xf = x.astype(jnp.float32)
    mu = jnp.mean(xf, axis=-1, keepdims=True)
    var = jnp.mean(jnp.square(xf - mu), axis=-1, keepdims=True)
    return ((xf - mu) * lax.rsqrt(var + LN_EPS) * g + b).astype(x.dtype)


def dsa_select(iq, iw, ik, t_pos, s_pos, topk):
    dots = jnp.einsum('...thd,...sd->...ths', iq, ik).astype(jnp.float32) * (IDX_DIM ** -0.5)
    score = jnp.einsum('...ths,...th->...ts', jax.nn.relu(dots), iw.astype(jnp.float32))
    visible = s_pos[None, :] <= t_pos[:, None]
    score = jnp.where(visible, score, -jnp.inf)
    _, idx = lax.top_k(score, topk)
    valid = idx <= t_pos[:, None]
    return idx, valid


def gathered_attend(q, k_sel, v_sel, valid):
    lead = q.shape[:-3]
    t, h, d = q.shape[-3:]
    qg = q.reshape(*lead, t, HKV_A, h // HKV_A, d)
    s = jnp.einsum('...tkgd,...tskd->...tkgs', qg, k_sel).astype(jnp.float32) * (d ** -0.5)
    s = jnp.where(valid[..., None, None, :], s, -jnp.inf)
    p = jax.nn.softmax(s, axis=-1).astype(v_sel.dtype)
    o = jnp.einsum('...tkgs,...tskd->...tkgd', p, v_sel)
    return o.reshape(*lead, t, h * d)


def stick_breaking_attend(q, k, v, t_pos, s_pos):
    z = jnp.einsum('...thd,...shd->...hts', q, k).astype(jnp.float32) * (HEAD_DIM ** -0.5)
    strict = s_pos[None, :] < t_pos[:, None]
    log_stay = jnp.where(strict, jax.nn.log_sigmoid(-z), 0.0)
    log_after = lax.cumsum(log_stay, axis=z.ndim - 1, reverse=True) - log_stay
    w = jnp.where(strict, jnp.exp(jax.nn.log_sigmoid(z) + log_after), 0.0)
    o = jnp.einsum('...hts,...shd->...thd', w.astype(v.dtype), v)
    return o.reshape(*o.shape[:-2], -1)


def forgetting_attend(q, k, v, cq, ck, t_pos, s_pos):
    logits = jnp.einsum('...thd,...shd->...hts', q, k).astype(jnp.float32) * (HEAD_DIM ** -0.5)
    decay = jnp.swapaxes(cq, -1, -2)[..., :, None] - jnp.swapaxes(ck, -1, -2)[..., None, :]
    causal = s_pos[None, :] <= t_pos[:, None]
    logits = jnp.where(causal, logits + decay, -jnp.inf)
    p = jax.nn.softmax(logits, axis=-1).astype(v.dtype)
    o = jnp.einsum('...hts,...shd->...thd', p, v)
    return o.reshape(*o.shape[:-2], -1)


def project_even(x, w_in, pos):
    lead = x.shape[:-1]
    qa, ka, va, qb, kb, vb, iq, ik, iw = _split(x @ w_in, L0_WIDTHS)
    qa = rope(qa.reshape(*lead, H_A, HEAD_DIM), pos)
    ka = rope(ka.reshape(*lead, HKV_A, HEAD_DIM), pos)
    va = va.reshape(*lead, HKV_A, HEAD_DIM)
    qb = qb.reshape(*lead, H_B, HEAD_DIM)
    kb = kb.reshape(*lead, H_B, HEAD_DIM)
    vb = vb.reshape(*lead, H_B, HEAD_DIM)
    iq = rope(iq.reshape(*lead, H_IDX, IDX_DIM), pos)
    ik = rope(ik[..., None, :], pos)[..., 0, :]
    iw = iw * (H_IDX ** -0.5)
    kv_a = jnp.stack([ka, va], axis=-3)
    kv_b = jnp.stack([kb, vb], axis=-3)
    return qa, kv_a, qb, kv_b, iq, ik, iw


def project_odd(x, w_in, b_f):
    lead = x.shape[:-1]
    q, k, v, f = _split(x @ w_in, L1_WIDTHS)
    q = q.reshape(*lead, H_C, HEAD_DIM)
    kv = jnp.stack([k.reshape(*lead, H_C, HEAD_DIM), v.reshape(*lead, H_C, HEAD_DIM)], axis=-3)
    logf = jax.nn.log_sigmoid((f + b_f).astype(jnp.float32))
    return q, kv, logf


def even_mixer_prompt(x, w_in, w_out):
    b, t_len, _ = x.shape
    qa, kv_a, qb, kv_b, iq, ik, iw = project_even(x, w_in, jnp.arange(t_len))
    pad = (-t_len) % BLOCK_Q
    qa_p, kv_a_p, qb_p, kv_b_p, iq_p, ik_p, iw_p = [_pad_seq(a, pad) for a in (qa, kv_a, qb, kv_b, iq, ik, iw)]
    k_b, v_b = kv_b_p[:, :, 0], kv_b_p[:, :, 1]
    s_pos = jnp.arange(t_len + pad)
    topk = min(TOPK_MAX, SEQ // 4)
    b_idx = jnp.arange(b)[:, None, None]

    def block(n):
        start = n * BLOCK_Q
        t_pos = start + jnp.arange(BLOCK_Q)
        sl = lambda a: lax.dynamic_slice_in_dim(a, start, BLOCK_Q, axis=1)
        idx, valid = dsa_select(sl(iq_p), sl(iw_p), ik_p, t_pos, s_pos, topk)
        kv_sel = kv_a_p[b_idx, idx]
        o_a = gathered_attend(sl(qa_p), kv_sel[..., 0, :, :], kv_sel[..., 1, :, :], valid)
        o_b = stick_breaking_attend(sl(qb_p), k_b, v_b, t_pos, s_pos)
        return jnp.concatenate([o_a, o_b], axis=-1)

    o = _sweep(block, (t_len + pad) // BLOCK_Q)[:, :t_len]
    return o @ w_out, kv_a, ik, kv_b


def even_mixer_sample(x, w_in, w_out, cache_a_kv, cache_idx_k, cache_b_kv, page_table):
    db, s_len, _ = x.shape
    t_pos = PAST_LEN + jnp.arange(s_len)
    s_pos = jnp.arange(PAST_LEN + s_len)
    qa, kv_a, qb, kv_b, iq, ik, iw = project_even(x, w_in, t_pos)
    topk = min(TOPK_MAX, (PAST_LEN + s_len) // 4)
    b_idx = jnp.arange(db)[:, None, None]
    ik_past = cache_idx_k[page_table].reshape(db, PAST_LEN, IDX_DIM)
    ik_all = jnp.concatenate([ik_past, ik], axis=1)
    idx, valid = dsa_select(iq, iw, ik_all, t_pos, s_pos, topk)
    pidx = jnp.minimum(idx, PAST_LEN - 1)
    phys = page_table[b_idx, pidx // PAGE_SIZE]
    kv_past_sel = cache_a_kv[phys, pidx % PAGE_SIZE]
    kv_new_sel = kv_a[b_idx, jnp.clip(idx - PAST_LEN, 0, s_len - 1)]
    kv_sel = jnp.where((idx < PAST_LEN)[..., None, None, None], kv_past_sel, kv_new_sel)
    o_a = gathered_attend(qa, kv_sel[..., 0, :, :], kv_sel[..., 1, :, :], valid)

    def per_seq(args):
        pages, q_i, kv_i = args
        kv_all = jnp.concatenate([cache_b_kv[pages].reshape(PAST_LEN, 2, H_B, HEAD_DIM), kv_i], axis=0)
        return stick_breaking_attend(q_i, kv_all[:, 0], kv_all[:, 1], t_pos, s_pos)

    o_b = lax.map(per_seq, (page_table, qb, kv_b))
    return jnp.concatenate([o_a, o_b], axis=-1) @ w_out, kv_a, ik, kv_b


def odd_mixer_prompt(x, w_in, b_f, w_out):
    b, t_len, _ = x.shape
    q, kv, logf = project_odd(x, w_in, b_f)
    cum = jnp.cumsum(logf, axis=1)
    pad = (-t_len) % BLOCK_Q
    q_p, kv_p, cum_p = [_pad_seq(a, pad) for a in (q, kv, cum)]
    k_p, v_p = kv_p[:, :, 0], kv_p[:, :, 1]
    s_pos = jnp.arange(t_len + pad)

    def block(n):
        start = n * BLOCK_Q
        t_pos = start + jnp.arange(BLOCK_Q)
        sl = lambda a: lax.dynamic_slice_in_dim(a, start, BLOCK_Q, axis=1)
        return forgetting_attend(sl(q_p), k_p, v_p, sl(cum_p), cum_p, t_pos, s_pos)

    o = _sweep(block, (t_len + pad) // BLOCK_Q)[:, :t_len]
    return o @ w_out, kv, logf


def odd_mixer_sample(x, w_in, b_f, w_out, cache_kv, cache_logf, page_table):
    s_len = x.shape[1]
    q, kv, logf = project_odd(x, w_in, b_f)
    t_pos = PAST_LEN + jnp.arange(s_len)
    s_pos = jnp.arange(PAST_LEN + s_len)

    def per_seq(args):
        pages, q_i, kv_i, lf_i = args
        kv_all = jnp.concatenate([cache_kv[pages].reshape(PAST_LEN, 2, H_C, HEAD_DIM), kv_i], axis=0)
        lf_all = jnp.concatenate([cache_logf[pages].reshape(PAST_LEN, H_C).astype(jnp.float32), lf_i], axis=0)
        cum = jnp.cumsum(lf_all, axis=0)
        return forgetting_attend(q_i, kv_all[:, 0], kv_all[:, 1], cum[PAST_LEN:], cum, t_pos, s_pos)

    o = lax.map(per_seq, (page_table, q, kv, logf))
    return o @ w_out, kv, logf


def shared_router(x, w_router, b_router):
    n = x.shape[0]
    s = jax.nn.sigmoid((x @ w_router).astype(jnp.float32))
    s_sel = s + b_router.astype(jnp.float32)
    grp_score = lax.top_k(s_sel.reshape(n, N_GROUPS, EXPERTS_PER_GROUP), TOPK_EXPERTS)[0].sum(-1)
    g_best = jnp.argmax(grp_score, axis=-1)
    in_group = (jnp.arange(N_EXPERTS) // EXPERTS_PER_GROUP)[None, :] == g_best[:, None]
    _, e_idx = lax.top_k(jnp.where(in_group, s_sel, -jnp.inf), TOPK_EXPERTS)
    g = jnp.take_along_axis(s, e_idx, axis=-1)
    g = g / jnp.sum(g, axis=-1, keepdims=True)
    return jnp.sum(jax.nn.one_hot(e_idx, N_EXPERTS, dtype=jnp.float32) * g[..., None], axis=1)


def moe(x, gates, w_gate, w_up, w_down):
    y = jnp.zeros_like(x)
    for e in range(N_EXPERTS):
        h = jax.nn.silu(x @ w_gate[e]) * (x @ w_up[e])
        y = y + gates[:, e:e + 1].astype(x.dtype) * (h @ w_down[e])
    return y


def setup_inputs(seed: int = 0) -> dict:
    key = jax.random.key(seed)
    ks = jax.random.split(key, 24)
    f32 = jnp.float32
    n_pages = PAST_LEN // PAGE_SIZE
    n_used = DEC_BATCH * n_pages
    n_pool = n_used + max(1, n_used // 4)
    mix_width = (H_A + H_B) * HEAD_DIM
    nrm = lambda k, shape, scale=1.0: scale * jax.random.normal(k, shape, f32)
    page_table = jax.random.permutation(ks[7], n_pool)[:n_used].reshape(DEC_BATCH, n_pages).astype(jnp.int32)
    return {
        'x_prompt': nrm(ks[0], (BATCH, SEQ, D_MODEL)),
        'x_sample': nrm(ks[1], (DEC_BATCH, DEC_SEQ, D_MODEL)),
        'cache_l0_a_kv': nrm(ks[2], (n_pool, PAGE_SIZE, 2, HKV_A, HEAD_DIM)),
        'cache_l0_idx_k': nrm(ks[3], (n_pool, PAGE_SIZE, IDX_DIM)),
        'cache_l0_b_kv': nrm(ks[4], (n_pool, PAGE_SIZE, 2, H_B, HEAD_DIM)),
        'cache_l1_c_kv': nrm(ks[5], (n_pool, PAGE_SIZE, 2, H_C, HEAD_DIM)),
        'cache_l1_logf': jax.nn.log_sigmoid(nrm(ks[6], (n_pool, PAGE_SIZE, H_C)) + 3.5),
        'page_table': page_table,
        'meta_tokens': nrm(ks[8], (N_META, D_MODEL)),
        'w_in_l0': nrm(ks[9], (D_MODEL, sum(L0_WIDTHS)), D_MODEL ** -0.5),
        'w_out_l0': nrm(ks[10], (mix_width, D_MODEL), DEEPNORM_BETA * mix_width ** -0.5),
        'w_in_l1': nrm(ks[11], (D_MODEL, sum(L1_WIDTHS)), D_MODEL ** -0.5),
        'b_forget_l1': jax.random.uniform(ks[12], (H_C,), f32, 1.0, 6.0),
        'w_out_l1': nrm(ks[13], (H_C * HEAD_DIM, D_MODEL), DEEPNORM_BETA * (H_C * HEAD_DIM) ** -0.5),
        'ln_mix_g': 1.0 + nrm(ks[14], (DEPTH, D_MODEL), 0.02),
        'ln_mix_b': nrm(ks[15], (DEPTH, D_MODEL), 0.02),
        'ln_ffn_g': 1.0 + nrm(ks[16], (DEPTH, D_MODEL), 0.02),
        'ln_ffn_b': nrm(ks[17], (DEPTH, D_MODEL), 0.02),
        'w_router': nrm(ks[18], (D_MODEL, N_EXPERTS), D_MODEL ** -0.5),
        'b_router': nrm(ks[19], (N_EXPERTS,), 0.01),
        'w_gate': nrm(ks[20], (DEPTH, N_EXPERTS, D_MODEL, D_EXPERT), D_MODEL ** -0.5),
        'w_up': nrm(ks[21], (DEPTH, N_EXPERTS, D_MODEL, D_EXPERT), D_MODEL ** -0.5),
        'w_down': nrm(ks[22], (DEPTH, N_EXPERTS, D_EXPERT, D_MODEL), DEEPNORM_BETA * D_EXPERT ** -0.5),
    }


def reference(x_prompt, x_sample, cache_l0_a_kv, cache_l0_idx_k, cache_l0_b_kv, cache_l1_c_kv,
              cache_l1_logf, page_table, meta_tokens, w_in_l0, w_out_l0, w_in_l1, b_forget_l1,
              w_out_l1, ln_mix_g, ln_mix_b, ln_ffn_g, ln_ffn_b, w_router, b_router, w_gate, w_up,
              w_down):
    b = x_prompt.shape[0]
    meta = jnp.broadcast_to(meta_tokens[None], (b, N_META, D_MODEL)).astype(x_prompt.dtype)
    hp = jnp.concatenate([meta, x_prompt], axis=1)
    hs = x_sample
    t_len = hp.shape[1]
    for layer in range(DEPTH):
        if layer % 2 == 0:
            mp, a_kv_p, idx_k_p, b_kv_p = even_mixer_prompt(hp, w_in_l0, w_out_l0)
            ms, a_kv_s, idx_k_s, b_kv_s = even_mixer_sample(hs, w_in_l0, w_out_l0, cache_l0_a_kv,
                                                             cache_l0_idx_k, cache_l0_b_kv, page_table)
        else:
            mp, c_kv_p, logf_p = odd_mixer_prompt(hp, w_in_l1, b_forget_l1, w_out_l1)
            ms, c_kv_s, logf_s = odd_mixer_sample(hs, w_in_l1, b_forget_l1, w_out_l1, cache_l1_c_kv,
                                                  cache_l1_logf, page_table)
        hp = layer_norm(DEEPNORM_ALPHA * hp + mp, ln_mix_g[layer], ln_mix_b[layer])
        hs = layer_norm(DEEPNORM_ALPHA * hs + ms, ln_mix_g[layer], ln_mix_b[layer])
        flat = jnp.concatenate([hp.reshape(-1, D_MODEL), hs.reshape(-1, D_MODEL)], axis=0)
        gates = shared_router(flat, w_router, b_router)
        f = moe(flat, gates, w_gate[layer], w_up[layer], w_down[layer])
        flat = layer_norm(DEEPNORM_ALPHA * flat + f, ln_ffn_g[layer], ln_ffn_b[layer])
        n_p = b * t_len
        hp = flat[:n_p].reshape(b, t_len, D_MODEL)
        hs = flat[n_p:].reshape(hs.shape)
    return (hp[:, N_META:], hs, a_kv_p, a_kv_s, idx_k_p, idx_k_s, b_kv_p, b_kv_s,
            c_kv_p, c_kv_s, logf_p, logf_s)
```

```python
import functools

import numpy as np
import jax
import jax.numpy as jnp
from jax import lax
from jax.experimental import pallas as pl
from jax.experimental.pallas import tpu as pltpu

HEAD_DIM = 128
H_A = 8
HKV_A = 4
H_B = 8
H_C = 16
H_IDX = 16
IDX_DIM = 64
TOPK_MAX = 256
N_META = 16
BLK = 128
ROPE_THETA = 10000.0
N_EXPERTS = 32
N_GROUPS = 8
EPG = N_EXPERTS // N_GROUPS
LN_EPS = 1e-5
DEPTH = 2
ALPHA = (2 * DEPTH) ** 0.25
NEG = -1e30
INT_MIN = -2 ** 31
KEY_NEG_INF = -2139095041
LANES = 128
TN = 256
XCOLS = 128
VMEM_LIMIT = 56 * 1024 * 1024

F32 = jnp.float32
BF16 = jnp.bfloat16
NT_DIMS = (((1,), (1,)), ((), ()))


def _cdiv(a, b):
    return (a + b - 1) // b


def _dot(a, b):
    return jnp.dot(a, b, preferred_element_type=F32)


def _dot_nt(a, b):
    return lax.dot_general(a, b, NT_DIMS, preferred_element_type=F32)


def _split2(x):
    hi = x.astype(BF16)
    lo = (x - hi.astype(F32)).astype(BF16)
    return hi, lo


def _split3(x):
    hi = x.astype(BF16)
    r = x - hi.astype(F32)
    mid = r.astype(BF16)
    lo = (r - mid.astype(F32)).astype(BF16)
    return hi, mid, lo


def _neg_softplus(z):
    return -(jnp.maximum(z, 0.0) + jnp.log1p(jnp.exp(-jnp.abs(z))))


def _params(sem, vmem=VMEM_LIMIT):
    return pltpu.CompilerParams(dimension_semantics=sem, vmem_limit_bytes=vmem)


def _rope128(a, c, s):
    return a * c + pltpu.roll(a, 64, 1) * s


def _rope64(a, c, s):
    lane = lax.broadcasted_iota(jnp.int32, a.shape, 1)
    first = (lane % 64) < 32
    partner = jnp.where(first, pltpu.roll(a, 96, 1), pltpu.roll(a, 32, 1))
    return a * c + partner * s


def _inproj_rope_kernel(modes_ref, x_ref, w_ref, c128_ref, s128_ref, c64_ref, s64_ref, o_ref, xb_ref):
    j = pl.program_id(1)

    @pl.when(j == 0)
    def _():
        xb_ref[...] = x_ref[...].astype(BF16)

    acc = _dot(xb_ref[...], w_ref[...])
    mode = modes_ref[j]

    @pl.when(mode == 0)
    def _():
        o_ref[...] = acc

    @pl.when(mode == 1)
    def _():
        for c in range(TN // LANES):
            sl = slice(c * LANES, (c + 1) * LANES)
            o_ref[:, sl] = _rope128(acc[:, sl], c128_ref[:, sl], s128_ref[:, sl])

    @pl.when(mode == 2)
    def _():
        for c in range(TN // LANES):
            sl = slice(c * LANES, (c + 1) * LANES)
            o_ref[:, sl] = _rope64(acc[:, sl], c64_ref[:, sl], s64_ref[:, sl])

    @pl.when(mode == 3)
    def _():
        sl = slice(0, LANES)
        o_ref[:, sl] = _rope64(acc[:, sl], c64_ref[:, sl], s64_ref[:, sl])
        o_ref[:, LANES:] = acc[:, LANES:]


def _inproj_plain_kernel(x_ref, w_ref, o_ref, xb_ref):
    @pl.when(pl.program_id(1) == 0)
    def _():
        xb_ref[...] = x_ref[...].astype(BF16)

    o_ref[...] = _dot(xb_ref[...], w_ref[...])


def _inproj(x, w_bf16, d_model, tm, modes=None, tables=None):
    nf = x.shape[0]
    ncols = w_bf16.shape[1]
    grid = (_cdiv(nf, tm), ncols // TN)
    scratch = [pltpu.VMEM((tm, d_model), BF16)]
    out_shape = jax.ShapeDtypeStruct((nf, ncols), F32)
    if modes is None:
        return pl.pallas_call(
            _inproj_plain_kernel,
            grid=grid,
            in_specs=[pl.BlockSpec((tm, d_model), lambda i, j: (i, 0)),
                      pl.BlockSpec((d_model, TN), lambda i, j: (0, j))],
            out_specs=pl.BlockSpec((tm, TN), lambda i, j: (i, j)),
            out_shape=out_shape,
            scratch_shapes=scratch,
            compiler_params=_params(("parallel", "arbitrary")),
            name="inproj_plain",
        )(x, w_bf16)
    tab_spec = pl.BlockSpec((tm, TN), lambda i, j, m: (i, 0))
    return pl.pallas_call(
        _inproj_rope_kernel,
        grid_spec=pltpu.PrefetchScalarGridSpec(
            num_scalar_prefetch=1,
            grid=grid,
            in_specs=[pl.BlockSpec((tm, d_model), lambda i, j, m: (i, 0)),
                      pl.BlockSpec((d_model, TN), lambda i, j, m: (0, j)),
                      tab_spec, tab_spec, tab_spec, tab_spec],
            out_specs=pl.BlockSpec((tm, TN), lambda i, j, m: (i, j)),
            scratch_shapes=scratch),
        out_shape=out_shape,
        compiler_params=_params(("parallel", "arbitrary")),
        name="inproj_rope",
    )(modes, x, w_bf16, *tables)


def _sortable_key(score):
    score = jnp.where(score == 0.0, 0.0, score)
    bits = lax.bitcast_convert_type(score, jnp.int32)
    return bits ^ ((bits >> 31) & 0x7FFFFFFF)


def _count(mask):
    return jnp.sum(mask.astype(F32), axis=1, keepdims=True)


def _topk_select(key_ref, vis, s_pos, topk, n_cols, thr_ref, need_ref, jb_ref):
    kf = float(topk)
    c0 = _count(key_ref[...] >= 0)
    ans0 = jnp.where(c0 >= kf, 0, INT_MIN).astype(jnp.int32)

    def body(i, ans):
        cand = ans | jnp.left_shift(jnp.int32(1), 30 - i)
        cnt = _count(key_ref[...] >= cand)
        return jnp.where(cnt >= kf, cand, ans)

    thr = lax.fori_loop(0, 31, body, ans0)
    key = key_ref[...]
    need = kf - _count(key > thr)
    n_eq = _count((key == thr) & vis)
    thr_ref[...] = thr
    need_ref[...] = need
    jb_ref[...] = jnp.full(jb_ref.shape, n_cols, jnp.int32)
    n_bits = int(np.ceil(np.log2(n_cols))) + 1

    @pl.when(jnp.max(n_eq - need) > 0.0)
    def _():
        def body2(i, ans):
            cand = ans | jnp.left_shift(jnp.int32(1), n_bits - 1 - i)
            eqv = (key_ref[...] == thr_ref[...]) & vis
            c = _count(eqv & (s_pos < cand))
            return jnp.where(c < need_ref[...], cand, ans)

        jb_ref[...] = lax.fori_loop(0, n_bits, body2, jnp.zeros(jb_ref.shape, jnp.int32))

    return vis & ((key > thr) | ((key == thr) & (s_pos <= jb_ref[...])))


def _idx_prompt_kernel(iq_ref, iw_ref, ik_ref, bias_ref, ikd_ref, key_ref, thr_ref, need_ref, jb_ref,
                       *, topk, tp):
    qi = pl.program_id(1)

    @pl.when(qi == 0)
    def _():
        ik = ik_ref[...]
        ikd_ref[...] = (ik + pltpu.roll(ik, 64, 1)).astype(BF16)

    lane = lax.broadcasted_iota(jnp.int32, (1, LANES), 1)
    score = jnp.zeros((BLK, tp), F32)
    for h in range(H_IDX):
        pair = iq_ref[:, (h // 2) * LANES:(h // 2 + 1) * LANES]
        lo = (h % 2) * IDX_DIM
        qh = jnp.where((lane >= lo) & (lane < lo + IDX_DIM), pair, 0.0).astype(BF16)
        dots = _dot_nt(qh, ikd_ref[...]) * (IDX_DIM ** -0.5)
        score = score + jnp.maximum(dots, 0.0) * (iw_ref[:, h:h + 1] * (H_IDX ** -0.5))

    t_pos = qi * BLK + lax.broadcasted_iota(jnp.int32, (BLK, 1), 0)
    s_pos = lax.broadcasted_iota(jnp.int32, (1, tp), 1)
    vis = s_pos <= t_pos
    key_ref[...] = jnp.where(vis, _sortable_key(score), KEY_NEG_INF)
    sel = _topk_select(key_ref, vis, s_pos, topk, tp, thr_ref, need_ref, jb_ref)
    bias_ref[...] = jnp.where(sel, 0.0, NEG).astype(BF16)


def _idx_prompt(p0, n_batch, tp, topk):
    nqb = tp // BLK
    return pl.pallas_call(
        functools.partial(_idx_prompt_kernel, topk=topk, tp=tp),
        grid=(n_batch, nqb),
        in_specs=[pl.BlockSpec((BLK, 1024), lambda b, q: (b * nqb + q, 1)),
                  pl.BlockSpec((BLK, LANES), lambda b, q: (b * nqb + q, 49)),
                  pl.BlockSpec((tp, LANES), lambda b, q: (b, 48))],
        out_specs=pl.BlockSpec((BLK, tp), lambda b, q: (b * nqb + q, 0)),
        out_shape=jax.ShapeDtypeStruct((n_batch * tp, tp), BF16),
        scratch_shapes=[pltpu.VMEM((tp, LANES), BF16),
                        pltpu.VMEM((BLK, tp), jnp.int32),
                        pltpu.VMEM((BLK, 1), jnp.int32),
                        pltpu.VMEM((BLK, 1), F32),
                        pltpu.VMEM((BLK, 1), jnp.int32)],
        compiler_params=_params(("parallel", "arbitrary")),
        name="dsa_index_prompt",
    )(p0, p0, p0)


def _attn_a_prompt_kernel(q_ref, k_ref, v_ref, bias_ref, o_ref, kb_ref, vb_ref):
    @pl.when(pl.program_id(2) == 0)
    def _():
        kb_ref[...] = k_ref[...].astype(BF16)
        vb_ref[...] = v_ref[...].astype(BF16)

    bias = bias_ref[...].astype(F32)
    for g in range(H_A // HKV_A):
        sl = slice(g * HEAD_DIM, (g + 1) * HEAD_DIM)
        s = _dot_nt(q_ref[:, sl].astype(BF16), kb_ref[...]) * (HEAD_DIM ** -0.5) + bias
        m = jnp.max(s, axis=1, keepdims=True)
        p = jnp.exp(s - m)
        l = jnp.sum(p, axis=1, keepdims=True)
        o_ref[:, sl] = _dot(p.astype(BF16), vb_ref[...]) / l


def _attn_a_prompt(p0, bias, n_batch, tp):
    nqb = tp // BLK
    gw = (H_A // HKV_A) * HEAD_DIM
    return pl.pallas_call(
        _attn_a_prompt_kernel,
        grid=(n_batch, HKV_A, nqb),
        in_specs=[pl.BlockSpec((BLK, gw), lambda b, k, q: (b * nqb + q, k)),
                  pl.BlockSpec((tp, HEAD_DIM), lambda b, k, q: (b, 16 + k)),
                  pl.BlockSpec((tp, HEAD_DIM), lambda b, k, q: (b, 20 + k)),
                  pl.BlockSpec((BLK, tp), lambda b, k, q: (b * nqb + q, 0))],
        out_specs=pl.BlockSpec((BLK, gw), lambda b, k, q: (b * nqb + q, k)),
        out_shape=jax.ShapeDtypeStruct((n_batch * tp, H_A * HEAD_DIM), F32),
        scratch_shapes=[pltpu.VMEM((tp, HEAD_DIM), BF16), pltpu.VMEM((tp, HEAD_DIM), BF16)],
        compiler_params=_params(("parallel", "parallel", "arbitrary")),
        name="dsa_attend_prompt",
    )(p0, p0, p0, bias)


def _suffix_matrix(n):
    r = lax.broadcasted_iota(jnp.int32, (n, n), 0)
    c = lax.broadcasted_iota(jnp.int32, (n, n), 1)
    return (r > c).astype(BF16)


def _stick_block(z, strict, carry, v_bf16, suffix):
    lsn = _neg_softplus(z)
    log_stay = jnp.where(strict, lsn, 0.0)
    hi, lo = _split2(log_stay)
    after = carry + _dot(hi, suffix) + _dot(lo, suffix)
    w = jnp.where(strict, jnp.exp(z + lsn + after), 0.0)
    return _dot(w.astype(BF16), v_bf16), jnp.sum(log_stay, axis=1, keepdims=True)


def _attn_b_prompt_kernel(q_ref, k_ref, v_ref, o_ref, kb_ref, vb_ref, acc_ref, carry_ref):
    qi = pl.program_id(2)

    @pl.when(qi == 0)
    def _():
        kb_ref[...] = k_ref[...].astype(BF16)
        vb_ref[...] = v_ref[...].astype(BF16)

    q = q_ref[...].astype(BF16)
    acc_ref[...] = jnp.zeros_like(acc_ref)
    carry_ref[...] = jnp.zeros_like(carry_ref)
    r = lax.broadcasted_iota(jnp.int32, (BLK, BLK), 0)
    c = lax.broadcasted_iota(jnp.int32, (BLK, BLK), 1)
    suffix = _suffix_matrix(BLK)

    def body(it, carry_unused):
        kb = qi - it
        off = pl.multiple_of(kb * BLK, BLK)
        z = _dot_nt(q, kb_ref[pl.ds(off, BLK), :]) * (HEAD_DIM ** -0.5)
        strict = (kb * BLK + c) < (qi * BLK + r)
        o, ssum = _stick_block(z, strict, carry_ref[...], vb_ref[pl.ds(off, BLK), :], suffix)
        acc_ref[...] += o
        carry_ref[...] += ssum
        return carry_unused

    lax.fori_loop(0, qi + 1, body, 0)
    o_ref[...] = acc_ref[...]


def _attn_b_prompt(p0, n_batch, tp):
    nqb = tp // BLK
    return pl.pallas_call(
        _attn_b_prompt_kernel,
        grid=(n_batch, H_B, nqb),
        in_specs=[pl.BlockSpec((BLK, HEAD_DIM), lambda b, h, q: (b * nqb + q, 24 + h)),
                  pl.BlockSpec((tp, HEAD_DIM), lambda b, h, q: (b, 32 + h)),
                  pl.BlockSpec((tp, HEAD_DIM), lambda b, h, q: (b, 40 + h))],
        out_specs=pl.BlockSpec((BLK, HEAD_DIM), lambda b, h, q: (b * nqb + q, h)),
        out_shape=jax.ShapeDtypeStruct((n_batch * tp, H_B * HEAD_DIM), F32),
        scratch_shapes=[pltpu.VMEM((tp, HEAD_DIM), BF16), pltpu.VMEM((tp, HEAD_DIM), BF16),
                        pltpu.VMEM((BLK, HEAD_DIM), F32), pltpu.VMEM((BLK, 1), F32)],
        compiler_params=_params(("parallel", "parallel", "arbitrary")),
        name="stickbreak_prompt",
    )(p0, p0, p0)


def _logf_kernel(f_ref, bf_ref, logf_ref, cum_ref, carry_ref, *, blocks_per_seq):
    i = pl.program_id(0)

    @pl.when(i % blocks_per_seq == 0)
    def _():
        carry_ref[...] = jnp.zeros_like(carry_ref)

    x = f_ref[...] + bf_ref[...]
    logf = _neg_softplus(-x)
    logf_ref[...] = logf
    r = lax.broadcasted_iota(jnp.int32, (BLK, BLK), 0)
    c = lax.broadcasted_iota(jnp.int32, (BLK, BLK), 1)
    prefix = (c <= r).astype(BF16)
    hi, mid, lo = _split3(logf)
    cum = carry_ref[...] + _dot(prefix, hi) + _dot(prefix, mid) + _dot(prefix, lo)
    cum_ref[...] = cum
    carry_ref[...] = cum[BLK - 1:BLK, :]


def _logf(p1, bf_row, n_rows, blocks_per_seq):
    return pl.pallas_call(
        functools.partial(_logf_kernel, blocks_per_seq=blocks_per_seq),
        grid=(_cdiv(n_rows, BLK),),
        in_specs=[pl.BlockSpec((BLK, LANES), lambda i: (i, 48)),
                  pl.BlockSpec((1, LANES), lambda i: (0, 0))],
        out_specs=[pl.BlockSpec((BLK, LANES), lambda i: (i, 0)),
                   pl.BlockSpec((BLK, LANES), lambda i: (i, 0))],
        out_shape=[jax.ShapeDtypeStruct((n_rows, LANES), F32),
                   jax.ShapeDtypeStruct((n_rows, LANES), F32)],
        scratch_shapes=[pltpu.VMEM((1, LANES), F32)],
        compiler_params=_params(("arbitrary",)),
        name="log_forget_cumsum",
    )(p1, bf_row)


def _online_softmax_step(s, v_bf16, m_ref, l_ref, acc_ref):
    m_new = jnp.maximum(m_ref[...], jnp.max(s, axis=1, keepdims=True))
    corr = jnp.exp(m_ref[...] - m_new)
    p = jnp.exp(s - m_new)
    l_ref[...] = l_ref[...] * corr + jnp.sum(p, axis=1, keepdims=True)
    acc_ref[...] = acc_ref[...] * corr + _dot(p.astype(BF16), v_bf16)
    m_ref[...] = m_new


def _attn_c_prompt_kernel(q_ref, k_ref, v_ref, cq_ref, ck_ref, o_ref, kb_ref, vb_ref, m_ref, l_ref, acc_ref):
    qi = pl.program_id(2)

    @pl.when(qi == 0)
    def _():
        kb_ref[...] = k_ref[...].astype(BF16)
        vb_ref[...] = v_ref[...].astype(BF16)

    q = q_ref[...].astype(BF16)
    cq = cq_ref[0, 0]
    m_ref[...] = jnp.full(m_ref.shape, NEG, F32)
    l_ref[...] = jnp.zeros_like(l_ref)
    acc_ref[...] = jnp.zeros_like(acc_ref)
    r = lax.broadcasted_iota(jnp.int32, (BLK, BLK), 0)
    c = lax.broadcasted_iota(jnp.int32, (BLK, BLK), 1)

    def body(kb, carry_unused):
        off = pl.multiple_of(kb * BLK, BLK)
        ck = ck_ref[0, 0, :, pl.ds(off, BLK)]
        s = _dot_nt(q, kb_ref[pl.ds(off, BLK), :]) * (HEAD_DIM ** -0.5) + (cq - ck)
        s = jnp.where((kb * BLK + c) <= (qi * BLK + r), s, NEG)
        _online_softmax_step(s, vb_ref[pl.ds(off, BLK), :], m_ref, l_ref, acc_ref)
        return carry_unused

    lax.fori_loop(0, qi + 1, body, 0)
    o_ref[...] = acc_ref[...] / l_ref[...]


def _attn_c_prompt(p1, cum_col, cum_row, n_batch, tp):
    nqb = tp // BLK
    return pl.pallas_call(
        _attn_c_prompt_kernel,
        grid=(n_batch, H_C, nqb),
        in_specs=[pl.BlockSpec((BLK, HEAD_DIM), lambda b, h, q: (b * nqb + q, h)),
                  pl.BlockSpec((tp, HEAD_DIM), lambda b, h, q: (b, 16 + h)),
                  pl.BlockSpec((tp, HEAD_DIM), lambda b, h, q: (b, 32 + h)),
                  pl.BlockSpec((1, 1, BLK, 1), lambda b, h, q: (b, h, q, 0)),
                  pl.BlockSpec((1, 1, 1, tp), lambda b, h, q: (b, h, 0, 0))],
        out_specs=pl.BlockSpec((BLK, HEAD_DIM), lambda b, h, q: (b * nqb + q, h)),
        out_shape=jax.ShapeDtypeStruct((n_batch * tp, H_C * HEAD_DIM), F32),
        scratch_shapes=[pltpu.VMEM((tp, HEAD_DIM), BF16), pltpu.VMEM((tp, HEAD_DIM), BF16),
                        pltpu.VMEM((BLK, 1), F32), pltpu.VMEM((BLK, 1), F32),
                        pltpu.VMEM((BLK, HEAD_DIM), F32)],
        compiler_params=_params(("parallel", "parallel", "arbitrary")),
        name="forget_attend_prompt",
    )(p1, p1, p1, cum_col, cum_row)


def _idx_sample_kernel(pt_ref, iq_ref, iw_ref, ikn_ref, page_ref, bias_ref,
                       score_ref, key_ref, thr_ref, need_ref, jb_ref, *, topk, n_pages, ds):
    j = pl.program_id(1)
    past = n_pages * BLK
    ncol = past + BLK
    nq = iq_ref.shape[1] // H_IDX
    q = iq_ref[0].astype(BF16)
    wgt = iw_ref[0] * (H_IDX ** -0.5)

    def scores(keys):
        dots = _dot_nt(q, keys.astype(BF16)) * (IDX_DIM ** -0.5)
        wd = jnp.maximum(dots, 0.0) * wgt
        return jnp.sum(wd.reshape(nq, H_IDX, BLK), axis=1)

    score_ref[:, pl.ds(pl.multiple_of(j * BLK, BLK), BLK)] = scores(page_ref[0])

    @pl.when(j == n_pages - 1)
    def _():
        score_ref[:, past:] = scores(ikn_ref[0])
        t_idx = lax.broadcasted_iota(jnp.int32, (nq, 1), 0) % ds
        s_pos = lax.broadcasted_iota(jnp.int32, (1, ncol), 1)
        vis = s_pos <= past + t_idx
        key_ref[...] = jnp.where(vis, _sortable_key(score_ref[...]), KEY_NEG_INF)
        sel = _topk_select(key_ref, vis, s_pos, topk, ncol, thr_ref, need_ref, jb_ref)
        bias_ref[0] = jnp.where(sel, 0.0, NEG)


def _idx_sample(page_table, iq_th, iw_th, ik_new, cache_idx, topk, ds):
    db, n_pages = page_table.shape
    ncol = n_pages * BLK + BLK
    rows = iq_th.shape[1]
    nq = rows // H_IDX
    return pl.pallas_call(
        functools.partial(_idx_sample_kernel, topk=topk, n_pages=n_pages, ds=ds),
        grid_spec=pltpu.PrefetchScalarGridSpec(
            num_scalar_prefetch=1,
            grid=(db, n_pages),
            in_specs=[pl.BlockSpec((1, rows, IDX_DIM), lambda b, j, pt: (b, 0, 0)),
                      pl.BlockSpec((1, rows, 1), lambda b, j, pt: (b, 0, 0)),
                      pl.BlockSpec((1, BLK, IDX_DIM), lambda b, j, pt: (b, 0, 0)),
                      pl.BlockSpec((1, BLK, IDX_DIM), lambda b, j, pt: (pt[b, j], 0, 0))],
            out_specs=pl.BlockSpec((1, nq, ncol), lambda b, j, pt: (b, 0, 0)),
            scratch_shapes=[pltpu.VMEM((nq, ncol), F32),
                            pltpu.VMEM((nq, ncol), jnp.int32),
                            pltpu.VMEM((nq, 1), jnp.int32),
                            pltpu.VMEM((nq, 1), F32),
                            pltpu.VMEM((nq, 1), jnp.int32)]),
        out_shape=jax.ShapeDtypeStruct((db, nq, ncol), F32),
        compiler_params=_params(("parallel", "arbitrary")),
        name="dsa_index_sample",
    )(page_table, iq_th, iw_th, ik_new, cache_idx)


NEW_ROWS = 8


def _head_rows(ref, slot, n_slots, is_page):
    if is_page:
        return ref[0, pl.ds(slot, BLK, stride=n_slots), :].astype(BF16)
    rows = ref[0, pl.ds(slot, NEW_ROWS, stride=n_slots), :]
    return jnp.concatenate([rows, jnp.zeros((BLK - NEW_ROWS, HEAD_DIM), F32)], axis=0).astype(BF16)


def _attn_a_sample_kernel(pt_ref, q_ref, bias_ref, new_ref, page_ref, o_ref, m_ref, l_ref, acc_ref,
                          *, n_pages, rq):
    j = pl.program_id(1)
    n_slots = 2 * HKV_A

    @pl.when(j == 0)
    def _():
        m_ref[...] = jnp.full(m_ref.shape, NEG, F32)
        l_ref[...] = jnp.zeros_like(l_ref)
        acc_ref[...] = jnp.zeros_like(acc_ref)

    def attend(ref, is_page):
        bias = bias_ref[0]
        for k in range(HKV_A):
            s = _dot_nt(q_ref[0, k].astype(BF16), _head_rows(ref, k, n_slots, is_page))
            s = s * (HEAD_DIM ** -0.5) + bias
            _online_softmax_step(s, _head_rows(ref, HKV_A + k, n_slots, is_page),
                                 m_ref.at[k], l_ref.at[k], acc_ref.at[k])

    @pl.when(j < n_pages)
    def _():
        attend(page_ref, True)

    @pl.when(j == n_pages)
    def _():
        attend(new_ref, False)
        o_ref[0] = acc_ref[...] / l_ref[...]


def _attn_a_sample(page_table, q_kgt, bias, new_rows, cache_flat, ds):
    db, n_pages = page_table.shape
    rq = (H_A // HKV_A) * ds
    n_slots = 2 * HKV_A
    last = n_pages - 1
    return pl.pallas_call(
        functools.partial(_attn_a_sample_kernel, n_pages=n_pages, rq=rq),
        grid_spec=pltpu.PrefetchScalarGridSpec(
            num_scalar_prefetch=1,
            grid=(db, n_pages + 1),
            in_specs=[pl.BlockSpec((1, HKV_A, rq, HEAD_DIM), lambda b, j, pt: (b, 0, 0, 0)),
                      pl.BlockSpec((1, rq, BLK), lambda b, j, pt: (b, 0, j)),
                      pl.BlockSpec((1, NEW_ROWS * n_slots, HEAD_DIM), lambda b, j, pt: (b, 0, 0)),
                      pl.BlockSpec((1, BLK * n_slots, HEAD_DIM),
                                   lambda b, j, pt: (pt[b, jnp.minimum(j, last)], 0, 0))],
            out_specs=pl.BlockSpec((1, HKV_A, rq, HEAD_DIM), lambda b, j, pt: (b, 0, 0, 0)),
            scratch_shapes=[pltpu.VMEM((HKV_A, rq, 1), F32), pltpu.VMEM((HKV_A, rq, 1), F32),
                            pltpu.VMEM((HKV_A, rq, HEAD_DIM), F32)]),
        out_shape=jax.ShapeDtypeStruct((db, HKV_A, rq, HEAD_DIM), F32),
        compiler_params=_params(("parallel", "arbitrary")),
        name="dsa_attend_sample",
    )(page_table, q_kgt, bias, new_rows, cache_flat)


def _attn_b_sample_kernel(pt_ref, q_ref, new_ref, page_ref, o_ref, acc_ref, carry_ref, *, n_pages, ds):
    j = pl.program_id(1)
    n_slots = 2 * H_B

    t_idx = lax.broadcasted_iota(jnp.int32, (NEW_ROWS, BLK), 0)
    s_idx = lax.broadcasted_iota(jnp.int32, (NEW_ROWS, BLK), 1)

    def attend(ref, is_page, strict):
        suffix = _suffix_matrix(BLK)
        for h in range(H_B):
            z = _dot_nt(q_ref[0, h].astype(BF16), _head_rows(ref, h, n_slots, is_page)) * (HEAD_DIM ** -0.5)
            o, ssum = _stick_block(z, strict, carry_ref[h], _head_rows(ref, H_B + h, n_slots, is_page), suffix)
            acc_ref[h] += o
            carry_ref[h] += ssum

    @pl.when(j == 0)
    def _():
        acc_ref[...] = jnp.zeros_like(acc_ref)
        carry_ref[...] = jnp.zeros_like(carry_ref)
        attend(new_ref, False, (s_idx < t_idx) & (s_idx < ds))

    @pl.when(j > 0)
    def _():
        attend(page_ref, True, s_idx >= 0)

    @pl.when(j == n_pages)
    def _():
        o_ref[0] = acc_ref[...]


def _attn_b_sample(page_table, q_h8, new_rows, cache_flat, ds):
    db, n_pages = page_table.shape
    n_slots = 2 * H_B
    return pl.pallas_call(
        functools.partial(_attn_b_sample_kernel, n_pages=n_pages, ds=ds),
        grid_spec=pltpu.PrefetchScalarGridSpec(
            num_scalar_prefetch=1,
            grid=(db, n_pages + 1),
            in_specs=[pl.BlockSpec((1, H_B, 8, HEAD_DIM), lambda b, j, pt: (b, 0, 0, 0)),
                      pl.BlockSpec((1, NEW_ROWS * n_slots, HEAD_DIM), lambda b, j, pt: (b, 0, 0)),
                      pl.BlockSpec((1, BLK * n_slots, HEAD_DIM),
                                   lambda b, j, pt: (pt[b, n_pages - jnp.maximum(j, 1)], 0, 0))],
            out_specs=pl.BlockSpec((1, H_B, 8, HEAD_DIM), lambda b, j, pt: (b, 0, 0, 0)),
            scratch_shapes=[pltpu.VMEM((H_B, 8, HEAD_DIM), F32), pltpu.VMEM((H_B, 8, 1), F32)]),
        out_shape=jax.ShapeDtypeStruct((db, H_B, 8, HEAD_DIM), F32),
        compiler_params=_params(("parallel", "arbitrary")),
        name="stickbreak_sample",
    )(page_table, q_h8, new_rows, cache_flat)


def _attn_c_sample_kernel(pt_ref, q_ref, lfrow_ref, lfcol_ref, new_ref, lfpage_ref, page_ref, o_ref,
                          m_ref, l_ref, acc_ref, carry_ref, *, n_pages, ds):
    j = pl.program_id(1)
    n_slots = 2 * H_C
    t_idx = lax.broadcasted_iota(jnp.int32, (NEW_ROWS, BLK), 0)
    s_idx = lax.broadcasted_iota(jnp.int32, (NEW_ROWS, BLK), 1)

    def cum_new_col(h):
        return jnp.sum(jnp.where(s_idx <= t_idx, lfrow_ref[0, h], 0.0), axis=1, keepdims=True)

    @pl.when(j == 0)
    def _():
        m_ref[...] = jnp.full(m_ref.shape, NEG, F32)
        l_ref[...] = jnp.zeros_like(l_ref)
        acc_ref[...] = jnp.zeros_like(acc_ref)
        carry_ref[...] = jnp.zeros_like(carry_ref)
        causal = (s_idx <= t_idx) & (s_idx < ds)
        for h in range(H_C):
            cum_row = jnp.sum(jnp.where(t_idx <= s_idx, lfcol_ref[0, h], 0.0), axis=0, keepdims=True)
            s = _dot_nt(q_ref[0, h].astype(BF16), _head_rows(new_ref, h, n_slots, False)) * (HEAD_DIM ** -0.5)
            s = jnp.where(causal, s + (cum_new_col(h) - cum_row), NEG)
            _online_softmax_step(s, _head_rows(new_ref, H_C + h, n_slots, False),
                                 m_ref.at[h], l_ref.at[h], acc_ref.at[h])

    @pl.when(j > 0)
    def _():
        lf = lfpage_ref[0]
        hi, lo = _split2(lf)
        suffix = _suffix_matrix(BLK)
        later = carry_ref[...] + _dot(hi, suffix) + _dot(lo, suffix)
        for h in range(H_C):
            s = _dot_nt(q_ref[0, h].astype(BF16), _head_rows(page_ref, h, n_slots, True)) * (HEAD_DIM ** -0.5)
            s = s + (cum_new_col(h) + later[h:h + 1, :])
            _online_softmax_step(s, _head_rows(page_ref, H_C + h, n_slots, True),
                                 m_ref.at[h], l_ref.at[h], acc_ref.at[h])
        carry_ref[...] += jnp.sum(lf, axis=1, keepdims=True)

    @pl.when(j == n_pages)
    def _():
        o_ref[0] = acc_ref[...] / l_ref[...]


def _attn_c_sample(page_table, q_h8, lf_row, lf_col, new_rows, logf_pages, cache_flat, ds):
    db, n_pages = page_table.shape
    n_slots = 2 * H_C
    page_of = lambda b, j, pt: pt[b, n_pages - jnp.maximum(j, 1)]
    return pl.pallas_call(
        functools.partial(_attn_c_sample_kernel, n_pages=n_pages, ds=ds),
        grid_spec=pltpu.PrefetchScalarGridSpec(
            num_scalar_prefetch=1,
            grid=(db, n_pages + 1),
            in_specs=[pl.BlockSpec((1, H_C, 8, HEAD_DIM), lambda b, j, pt: (b, 0, 0, 0)),
                      pl.BlockSpec((1, H_C, 1, BLK), lambda b, j, pt: (b, 0, 0, 0)),
                      pl.BlockSpec((1, H_C, NEW_ROWS, 1), lambda b, j, pt: (b, 0, 0, 0)),
                      pl.BlockSpec((1, NEW_ROWS * n_slots, HEAD_DIM), lambda b, j, pt: (b, 0, 0)),
                      pl.BlockSpec((1, H_C, BLK), lambda b, j, pt: (page_of(b, j, pt), 0, 0)),
                      pl.BlockSpec((1, BLK * n_slots, HEAD_DIM), lambda b, j, pt: (page_of(b, j, pt), 0, 0))],
            out_specs=pl.BlockSpec((1, H_C, 8, HEAD_DIM), lambda b, j, pt: (b, 0, 0, 0)),
            scratch_shapes=[pltpu.VMEM((H_C, 8, 1), F32), pltpu.VMEM((H_C, 8, 1), F32),
                            pltpu.VMEM((H_C, 8, HEAD_DIM), F32), pltpu.VMEM((H_C, 1), F32)]),
        out_shape=jax.ShapeDtypeStruct((db, H_C, 8, HEAD_DIM), F32),
        compiler_params=_params(("parallel", "arbitrary")),
        name="forget_attend_sample",
    )(page_table, q_h8, lf_row, lf_col, new_rows, logf_pages, cache_flat)


def _layer_norm(xf, g, b):
    mu = jnp.mean(xf, axis=1, keepdims=True)
    d = xf - mu
    var = jnp.mean(d * d, axis=1, keepdims=True)
    return d * lax.rsqrt(var + LN_EPS) * g + b


def _route(y, wr_ref, br_ref):
    y_hi, y_lo = _split2(y)
    w = wr_ref[...]
    p1 = _dot(y_hi, w)
    p2 = _dot(y_lo, w)
    logits = p1 + pltpu.roll(p1, LANES - N_EXPERTS, 1) + p2
    lane = lax.broadcasted_iota(jnp.int32, logits.shape, 1)
    in_grp = lane < N_GROUPS
    s = [jax.nn.sigmoid(logits if j == 0 else pltpu.roll(logits, LANES - j * N_GROUPS, 1))
         for j in range(EPG)]
    sel = [jnp.where(in_grp, s[j] + br_ref[j:j + 1, :], NEG) for j in range(EPG)]
    top2 = None
    for a in range(EPG):
        for b in range(a + 1, EPG):
            pair = sel[a] + sel[b]
            top2 = pair if top2 is None else jnp.maximum(top2, pair)
    top2 = jnp.where(in_grp, top2, -jnp.inf)
    best_val = jnp.max(top2, axis=1, keepdims=True)
    g_best = jnp.min(jnp.where(top2 == best_val, lane, LANES), axis=1, keepdims=True)
    mine = lane == g_best
    picked = []
    for j in range(EPG):
        rank = jnp.zeros(logits.shape, F32)
        for i in range(EPG):
            if i == j:
                continue
            ahead = (sel[i] >= sel[j]) if i < j else (sel[i] > sel[j])
            rank = rank + ahead.astype(F32)
        picked.append(jnp.sum(jnp.where(mine & (rank < 2.0), s[j], 0.0), axis=1, keepdims=True))
    denom = picked[0] + picked[1] + picked[2] + picked[3]
    extra = jnp.where(lane == EPG, g_best.astype(F32), 0.0)
    for j in range(EPG):
        extra = jnp.where(lane == j, picked[j] / denom, extra)
    return extra


def _outproj_kernel(o1_ref, o2_ref, w1_ref, w2_ref, x_ref, g_ref, b_ref, wr_ref, br_ref, out_ref, *, d_model):
    mix = _dot(o1_ref[...].astype(BF16), w1_ref[...]) + _dot(o2_ref[...].astype(BF16), w2_ref[...])
    y = _layer_norm(ALPHA * x_ref[...] + mix, g_ref[...], b_ref[...])
    out_ref[:, :d_model] = y
    out_ref[:, d_model:] = _route(y, wr_ref, br_ref)


def _outproj(o1, o2, o2_block, w_bf16, x, g, b, wr, br, d_model, tm):
    nf = x.shape[0]
    half = w_bf16.shape[0] // 2
    return pl.pallas_call(
        functools.partial(_outproj_kernel, d_model=d_model),
        grid=(_cdiv(nf, tm),),
        in_specs=[pl.BlockSpec((tm, half), lambda i: (i, 0)),
                  pl.BlockSpec((tm, half), lambda i: (i, o2_block)),
                  pl.BlockSpec((half, d_model), lambda i: (0, 0)),
                  pl.BlockSpec((half, d_model), lambda i: (1, 0)),
                  pl.BlockSpec((tm, d_model), lambda i: (i, 0)),
                  pl.BlockSpec((1, d_model), lambda i: (0, 0)),
                  pl.BlockSpec((1, d_model), lambda i: (0, 0)),
                  pl.BlockSpec((d_model, LANES), lambda i: (0, 0)),
                  pl.BlockSpec((EPG, LANES), lambda i: (0, 0))],
        out_specs=pl.BlockSpec((tm, d_model + XCOLS), lambda i: (i, 0)),
        out_shape=jax.ShapeDtypeStruct((nf, d_model + XCOLS), F32),
        compiler_params=_params(("parallel",)),
        name="outproj_norm_route",
    )(o1, o2, w_bf16, w_bf16, x, g, b, wr, br)


def _gather_rows(idx_ref, base, src_ref, dst_ref, sem, n_rows):
    def issue(r, c):
        pltpu.make_async_copy(src_ref.at[pl.ds(idx_ref[base + r], 1)], dst_ref.at[pl.ds(r, 1)], sem).start()
        return c

    lax.fori_loop(0, n_rows, issue, 0)

    def drain(r, c):
        pltpu.make_async_copy(src_ref.at[pl.ds(0, 1)], dst_ref.at[pl.ds(r, 1)], sem).wait()
        return c

    lax.fori_loop(0, n_rows, drain, 0)


def _moe_kernel(grp_ref, valid_ref, src_ref, xa_ref, wg_ref, wu_ref, wd_ref, g_ref, b_ref, o_ref,
                xs_ref, xb_ref, acc_ref, sem, *, d_model, tm):
    i = pl.program_id(0)
    e = pl.program_id(1)
    valid = valid_ref[i] == 1

    @pl.when(valid & (e == 0))
    def _():
        _gather_rows(src_ref, i * tm, xa_ref, xs_ref, sem, tm)
        xb_ref[...] = xs_ref[:, :d_model].astype(BF16)
        acc_ref[...] = jnp.zeros_like(acc_ref)

    @pl.when(valid)
    def _():
        xb = xb_ref[...]
        a = _dot(xb, wg_ref[0])
        h = a * jax.nn.sigmoid(a) * _dot(xb, wu_ref[0])
        extra = xs_ref[:, d_model:]
        lane = lax.broadcasted_iota(jnp.int32, extra.shape, 1)
        gate = jnp.sum(jnp.where(lane == e, extra, 0.0), axis=1, keepdims=True)
        acc_ref[...] += _dot((h * gate).astype(BF16), wd_ref[0])

    @pl.when(valid & (e == EPG - 1))
    def _():
        o_ref[...] = _layer_norm(ALPHA * xs_ref[:, :d_model] + acc_ref[...], g_ref[...], b_ref[...])

    @pl.when(jnp.logical_not(valid) & (e == EPG - 1))
    def _():
        o_ref[...] = jnp.zeros_like(o_ref)


def _moe(tile_grp, tile_valid, src, xa, wg, wu, wd, g, b, d_model, tm):
    n_tiles = tile_grp.shape[0]
    d_exp = wg.shape[2]

    def w_idx(i, e, grp, valid, src):
        return (grp[i] * EPG + jnp.where(valid[i] == 1, e, EPG - 1), 0, 0)

    return pl.pallas_call(
        functools.partial(_moe_kernel, d_model=d_model, tm=tm),
        grid_spec=pltpu.PrefetchScalarGridSpec(
            num_scalar_prefetch=3,
            grid=(n_tiles, EPG),
            in_specs=[pl.BlockSpec(memory_space=pl.ANY),
                      pl.BlockSpec((1, d_model, d_exp), w_idx),
                      pl.BlockSpec((1, d_model, d_exp), w_idx),
                      pl.BlockSpec((1, d_exp, d_model), w_idx),
                      pl.BlockSpec((1, d_model), lambda i, e, *_: (0, 0)),
                      pl.BlockSpec((1, d_model), lambda i, e, *_: (0, 0))],
            out_specs=pl.BlockSpec((tm, d_model), lambda i, e, *_: (i, 0)),
            scratch_shapes=[pltpu.VMEM((tm, d_model + XCOLS), F32),
                            pltpu.VMEM((tm, d_model), BF16),
                            pltpu.VMEM((tm, d_model), F32),
                            pltpu.SemaphoreType.DMA(())]),
        out_shape=jax.ShapeDtypeStruct((n_tiles * tm, d_model), F32),
        compiler_params=_params(("arbitrary", "arbitrary")),
        name="grouped_moe",
    )(tile_grp, tile_valid, src, xa, wg, wu, wd, g, b)


def _unpermute_kernel(idx_ref, src_ref, o_ref, sem, *, tg):
    _gather_rows(idx_ref, pl.program_id(0) * tg, src_ref, o_ref, sem, tg)


def _unpermute(dest_padded, ys, n_rows, tg):
    d = ys.shape[1]
    return pl.pallas_call(
        functools.partial(_unpermute_kernel, tg=tg),
        grid_spec=pltpu.PrefetchScalarGridSpec(
            num_scalar_prefetch=1,
            grid=(_cdiv(n_rows, tg),),
            in_specs=[pl.BlockSpec(memory_space=pl.ANY)],
            out_specs=pl.BlockSpec((tg, d), lambda i, idx: (i, 0)),
            scratch_shapes=[pltpu.SemaphoreType.DMA(())]),
        out_shape=jax.ShapeDtypeStruct((n_rows, d), F32),
        compiler_params=_params(("arbitrary",)),
        name="unpermute_rows",
    )(dest_padded, ys)


def _routing_plan(grp, tm):
    nf = grp.shape[0]
    n_tiles = _cdiv(nf + N_GROUPS * (tm - 1), tm)
    onehot = (grp[:, None] == jnp.arange(N_GROUPS, dtype=jnp.int32)[None, :]).astype(jnp.int32)
    counts = jnp.sum(onehot, axis=0)
    rank = jnp.sum((jnp.cumsum(onehot, axis=0) - onehot) * onehot, axis=1)
    padded = ((counts + tm - 1) // tm) * tm
    ends = jnp.cumsum(padded)
    dest = (ends - padded)[grp] + rank
    src = jnp.zeros((n_tiles * tm,), jnp.int32).at[dest].set(jnp.arange(nf, dtype=jnp.int32))
    starts = jnp.arange(n_tiles, dtype=jnp.int32) * tm
    tile_valid = (starts < ends[-1]).astype(jnp.int32)
    tile_grp = jnp.minimum(jnp.searchsorted(ends, starts, side="right"), N_GROUPS - 1).astype(jnp.int32)
    last_grp = tile_grp[jnp.maximum(ends[-1] // tm - 1, 0)]
    tile_grp = jnp.where(tile_valid == 1, tile_grp, last_grp)
    return tile_grp, tile_valid, src, dest


def _ffn(xa, wg, wu, wd, g, b, d_model, tm, tg):
    nf = xa.shape[0]
    grp = xa[:, d_model + EPG].astype(jnp.int32)
    tile_grp, tile_valid, src, dest = _routing_plan(grp, tm)
    ys = _moe(tile_grp, tile_valid, src, xa, wg, wu, wd, g, b, d_model, tm)
    dest_padded = jnp.zeros((_cdiv(nf, tg) * tg,), jnp.int32).at[:nf].set(dest)
    return _unpermute(dest_padded, ys, nf, tg)


def _rope_tables(pos):
    def table(dim, period):
        half = dim // 2
        inv = ROPE_THETA ** (-jnp.arange(half, dtype=F32) / half)
        ang = pos.astype(F32)[:, None] * inv[None, :]
        cos = jnp.cos(ang)
        sin = jnp.sin(ang)
        reps = TN // dim
        return jnp.tile(jnp.concatenate([cos, cos], axis=1), (1, reps)), \
            jnp.tile(jnp.concatenate([-sin, sin], axis=1), (1, reps))

    c128, s128 = table(HEAD_DIM, HEAD_DIM)
    c64, s64 = table(IDX_DIM, IDX_DIM)
    return c128, s128, c64, s64


def _pad_rows(a, rows):
    return jnp.pad(a, [(0, 0), (0, rows - a.shape[1])] + [(0, 0)] * (a.ndim - 2))


def kernel(x_prompt, x_sample, cache_l0_a_kv, cache_l0_idx_k, cache_l0_b_kv, cache_l1_c_kv, cache_l1_logf,
           page_table, meta_tokens, w_in_l0, w_out_l0, w_in_l1, b_forget_l1, w_out_l1, ln_mix_g, ln_mix_b,
           ln_ffn_g, ln_ffn_b, w_router, b_router, w_gate, w_up, w_down):
    n_batch, seq, d_model = x_prompt.shape
    db, ds, _ = x_sample.shape
    assert ds <= 8 and d_model % LANES == 0
    t_len = seq + N_META
    tp = _cdiv(t_len, BLK) * BLK
    n_prompt = n_batch * tp
    n_sample = db * ds
    nf = n_prompt + n_sample
    n_pool = cache_l0_a_kv.shape[0]
    n_pages = page_table.shape[1]
    past = n_pages * BLK
    topk_prompt = min(TOPK_MAX, seq // 4)
    topk_sample = min(TOPK_MAX, (past + ds) // 4)
    tm_proj = 536 if nf % 536 == 0 else 128
    tm_out = 256
    tm_moe = 512
    tg = 536 if nf % 536 == 0 else 8

    meta = jnp.broadcast_to(meta_tokens[None], (n_batch, N_META, d_model)).astype(x_prompt.dtype)
    hp = _pad_rows(jnp.concatenate([meta, x_prompt], axis=1), tp)
    x0 = jnp.concatenate([hp.reshape(n_prompt, d_model), x_sample.reshape(n_sample, d_model)], axis=0)
    pos = jnp.concatenate([jnp.tile(jnp.arange(tp), n_batch), jnp.tile(past + jnp.arange(ds), db)])
    tables = _rope_tables(pos)

    cuts = np.cumsum((0, H_A * HEAD_DIM, HKV_A * HEAD_DIM, HKV_A * HEAD_DIM, H_B * HEAD_DIM, H_B * HEAD_DIM,
                      H_B * HEAD_DIM, H_IDX * IDX_DIM, IDX_DIM, H_IDX))
    seg = [w_in_l0[:, cuts[i]:cuts[i + 1]] for i in range(9)]
    qa_w, ka_w, va_w, qb_w, kb_w, vb_w, iq_w, ik_w, iw_w = seg
    zeros = lambda n: jnp.zeros((d_model, n), w_in_l0.dtype)
    w0 = jnp.concatenate([qa_w, iq_w, ka_w, va_w, qb_w, kb_w, vb_w, ik_w, zeros(LANES - IDX_DIM),
                          iw_w, zeros(LANES - H_IDX)], axis=1).astype(BF16)
    modes0 = jnp.asarray([1] * 4 + [2] * 4 + [1] * 2 + [0] * 14 + [3], jnp.int32)
    n_qkv = 3 * H_C * HEAD_DIM
    w1 = jnp.concatenate([w_in_l1[:, :n_qkv], w_in_l1[:, n_qkv:], zeros(TN - H_C)], axis=1).astype(BF16)
    bf_row = jnp.zeros((1, LANES), F32).at[0, :H_C].set(b_forget_l1)
    w_out0 = w_out_l0.astype(BF16)
    w_out1 = w_out_l1.astype(BF16)
    wr_perm = w_router.reshape(d_model, N_GROUPS, EPG).transpose(0, 2, 1).reshape(d_model, N_EXPERTS)
    wr_hi = wr_perm.astype(BF16)
    wr_lo = (wr_perm - wr_hi.astype(F32)).astype(BF16)
    wr = jnp.concatenate([wr_hi, wr_lo, jnp.zeros((d_model, LANES - 2 * N_EXPERTS), BF16)], axis=1)
    br = jnp.zeros((EPG, LANES), F32).at[:, :N_GROUPS].set(b_router.reshape(N_GROUPS, EPG).T)
    wg = w_gate.astype(BF16)
    wu = w_up.astype(BF16)
    wd = w_down.astype(BF16)
    row = lambda v: v.reshape(1, d_model)

    p0 = _inproj(x0, w0, d_model, tm_proj, modes0, tables)
    col = lambda blk0, n: slice(blk0 * LANES, (blk0 + n) * LANES)
    ps = p0[n_prompt:]

    bias_p = _idx_prompt(p0, n_batch, tp, topk_prompt)
    oa_p = _attn_a_prompt(p0, bias_p, n_batch, tp)
    ob_p = _attn_b_prompt(p0, n_batch, tp)

    grp_q = H_A // HKV_A
    iq_th = jnp.tile(ps[:, col(8, 8)].reshape(db, ds * H_IDX, IDX_DIM), (1, grp_q, 1))
    iw_th = jnp.tile(ps[:, 49 * LANES:49 * LANES + H_IDX].reshape(db, ds * H_IDX, 1), (1, grp_q, 1))
    ik_s = ps[:, 48 * LANES:48 * LANES + IDX_DIM].reshape(db, ds, IDX_DIM)
    bias_s = _idx_sample(page_table, iq_th, iw_th, _pad_rows(ik_s, BLK), cache_l0_idx_k, topk_sample, ds)

    ka_s = ps[:, col(16, 4)].reshape(db, ds, HKV_A, HEAD_DIM)
    va_s = ps[:, col(20, 4)].reshape(db, ds, HKV_A, HEAD_DIM)
    kv_a_s = jnp.stack([ka_s, va_s], axis=2)
    new_a = _pad_rows(kv_a_s, 8).reshape(db, 8 * 2 * HKV_A, HEAD_DIM)
    qa_s = ps[:, col(0, 8)].reshape(db, ds, HKV_A, grp_q, HEAD_DIM).transpose(0, 2, 3, 1, 4)
    qa_s = qa_s.reshape(db, HKV_A, grp_q * ds, HEAD_DIM)
    oa_s = _attn_a_sample(page_table, qa_s, bias_s, new_a,
                          cache_l0_a_kv.reshape(n_pool, BLK * 2 * HKV_A, HEAD_DIM), ds)
    oa_s = oa_s.reshape(db, HKV_A, grp_q, ds, HEAD_DIM).transpose(0, 3, 1, 2, 4).reshape(n_sample, H_A * HEAD_DIM)

    kb_s = ps[:, col(32, 8)].reshape(db, ds, H_B, HEAD_DIM)
    vb_s = ps[:, col(40, 8)].reshape(db, ds, H_B, HEAD_DIM)
    kv_b_s = jnp.stack([kb_s, vb_s], axis=2)
    new_b = _pad_rows(kv_b_s, 8).reshape(db, 8 * 2 * H_B, HEAD_DIM)
    qb_s = _pad_rows(ps[:, col(24, 8)].reshape(db, ds, H_B, HEAD_DIM), 8).transpose(0, 2, 1, 3)
    ob_s = _attn_b_sample(page_table, qb_s, new_b, cache_l0_b_kv.reshape(n_pool, BLK * 2 * H_B, HEAD_DIM), ds)
    ob_s = ob_s[:, :, :ds].transpose(0, 2, 1, 3).reshape(n_sample, H_B * HEAD_DIM)

    oa = jnp.concatenate([oa_p, oa_s], axis=0)
    ob = jnp.concatenate([ob_p, ob_s], axis=0)
    xa1 = _outproj(oa, ob, 0, w_out0, x0, row(ln_mix_g[0]), row(ln_mix_b[0]), wr, br, d_model, tm_out)
    x1 = _ffn(xa1, wg[0], wu[0], wd[0], row(ln_ffn_g[0]), row(ln_ffn_b[0]), d_model, tm_moe, tg)

    p1 = _inproj(x1, w1, d_model, tm_proj)
    logf, cum = _logf(p1, bf_row, nf, tp // BLK)
    cum_t = cum[:n_prompt, :H_C].reshape(n_batch, tp, H_C).transpose(0, 2, 1)
    oc_p = _attn_c_prompt(p1, cum_t[..., None], cum_t[:, :, None, :], n_batch, tp)

    ps1 = p1[n_prompt:]
    kc_s = ps1[:, col(16, 16)].reshape(db, ds, H_C, HEAD_DIM)
    vc_s = ps1[:, col(32, 16)].reshape(db, ds, H_C, HEAD_DIM)
    kv_c_s = jnp.stack([kc_s, vc_s], axis=2)
    new_c = _pad_rows(kv_c_s, 8).reshape(db, 8 * 2 * H_C, HEAD_DIM)
    qc_s = _pad_rows(ps1[:, col(0, 16)].reshape(db, ds, H_C, HEAD_DIM), 8).transpose(0, 2, 1, 3)
    logf_s = logf[n_prompt:, :H_C].reshape(db, ds, H_C)
    lf_t = _pad_rows(logf_s, BLK).transpose(0, 2, 1)
    oc_s = _attn_c_sample(page_table, qc_s, lf_t[:, :, None, :], lf_t[:, :, :8, None], new_c,
                          cache_l1_logf.astype(F32).transpose(0, 2, 1),
                          cache_l1_c_kv.reshape(n_pool, BLK * 2 * H_C, HEAD_DIM), ds)
    oc_s = oc_s[:, :, :ds].transpose(0, 2, 1, 3).reshape(n_sample, H_C * HEAD_DIM)

    oc = jnp.concatenate([oc_p, oc_s], axis=0)
    xa2 = _outproj(oc, oc, 1, w_out1, x1, row(ln_mix_g[1]), row(ln_mix_b[1]), wr, br, d_model, tm_out)
    x2 = _ffn(xa2, wg[1], wu[1], wd[1], row(ln_ffn_g[1]), row(ln_ffn_b[1]), d_model, tm_moe, tg)

    def prompt_rows(a, blk0, heads):
        return a[:n_prompt, col(blk0, heads)].reshape(n_batch, tp, heads, HEAD_DIM)[:, :t_len]

    y_prompt = x2[:n_prompt].reshape(n_batch, tp, d_model)[:, N_META:t_len]
    y_sample = x2[n_prompt:].reshape(db, ds, d_model)
    a_kv_p = jnp.stack([prompt_rows(p0, 16, HKV_A), prompt_rows(p0, 20, HKV_A)], axis=2)
    idx_k_p = p0[:n_prompt, 48 * LANES:48 * LANES + IDX_DIM].reshape(n_batch, tp, IDX_DIM)[:, :t_len]
    b_kv_p = jnp.stack([prompt_rows(p0, 32, H_B), prompt_rows(p0, 40, H_B)], axis=2)
    c_kv_p = jnp.stack([prompt_rows(p1, 16, H_C), prompt_rows(p1, 32, H_C)], axis=2)
    logf_p = logf[:n_prompt, :H_C].reshape(n_batch, tp, H_C)[:, :t_len]
    return (y_prompt, y_sample, a_kv_p, kv_a_s, idx_k_p, ik_s, b_kv_p, kv_b_s, c_kv_p, kv_c_s, logf_p, logf_s)
```

```python
import functools

import numpy as np
import jax
import jax.numpy as jnp
from jax import lax
from jax.experimental import pallas as pl
from jax.experimental.pallas import tpu as pltpu

HEAD_DIM = 128
H_A = 8
HKV_A = 4
H_B = 8
H_C = 16
H_IDX = 16
IDX_DIM = 64
TOPK_MAX = 256
N_META = 16
BLK = 128
ROPE_THETA = 10000.0
N_EXPERTS = 32
N_GROUPS = 8
EPG = N_EXPERTS // N_GROUPS
LN_EPS = 1e-5
DEPTH = 2
ALPHA = (2 * DEPTH) ** 0.25
NEG = -1e30
INT_MIN = -2 ** 31
KEY_NEG_INF = -2139095041
LANES = 128
TN = 256
XCOLS = 128
VMEM_LIMIT = 56 * 1024 * 1024
SCALE = HEAD_DIM ** -0.5

F32 = jnp.float32
BF16 = jnp.bfloat16
NT_DIMS = (((1,), (1,)), ((), ()))


def _cdiv(a, b):
    return (a + b - 1) // b


def _dot(a, b):
    return jnp.dot(a, b, preferred_element_type=F32)


def _dot_nt(a, b):
    return lax.dot_general(a, b, NT_DIMS, preferred_element_type=F32)


def _split2(x):
    hi = x.astype(BF16)
    lo = (x - hi.astype(F32)).astype(BF16)
    return hi, lo


def _split3(x):
    hi = x.astype(BF16)
    r = x - hi.astype(F32)
    mid = r.astype(BF16)
    lo = (r - mid.astype(F32)).astype(BF16)
    return hi, mid, lo


def _dot3(a_bf16, x):
    hi, mid, lo = _split3(x)
    return _dot(a_bf16, hi) + _dot(a_bf16, mid) + _dot(a_bf16, lo)


def _dot3_left(x, a_bf16):
    hi, mid, lo = _split3(x)
    return _dot(hi, a_bf16) + _dot(mid, a_bf16) + _dot(lo, a_bf16)


def _neg_softplus(z):
    return -(jnp.maximum(z, 0.0) + jnp.log1p(jnp.exp(-jnp.abs(z))))


def _lane_chunks(x):
    return [x[:, c * LANES:(c + 1) * LANES] for c in range(x.shape[1] // LANES)]


def _params(sem, vmem=VMEM_LIMIT):
    return pltpu.CompilerParams(dimension_semantics=sem, vmem_limit_bytes=vmem)


def _rope128(a, c, s):
    return a * c + pltpu.roll(a, 64, 1) * s


def _rope64(a, c, s):
    lane = lax.broadcasted_iota(jnp.int32, a.shape, 1)
    first = (lane % 64) < 32
    partner = jnp.where(first, pltpu.roll(a, 96, 1), pltpu.roll(a, 32, 1))
    return a * c + partner * s


def _inproj_rope_kernel(modes_ref, x_ref, w_ref, c128_ref, s128_ref, c64_ref, s64_ref, o_ref, xb_ref):
    j = pl.program_id(1)

    @pl.when(j == 0)
    def _():
        xb_ref[...] = x_ref[...].astype(BF16)

    acc = _dot(xb_ref[...], w_ref[...])
    mode = modes_ref[j]

    @pl.when(mode == 0)
    def _():
        o_ref[...] = acc

    @pl.when(mode == 1)
    def _():
        for c in range(TN // LANES):
            sl = slice(c * LANES, (c + 1) * LANES)
            o_ref[:, sl] = _rope128(acc[:, sl], c128_ref[:, sl], s128_ref[:, sl])

    @pl.when(mode == 2)
    def _():
        for c in range(TN // LANES):
            sl = slice(c * LANES, (c + 1) * LANES)
            o_ref[:, sl] = _rope64(acc[:, sl], c64_ref[:, sl], s64_ref[:, sl])

    @pl.when(mode == 3)
    def _():
        sl = slice(0, LANES)
        o_ref[:, sl] = _rope64(acc[:, sl], c64_ref[:, sl], s64_ref[:, sl])
        o_ref[:, LANES:] = acc[:, LANES:]


def _inproj_plain_kernel(x_ref, w_ref, o_ref, xb_ref):
    @pl.when(pl.program_id(1) == 0)
    def _():
        xb_ref[...] = x_ref[...].astype(BF16)

    o_ref[...] = _dot(xb_ref[...], w_ref[...])


def _inproj(x, w_bf16, d_model, tm, modes=None, tables=None):
    nf = x.shape[0]
    ncols = w_bf16.shape[1]
    grid = (_cdiv(nf, tm), ncols // TN)
    scratch = [pltpu.VMEM((tm, d_model), BF16)]
    out_shape = jax.ShapeDtypeStruct((nf, ncols), F32)
    if modes is None:
        return pl.pallas_call(
            _inproj_plain_kernel,
            grid=grid,
            in_specs=[pl.BlockSpec((tm, d_model), lambda i, j: (i, 0)),
                      pl.BlockSpec((d_model, TN), lambda i, j: (0, j))],
            out_specs=pl.BlockSpec((tm, TN), lambda i, j: (i, j)),
            out_shape=out_shape,
            scratch_shapes=scratch,
            compiler_params=_params(("parallel", "arbitrary")),
            name="inproj_plain",
        )(x, w_bf16)
    tab_spec = pl.BlockSpec((tm, TN), lambda i, j, m: (i, 0))
    return pl.pallas_call(
        _inproj_rope_kernel,
        grid_spec=pltpu.PrefetchScalarGridSpec(
            num_scalar_prefetch=1,
            grid=grid,
            in_specs=[pl.BlockSpec((tm, d_model), lambda i, j, m: (i, 0)),
                      pl.BlockSpec((d_model, TN), lambda i, j, m: (0, j)),
                      tab_spec, tab_spec, tab_spec, tab_spec],
            out_specs=pl.BlockSpec((tm, TN), lambda i, j, m: (i, j)),
            scratch_shapes=scratch),
        out_shape=out_shape,
        compiler_params=_params(("parallel", "arbitrary")),
        name="inproj_rope",
    )(modes, x, w_bf16, *tables)


def _sortable_key(score):
    score = jnp.where(score == 0.0, 0.0, score)
    bits = lax.bitcast_convert_type(score, jnp.int32)
    return bits ^ ((bits >> 31) & 0x7FFFFFFF)


def _count(mask):
    return jnp.sum(mask.astype(F32), axis=1, keepdims=True)


def _topk_select(key_ref, vis, s_pos, topk, n_cols, thr_ref, need_ref, jb_ref):
    kf = float(topk)
    c0 = _count(key_ref[...] >= 0)
    ans0 = jnp.where(c0 >= kf, 0, INT_MIN).astype(jnp.int32)

    def body(i, ans):
        cand = ans | jnp.left_shift(jnp.int32(1), 30 - i)
        cnt = _count(key_ref[...] >= cand)
        return jnp.where(cnt >= kf, cand, ans)

    thr = lax.fori_loop(0, 31, body, ans0)
    key = key_ref[...]
    need = kf - _count(key > thr)
    n_eq = _count((key == thr) & vis)
    thr_ref[...] = thr
    need_ref[...] = need
    jb_ref[...] = jnp.full(jb_ref.shape, n_cols, jnp.int32)
    n_bits = int(np.ceil(np.log2(n_cols))) + 1

    @pl.when(jnp.max(n_eq - need) > 0.0)
    def _():
        def body2(i, ans):
            cand = ans | jnp.left_shift(jnp.int32(1), n_bits - 1 - i)
            eqv = (key_ref[...] == thr_ref[...]) & vis
            c = _count(eqv & (s_pos < cand))
            return jnp.where(c < need_ref[...], cand, ans)

        jb_ref[...] = lax.fori_loop(0, n_bits, body2, jnp.zeros(jb_ref.shape, jnp.int32))

    return vis & ((key > thr) | ((key == thr) & (s_pos <= jb_ref[...])))


def _idx_prompt_kernel(iq_ref, iw_ref, ik_ref, bias_ref, ikd_ref, key_ref, thr_ref, need_ref, jb_ref,
                       *, topk, tp):
    qi = pl.program_id(1)

    @pl.when(qi == 0)
    def _():
        ik = ik_ref[...]
        ikd_ref[...] = (ik + pltpu.roll(ik, 64, 1)).astype(BF16)

    lane = lax.broadcasted_iota(jnp.int32, (1, LANES), 1)
    score = jnp.zeros((BLK, tp), F32)
    for h in range(H_IDX):
        pair = iq_ref[:, (h // 2) * LANES:(h // 2 + 1) * LANES]
        lo = (h % 2) * IDX_DIM
        qh = jnp.where((lane >= lo) & (lane < lo + IDX_DIM), pair, 0.0).astype(BF16)
        dots = _dot_nt(qh, ikd_ref[...]) * (IDX_DIM ** -0.5)
        score = score + jnp.maximum(dots, 0.0) * (iw_ref[:, h:h + 1] * (H_IDX ** -0.5))

    t_pos = qi * BLK + lax.broadcasted_iota(jnp.int32, (BLK, 1), 0)
    s_pos = lax.broadcasted_iota(jnp.int32, (1, tp), 1)
    vis = s_pos <= t_pos
    key_ref[...] = jnp.where(vis, _sortable_key(score), KEY_NEG_INF)
    sel = _topk_select(key_ref, vis, s_pos, topk, tp, thr_ref, need_ref, jb_ref)
    bias_ref[...] = jnp.where(sel, 0.0, NEG).astype(BF16)


def _idx_prompt(p0, n_batch, tp, topk):
    nqb = tp // BLK
    return pl.pallas_call(
        functools.partial(_idx_prompt_kernel, topk=topk, tp=tp),
        grid=(n_batch, nqb),
        in_specs=[pl.BlockSpec((BLK, 1024), lambda b, q: (b * nqb + q, 1)),
                  pl.BlockSpec((BLK, LANES), lambda b, q: (b * nqb + q, 49)),
                  pl.BlockSpec((tp, LANES), lambda b, q: (b, 48))],
        out_specs=pl.BlockSpec((BLK, tp), lambda b, q: (b * nqb + q, 0)),
        out_shape=jax.ShapeDtypeStruct((n_batch * tp, tp), BF16),
        scratch_shapes=[pltpu.VMEM((tp, LANES), BF16),
                        pltpu.VMEM((BLK, tp), jnp.int32),
                        pltpu.VMEM((BLK, 1), jnp.int32),
                        pltpu.VMEM((BLK, 1), F32),
                        pltpu.VMEM((BLK, 1), jnp.int32)],
        compiler_params=_params(("parallel", "arbitrary")),
        name="dsa_index_prompt",
    )(p0, p0, p0)


def _attn_a_prompt_kernel(q_ref, k_ref, v_ref, bias_ref, o_ref, kb_ref, vb_ref):
    @pl.when(pl.program_id(2) == 0)
    def _():
        kb_ref[...] = k_ref[...].astype(BF16)
        vb_ref[...] = v_ref[...].astype(BF16)

    bias = bias_ref[...].astype(F32)
    for g in range(H_A // HKV_A):
        sl = slice(g * HEAD_DIM, (g + 1) * HEAD_DIM)
        s = _dot_nt(q_ref[:, sl].astype(BF16), kb_ref[...]) * SCALE + bias
        m = jnp.max(s, axis=1, keepdims=True)
        p = jnp.exp(s - m)
        l = jnp.sum(p, axis=1, keepdims=True)
        o_ref[:, sl] = _dot(p.astype(BF16), vb_ref[...]) / l


def _attn_a_prompt(p0, bias, n_batch, tp):
    nqb = tp // BLK
    gw = (H_A // HKV_A) * HEAD_DIM
    return pl.pallas_call(
        _attn_a_prompt_kernel,
        grid=(n_batch, HKV_A, nqb),
        in_specs=[pl.BlockSpec((BLK, gw), lambda b, k, q: (b * nqb + q, k)),
                  pl.BlockSpec((tp, HEAD_DIM), lambda b, k, q: (b, 16 + k)),
                  pl.BlockSpec((tp, HEAD_DIM), lambda b, k, q: (b, 20 + k)),
                  pl.BlockSpec((BLK, tp), lambda b, k, q: (b * nqb + q, 0))],
        out_specs=pl.BlockSpec((BLK, gw), lambda b, k, q: (b * nqb + q, k)),
        out_shape=jax.ShapeDtypeStruct((n_batch * tp, H_A * HEAD_DIM), F32),
        scratch_shapes=[pltpu.VMEM((tp, HEAD_DIM), BF16), pltpu.VMEM((tp, HEAD_DIM), BF16)],
        compiler_params=_params(("parallel", "parallel", "arbitrary")),
        name="dsa_attend_prompt",
    )(p0, p0, p0, bias)


def _suffix_and_ones(group):
    r = lax.broadcasted_iota(jnp.int32, (LANES, 2 * LANES), 0)
    c = lax.broadcasted_iota(jnp.int32, (LANES, 2 * LANES), 1)
    return ((c >= LANES) | (r // group > c // group)).astype(BF16)


def _pick_tq(tp):
    return 384 if tp % 384 == 0 else BLK


def _stick_weights(z, strict_fn, tail, sums):
    rows = z.shape[0]
    pieces = _lane_chunks(z)
    n = len(pieces)
    lsn = [_neg_softplus(p) for p in pieces]
    ok = [strict_fn(i) for i in range(n)]
    hi, lo = _split2(jnp.concatenate([jnp.where(ok[i], lsn[i], 0.0) for i in range(n)], axis=0))
    ar = _dot(hi, sums) + _dot(lo, sums)
    w = [None] * n
    for i in reversed(range(n)):
        blk = ar[i * rows:(i + 1) * rows]
        w[i] = jnp.where(ok[i], jnp.exp(pieces[i] + lsn[i] + tail + blk[:, :LANES]), 0.0)
        tail = tail + blk[:, LANES:]
    return jnp.concatenate(w, axis=1), tail


def _attn_b_prompt_kernel(q_ref, k_ref, v_ref, o_ref, kb_ref, vb_ref, acc_ref, carry_ref, *, tq):
    qi = pl.program_id(2)

    @pl.when(qi == 0)
    def _():
        kb_ref[...] = k_ref[...].astype(BF16)
        vb_ref[...] = v_ref[...].astype(BF16)

    q = q_ref[...].astype(BF16)
    acc_ref[...] = jnp.zeros_like(acc_ref)
    carry_ref[...] = jnp.zeros_like(carry_ref)
    r = lax.broadcasted_iota(jnp.int32, (tq, BLK), 0)
    c = lax.broadcasted_iota(jnp.int32, (tq, BLK), 1)
    sums = _suffix_and_ones(1)

    def body(it, carry_unused):
        kb = qi - it
        off = pl.multiple_of(kb * tq, BLK)
        z = _dot_nt(q, kb_ref[pl.ds(off, tq), :]) * SCALE
        strict = lambda i: (kb * tq + i * BLK + c) < (qi * tq + r)
        w, tail = _stick_weights(z, strict, carry_ref[...], sums)
        acc_ref[...] += _dot(w.astype(BF16), vb_ref[pl.ds(off, tq), :])
        carry_ref[...] = tail
        return carry_unused

    lax.fori_loop(0, qi + 1, body, 0)
    o_ref[...] = acc_ref[...]


def _attn_b_prompt(p0, n_batch, tp):
    tq = _pick_tq(tp)
    nqb = tp // tq
    return pl.pallas_call(
        functools.partial(_attn_b_prompt_kernel, tq=tq),
        grid=(n_batch, H_B, nqb),
        in_specs=[pl.BlockSpec((tq, HEAD_DIM), lambda b, h, q: (b * nqb + q, 24 + h)),
                  pl.BlockSpec((tp, HEAD_DIM), lambda b, h, q: (b, 32 + h)),
                  pl.BlockSpec((tp, HEAD_DIM), lambda b, h, q: (b, 40 + h))],
        out_specs=pl.BlockSpec((tq, HEAD_DIM), lambda b, h, q: (b * nqb + q, h)),
        out_shape=jax.ShapeDtypeStruct((n_batch * tp, H_B * HEAD_DIM), F32),
        scratch_shapes=[pltpu.VMEM((tp, HEAD_DIM), BF16), pltpu.VMEM((tp, HEAD_DIM), BF16),
                        pltpu.VMEM((tq, HEAD_DIM), F32), pltpu.VMEM((tq, LANES), F32)],
        compiler_params=_params(("parallel", "parallel", "arbitrary")),
        name="stickbreak_prompt",
    )(p0, p0, p0)


def _logf_kernel(f_ref, bf_ref, logf_ref, cum_ref, carry_ref, *, blocks_per_seq):
    i = pl.program_id(0)

    @pl.when(i % blocks_per_seq == 0)
    def _():
        carry_ref[...] = jnp.zeros_like(carry_ref)

    x = f_ref[...] + bf_ref[...]
    logf = _neg_softplus(-x)
    logf_ref[...] = logf
    r = lax.broadcasted_iota(jnp.int32, (BLK, BLK), 0)
    c = lax.broadcasted_iota(jnp.int32, (BLK, BLK), 1)
    cum = carry_ref[...] + _dot3((c <= r).astype(BF16), logf)
    cum_ref[...] = cum
    carry_ref[...] = cum[BLK - 1:BLK, :]


def _logf(p1, bf_row, n_rows, blocks_per_seq):
    return pl.pallas_call(
        functools.partial(_logf_kernel, blocks_per_seq=blocks_per_seq),
        grid=(_cdiv(n_rows, BLK),),
        in_specs=[pl.BlockSpec((BLK, LANES), lambda i: (i, 48)),
                  pl.BlockSpec((1, LANES), lambda i: (0, 0))],
        out_specs=[pl.BlockSpec((BLK, LANES), lambda i: (i, 0)),
                   pl.BlockSpec((BLK, LANES), lambda i: (i, 0))],
        out_shape=[jax.ShapeDtypeStruct((n_rows, LANES), F32),
                   jax.ShapeDtypeStruct((n_rows, LANES), F32)],
        scratch_shapes=[pltpu.VMEM((1, LANES), F32)],
        compiler_params=_params(("arbitrary",)),
        name="log_forget_cumsum",
    )(p1, bf_row)


def _attn_c_prompt_kernel(q_ref, k_ref, v_ref, cq_ref, ck_ref, o_ref, kb_ref, vb_ref, s_ref, m_ref, l_ref,
                          acc_ref, *, tq):
    qi = pl.program_id(2)

    @pl.when(qi == 0)
    def _():
        kb_ref[...] = k_ref[...].astype(BF16)
        vb_ref[...] = v_ref[...].astype(BF16)

    q = q_ref[...].astype(BF16)
    cq = cq_ref[0, 0]
    m_ref[...] = jnp.full(m_ref.shape, NEG, F32)

    def logits(kb):
        off = pl.multiple_of(kb * tq, LANES)
        ck = ck_ref[0, 0, :, pl.ds(off, tq)]
        return off, _dot_nt(q, kb_ref[pl.ds(off, tq), :]) * SCALE + (cq - ck)

    def keep(off, s):
        s_ref[:, pl.ds(off, tq)] = s
        m = m_ref[...]
        for piece in _lane_chunks(s):
            m = jnp.maximum(m, piece)
        m_ref[...] = m

    def pass1(kb, carry_unused):
        keep(*logits(kb))
        return carry_unused

    lax.fori_loop(0, qi, pass1, 0)
    off, s = logits(qi)
    r = lax.broadcasted_iota(jnp.int32, (tq, tq), 0)
    c = lax.broadcasted_iota(jnp.int32, (tq, tq), 1)
    keep(off, jnp.where(c <= r, s, NEG))

    m = jnp.max(m_ref[...], axis=1, keepdims=True)
    l_ref[...] = jnp.zeros_like(l_ref)
    acc_ref[...] = jnp.zeros_like(acc_ref)

    def pass2(kb, carry_unused):
        off = pl.multiple_of(kb * tq, LANES)
        p = jnp.exp(s_ref[:, pl.ds(off, tq)] - m)
        l = l_ref[...]
        for piece in _lane_chunks(p):
            l = l + piece
        l_ref[...] = l
        acc_ref[...] += _dot(p.astype(BF16), vb_ref[pl.ds(off, tq), :])
        return carry_unused

    lax.fori_loop(0, qi + 1, pass2, 0)
    o_ref[...] = acc_ref[...] / jnp.sum(l_ref[...], axis=1, keepdims=True)


def _attn_c_prompt(p1, cum_col, cum_row, n_batch, tp):
    tq = _pick_tq(tp)
    nqb = tp // tq
    return pl.pallas_call(
        functools.partial(_attn_c_prompt_kernel, tq=tq),
        grid=(n_batch, H_C, nqb),
        in_specs=[pl.BlockSpec((tq, HEAD_DIM), lambda b, h, q: (b * nqb + q, h)),
                  pl.BlockSpec((tp, HEAD_DIM), lambda b, h, q: (b, 16 + h)),
                  pl.BlockSpec((tp, HEAD_DIM), lambda b, h, q: (b, 32 + h)),
                  pl.BlockSpec((1, 1, tq, 1), lambda b, h, q: (b, h, q, 0)),
                  pl.BlockSpec((1, 1, 1, tp), lambda b, h, q: (b, h, 0, 0))],
        out_specs=pl.BlockSpec((tq, HEAD_DIM), lambda b, h, q: (b * nqb + q, h)),
        out_shape=jax.ShapeDtypeStruct((n_batch * tp, H_C * HEAD_DIM), F32),
        scratch_shapes=[pltpu.VMEM((tp, HEAD_DIM), BF16), pltpu.VMEM((tp, HEAD_DIM), BF16),
                        pltpu.VMEM((tq, tp), F32), pltpu.VMEM((tq, LANES), F32),
                        pltpu.VMEM((tq, LANES), F32), pltpu.VMEM((tq, HEAD_DIM), F32)],
        compiler_params=_params(("parallel", "parallel", "arbitrary")),
        name="forget_attend_prompt",
    )(p1, p1, p1, cum_col, cum_row)


def _pages_per_step(n_pages, want):
    while n_pages % want:
        want //= 2
    return want


def _idx_sample_kernel(pt_ref, iq_ref, iw_ref, ikn_ref, *rest, topk, n_pages, pps, ds):
    page_refs = rest[:pps]
    bias_ref, score_ref, key_ref, thr_ref, need_ref, jb_ref = rest[pps:]
    j = pl.program_id(1)
    past = n_pages * BLK
    ncol = past + BLK
    nq = iq_ref.shape[1] // H_IDX
    q = iq_ref[0].astype(BF16)
    wgt = iw_ref[0] * (H_IDX ** -0.5)

    def scores(keys):
        dots = _dot_nt(q, keys.astype(BF16)) * (IDX_DIM ** -0.5)
        wd = jnp.maximum(dots, 0.0) * wgt
        return jnp.sum(wd.reshape(nq, H_IDX, BLK), axis=1)

    for i, page_ref in enumerate(page_refs):
        score_ref[:, pl.ds(pl.multiple_of((j * pps + i) * BLK, BLK), BLK)] = scores(page_ref[0])

    @pl.when(j == n_pages // pps - 1)
    def _():
        score_ref[:, past:] = scores(ikn_ref[0])
        t_idx = lax.broadcasted_iota(jnp.int32, (nq, 1), 0) % ds
        s_pos = lax.broadcasted_iota(jnp.int32, (1, ncol), 1)
        vis = s_pos <= past + t_idx
        key_ref[...] = jnp.where(vis, _sortable_key(score_ref[...]), KEY_NEG_INF)
        sel = _topk_select(key_ref, vis, s_pos, topk, ncol, thr_ref, need_ref, jb_ref)
        bias_ref[0] = jnp.where(sel, 0.0, NEG)


def _idx_sample(page_table, iq_th, iw_th, ik_new, cache_idx, topk, ds):
    db, n_pages = page_table.shape
    ncol = n_pages * BLK + BLK
    rows = iq_th.shape[1]
    nq = rows // H_IDX
    pps = _pages_per_step(n_pages, 8)

    def page_spec(i):
        return pl.BlockSpec((1, BLK, IDX_DIM), lambda b, j, pt: (pt[b, j * pps + i], 0, 0))

    return pl.pallas_call(
        functools.partial(_idx_sample_kernel, topk=topk, n_pages=n_pages, pps=pps, ds=ds),
        grid_spec=pltpu.PrefetchScalarGridSpec(
            num_scalar_prefetch=1,
            grid=(db, n_pages // pps),
            in_specs=[pl.BlockSpec((1, rows, IDX_DIM), lambda b, j, pt: (b, 0, 0)),
                      pl.BlockSpec((1, rows, 1), lambda b, j, pt: (b, 0, 0)),
                      pl.BlockSpec((1, BLK, IDX_DIM), lambda b, j, pt: (b, 0, 0))]
                     + [page_spec(i) for i in range(pps)],
            out_specs=pl.BlockSpec((1, nq, ncol), lambda b, j, pt: (b, 0, 0)),
            scratch_shapes=[pltpu.VMEM((nq, ncol), F32),
                            pltpu.VMEM((nq, ncol), jnp.int32),
                            pltpu.VMEM((nq, 1), jnp.int32),
                            pltpu.VMEM((nq, 1), F32),
                            pltpu.VMEM((nq, 1), jnp.int32)]),
        out_shape=jax.ShapeDtypeStruct((db, nq, ncol), F32),
        compiler_params=_params(("parallel", "arbitrary")),
        name="dsa_index_sample",
    )(page_table, iq_th, iw_th, ik_new, *([cache_idx] * pps))


def _softmax_update(s, pv_fn, m_ref, l_ref, acc_ref):
    m_old = m_ref[...]
    m_new = jnp.maximum(m_old, jnp.max(s, axis=1, keepdims=True))
    corr = jnp.exp(m_old - m_new)
    p = jnp.exp(s - m_new)
    l_ref[...] = l_ref[...] * corr + jnp.sum(p, axis=1, keepdims=True)
    acc_ref[...] = acc_ref[...] * corr + pv_fn(p)
    m_ref[...] = m_new


def _head_match(n_rows, heads_per_lane_group, rows_per_head):
    row = lax.broadcasted_iota(jnp.int32, (n_rows, LANES), 0)
    lane = lax.broadcasted_iota(jnp.int32, (n_rows, LANES), 1)
    return (lane % heads_per_lane_group) == (row // rows_per_head), lane, row


def _attn_a_sample_kernel(pt_ref, q_ref, bias_ref, new_ref, *rest, n_steps, pps, ds):
    page_refs = rest[:pps]
    o_ref, rep_ref, m_ref, l_ref, acc_ref = rest[pps:]
    b = pl.program_id(0)
    j = pl.program_id(1)
    n_slots = 2 * HKV_A
    n_rows = H_A * ds
    n_tile = n_rows // bias_ref.shape[1]

    @pl.when((b == 0) & (j == 0))
    def _():
        s_i = lax.broadcasted_iota(jnp.int32, rep_ref.shape, 0)
        c_i = lax.broadcasted_iota(jnp.int32, rep_ref.shape, 1)
        rep_ref[...] = (c_i // n_slots == s_i).astype(BF16)

    @pl.when(j == 0)
    def _():
        m_ref[...] = jnp.full(m_ref.shape, NEG, F32)
        l_ref[...] = jnp.zeros_like(l_ref)
        acc_ref[...] = jnp.zeros_like(acc_ref)

    q = q_ref[0].astype(BF16)
    match, _, _ = _head_match(n_rows, n_slots, ds * (H_A // HKV_A))

    def attend(blocks):
        pgs = [rows.astype(BF16) for rows, _ in blocks]
        pieces = []
        for pg, (_, picked) in zip(pgs, blocks):
            cols = pg.shape[0]
            pick = jnp.concatenate([picked.astype(BF16)] * n_tile, axis=0)
            pick = _dot(pick, rep_ref[:, :cols])
            pieces += [jnp.where(match & (pk > 0.5), sc, NEG)
                       for sc, pk in zip(_lane_chunks(_dot_nt(q, pg) * SCALE), _lane_chunks(pick))]

        def pv(p):
            out, at = 0.0, 0
            for pg in pgs:
                part = p[:, at:at + pg.shape[0]]
                moved = jnp.concatenate([pltpu.roll(x, HKV_A, 1) for x in _lane_chunks(part)], axis=1)
                out = out + _dot(moved.astype(BF16), pg)
                at += pg.shape[0]
            return out

        _softmax_update(jnp.concatenate(pieces, axis=1), pv, m_ref, l_ref, acc_ref)

    def picked(i):
        return jnp.where(bias_ref[0][:, i * BLK:(i + 1) * BLK] == 0.0, 1.0, 0.0)

    @pl.when(j < n_steps)
    def _():
        attend([(page_ref[0], picked(i)) for i, page_ref in enumerate(page_refs)])

    @pl.when(j == n_steps)
    def _():
        attend([(new_ref[0], picked(0))])
        o_ref[0] = acc_ref[...] / l_ref[...]


def _attn_a_sample(page_table, q_rows, bias, new_rows, cache_flat, ds):
    db, n_pages = page_table.shape
    n_slots = 2 * HKV_A
    n_rows = H_A * ds
    pps = _pages_per_step(n_pages, 4)
    n_steps = n_pages // pps
    last = n_steps - 1

    def page_spec(i):
        return pl.BlockSpec((1, BLK * n_slots, HEAD_DIM),
                            lambda b, j, pt: (pt[b, jnp.minimum(j, last) * pps + i], 0, 0))

    return pl.pallas_call(
        functools.partial(_attn_a_sample_kernel, n_steps=n_steps, pps=pps, ds=ds),
        grid_spec=pltpu.PrefetchScalarGridSpec(
            num_scalar_prefetch=1,
            grid=(db, n_steps + 1),
            in_specs=[pl.BlockSpec((1, n_rows, HEAD_DIM), lambda b, j, pt: (b, 0, 0)),
                      pl.BlockSpec((1, bias.shape[1], BLK * pps), lambda b, j, pt: (b, 0, j)),
                      pl.BlockSpec((1, LANES, HEAD_DIM), lambda b, j, pt: (b, 0, 0))]
                     + [page_spec(i) for i in range(pps)],
            out_specs=pl.BlockSpec((1, n_rows, HEAD_DIM), lambda b, j, pt: (b, 0, 0)),
            scratch_shapes=[pltpu.VMEM((BLK, BLK * n_slots), BF16),
                            pltpu.VMEM((n_rows, 1), F32), pltpu.VMEM((n_rows, 1), F32),
                            pltpu.VMEM((n_rows, HEAD_DIM), F32)]),
        out_shape=jax.ShapeDtypeStruct((db, n_rows, HEAD_DIM), F32),
        compiler_params=_params(("arbitrary", "arbitrary")),
        name="dsa_attend_sample",
    )(page_table, q_rows, bias, new_rows, *([cache_flat] * pps))


def _attn_b_sample_kernel(pt_ref, q_ref, knew_ref, vnew_ref, *rest, n_steps, pps, ds):
    page_refs = rest[:pps]
    o_ref, acc_ref, carry_ref = rest[pps:]
    j = pl.program_id(1)
    n_rows = H_B * ds
    q = q_ref[0].astype(BF16)
    match, lane, row = _head_match(n_rows, H_B, ds)
    sums = _suffix_and_ones(H_B)

    def attend(k_rows, v_rows, strict_fn):
        z = _dot_nt(q, k_rows.astype(BF16)) * SCALE
        w, tail = _stick_weights(z, strict_fn, carry_ref[...], sums)
        carry_ref[...] = tail
        acc_ref[...] += _dot(w.astype(BF16), v_rows.astype(BF16))

    @pl.when(j == 0)
    def _():
        acc_ref[...] = jnp.zeros_like(acc_ref)
        carry_ref[...] = jnp.zeros_like(carry_ref)
        s_new = lane // H_B
        attend(knew_ref[0], vnew_ref[0], lambda i: match & (s_new < row % ds))

    @pl.when(j > 0)
    def _():
        for page_ref in page_refs:
            k_rows = page_ref[0, :, 0].reshape(BLK * H_B, HEAD_DIM)
            v_rows = page_ref[0, :, 1].reshape(BLK * H_B, HEAD_DIM)
            attend(k_rows, v_rows, lambda i: match)

    @pl.when(j == n_steps)
    def _():
        o_ref[0] = acc_ref[...]


def _attn_b_sample(page_table, q_rows, k_new, v_new, cache, ds):
    db, n_pages = page_table.shape
    n_rows = H_B * ds
    pps = _pages_per_step(n_pages, 2)
    n_steps = n_pages // pps

    def page_spec(i):
        return pl.BlockSpec((1, BLK, 2, H_B, HEAD_DIM),
                            lambda b, j, pt: (pt[b, n_pages - 1 - (jnp.maximum(j, 1) - 1) * pps - i], 0, 0, 0, 0))

    new_spec = pl.BlockSpec((1, LANES, HEAD_DIM), lambda b, j, pt: (b, 0, 0))
    return pl.pallas_call(
        functools.partial(_attn_b_sample_kernel, n_steps=n_steps, pps=pps, ds=ds),
        grid_spec=pltpu.PrefetchScalarGridSpec(
            num_scalar_prefetch=1,
            grid=(db, n_steps + 1),
            in_specs=[pl.BlockSpec((1, n_rows, HEAD_DIM), lambda b, j, pt: (b, 0, 0)), new_spec, new_spec]
                     + [page_spec(i) for i in range(pps)],
            out_specs=pl.BlockSpec((1, n_rows, HEAD_DIM), lambda b, j, pt: (b, 0, 0)),
            scratch_shapes=[pltpu.VMEM((n_rows, HEAD_DIM), F32), pltpu.VMEM((n_rows, LANES), F32)]),
        out_shape=jax.ShapeDtypeStruct((db, n_rows, HEAD_DIM), F32),
        compiler_params=_params(("parallel", "arbitrary")),
        name="stickbreak_sample",
    )(page_table, q_rows, k_new, v_new, *([cache] * pps))


def _page_suffix_kernel(lf_ref, sfx_ref, tot_ref):
    x = lf_ref[...]
    r = lax.broadcasted_iota(jnp.int32, (BLK, BLK), 0)
    c = lax.broadcasted_iota(jnp.int32, (BLK, BLK), 1)
    sfx_ref[...] = _dot3((c > r).astype(BF16), x)
    tot_ref[...] = _dot3(jnp.ones((8, BLK), BF16), x)


def _page_suffix(lf_t, tn):
    n = lf_t.shape[1]
    return pl.pallas_call(
        _page_suffix_kernel,
        grid=(n // tn,),
        in_specs=[pl.BlockSpec((BLK, tn), lambda i: (0, i))],
        out_specs=[pl.BlockSpec((BLK, tn), lambda i: (0, i)), pl.BlockSpec((8, tn), lambda i: (0, i))],
        out_shape=[jax.ShapeDtypeStruct((BLK, n), F32), jax.ShapeDtypeStruct((8, n), F32)],
        compiler_params=_params(("parallel",)),
        name="log_forget_page_suffix",
    )(lf_t)


def _attn_c_sample_kernel(pt_ref, q_ref, lfnew_ref, knew_ref, vnew_ref, *rest, n_steps, pps, ds):
    sfx_refs, tot_refs, page_refs = rest[:pps], rest[pps:2 * pps], rest[2 * pps:3 * pps]
    o_ref, m_ref, l_ref, acc_ref, carry_ref, cn_ref = rest[3 * pps:]
    j = pl.program_id(1)
    n_rows = H_C * ds
    q = q_ref[0].astype(BF16)
    match, lane, row = _head_match(n_rows, H_C, ds)

    @pl.when(j == 0)
    def _():
        m_ref[...] = jnp.full(m_ref.shape, NEG, F32)
        l_ref[...] = jnp.zeros_like(l_ref)
        acc_ref[...] = jnp.zeros_like(acc_ref)
        carry_ref[...] = jnp.zeros_like(carry_ref)
        lf = lfnew_ref[0]
        s_new = lane // H_C
        cn_ref[...] = jnp.sum(jnp.where(match & (s_new <= row % ds), lf, 0.0), axis=1, keepdims=True)
        r2 = lax.broadcasted_iota(jnp.int32, (LANES, LANES), 0)
        c2 = lax.broadcasted_iota(jnp.int32, (LANES, LANES), 1)
        upto = ((r2 % H_C == c2 % H_C) & (r2 // H_C <= c2 // H_C)).astype(BF16)
        cum_keys = _dot3_left(jnp.broadcast_to(lf, (8, LANES)), upto)[0:1]
        k_rows = knew_ref[0].astype(BF16)
        s = _dot_nt(q, k_rows) * SCALE + (cn_ref[...] - cum_keys)
        s = jnp.where(match & (s_new <= row % ds), s, NEG)
        _softmax_update(s, lambda p: _dot(p.astype(BF16), vnew_ref[0].astype(BF16)), m_ref, l_ref, acc_ref)

    @pl.when(j > 0)
    def _():
        later = carry_ref[...]
        pieces, values = [], []
        for sfx_ref, tot_ref, page_ref in zip(sfx_refs, tot_refs, page_refs):
            k_rows = page_ref[0, :, 0].reshape(BLK * H_C, HEAD_DIM).astype(BF16)
            values.append(page_ref[0, :, 1].reshape(BLK * H_C, HEAD_DIM).astype(BF16))
            s = _dot_nt(q, k_rows) * SCALE + (sfx_ref[0] + (cn_ref[...] + later))
            pieces += [jnp.where(match, piece, NEG) for piece in _lane_chunks(s)]
            later = later + tot_ref[0]
        carry_ref[...] = later
        cols = BLK * H_C

        def pv(p):
            out = 0.0
            for i, v_rows in enumerate(values):
                out = out + _dot(p[:, i * cols:(i + 1) * cols].astype(BF16), v_rows)
            return out

        _softmax_update(jnp.concatenate(pieces, axis=1), pv, m_ref, l_ref, acc_ref)

    @pl.when(j == n_steps)
    def _():
        o_ref[0] = acc_ref[...] / l_ref[...]


def _attn_c_sample(page_table, q_rows, lf_new, k_new, v_new, sfx_flat, tot_col, cache, ds):
    db, n_pages = page_table.shape
    n_rows = H_C * ds
    pps = _pages_per_step(n_pages, 2)
    n_steps = n_pages // pps

    def page_of(i):
        return lambda b, j, pt: pt[b, n_pages - 1 - (jnp.maximum(j, 1) - 1) * pps - i]

    def specs(shape):
        zeros = (0,) * (len(shape) - 1)
        return [pl.BlockSpec(shape, (lambda f: lambda b, j, pt: (f(b, j, pt),) + zeros)(page_of(i)))
                for i in range(pps)]

    new_spec = pl.BlockSpec((1, LANES, HEAD_DIM), lambda b, j, pt: (b, 0, 0))
    return pl.pallas_call(
        functools.partial(_attn_c_sample_kernel, n_steps=n_steps, pps=pps, ds=ds),
        grid_spec=pltpu.PrefetchScalarGridSpec(
            num_scalar_prefetch=1,
            grid=(db, n_steps + 1),
            in_specs=[pl.BlockSpec((1, n_rows, HEAD_DIM), lambda b, j, pt: (b, 0, 0)),
                      pl.BlockSpec((1, 1, LANES), lambda b, j, pt: (b, 0, 0)),
                      new_spec, new_spec]
                     + specs((1, 1, BLK * H_C)) + specs((1, n_rows, 1)) + specs((1, BLK, 2, H_C, HEAD_DIM)),
            out_specs=pl.BlockSpec((1, n_rows, HEAD_DIM), lambda b, j, pt: (b, 0, 0)),
            scratch_shapes=[pltpu.VMEM((n_rows, 1), F32), pltpu.VMEM((n_rows, 1), F32),
                            pltpu.VMEM((n_rows, HEAD_DIM), F32), pltpu.VMEM((n_rows, 1), F32),
                            pltpu.VMEM((n_rows, 1), F32)]),
        out_shape=jax.ShapeDtypeStruct((db, n_rows, HEAD_DIM), F32),
        compiler_params=_params(("parallel", "arbitrary")),
        name="forget_attend_sample",
    )(page_table, q_rows, lf_new, k_new, v_new, *([sfx_flat] * pps), *([tot_col] * pps), *([cache] * pps))


def _layer_norm(xf, g, b):
    mu = jnp.mean(xf, axis=1, keepdims=True)
    d = xf - mu
    var = jnp.mean(d * d, axis=1, keepdims=True)
    return d * lax.rsqrt(var + LN_EPS) * g + b


def _route(y, wr_ref, br_ref):
    y_hi, y_lo = _split2(y)
    w = wr_ref[...]
    p1 = _dot(y_hi, w)
    p2 = _dot(y_lo, w)
    logits = p1 + pltpu.roll(p1, LANES - N_EXPERTS, 1) + p2
    lane = lax.broadcasted_iota(jnp.int32, logits.shape, 1)
    in_grp = lane < N_GROUPS
    s = [jax.nn.sigmoid(logits if j == 0 else pltpu.roll(logits, LANES - j * N_GROUPS, 1))
         for j in range(EPG)]
    sel = [jnp.where(in_grp, s[j] + br_ref[j:j + 1, :], NEG) for j in range(EPG)]
    top2 = None
    for a in range(EPG):
        for b in range(a + 1, EPG):
            pair = sel[a] + sel[b]
            top2 = pair if top2 is None else jnp.maximum(top2, pair)
    top2 = jnp.where(in_grp, top2, -jnp.inf)
    best_val = jnp.max(top2, axis=1, keepdims=True)
    g_best = jnp.min(jnp.where(top2 == best_val, lane, LANES), axis=1, keepdims=True)
    mine = lane == g_best
    picked = []
    for j in range(EPG):
        rank = jnp.zeros(logits.shape, F32)
        for i in range(EPG):
            if i == j:
                continue
            ahead = (sel[i] >= sel[j]) if i < j else (sel[i] > sel[j])
            rank = rank + ahead.astype(F32)
        picked.append(jnp.sum(jnp.where(mine & (rank < 2.0), s[j], 0.0), axis=1, keepdims=True))
    denom = picked[0] + picked[1] + picked[2] + picked[3]
    extra = jnp.where(lane == EPG, g_best.astype(F32), 0.0)
    for j in range(EPG):
        extra = jnp.where(lane == j, picked[j] / denom, extra)
    return extra


def _outproj_kernel(o1_ref, o2_ref, w1_ref, w2_ref, x_ref, g_ref, b_ref, wr_ref, br_ref, out_ref, *, d_model):
    mix = _dot(o1_ref[...].astype(BF16), w1_ref[...]) + _dot(o2_ref[...].astype(BF16), w2_ref[...])
    y = _layer_norm(ALPHA * x_ref[...] + mix, g_ref[...], b_ref[...])
    out_ref[:, :d_model] = y
    out_ref[:, d_model:] = _route(y, wr_ref, br_ref)


def _outproj(o1, o2, o2_block, w_bf16, x, g, b, wr, br, d_model, tm):
    nf = x.shape[0]
    half = w_bf16.shape[0] // 2
    return pl.pallas_call(
        functools.partial(_outproj_kernel, d_model=d_model),
        grid=(_cdiv(nf, tm),),
        in_specs=[pl.BlockSpec((tm, half), lambda i: (i, 0)),
                  pl.BlockSpec((tm, half), lambda i: (i, o2_block)),
                  pl.BlockSpec((half, d_model), lambda i: (0, 0)),
                  pl.BlockSpec((half, d_model), lambda i: (1, 0)),
                  pl.BlockSpec((tm, d_model), lambda i: (i, 0)),
                  pl.BlockSpec((1, d_model), lambda i: (0, 0)),
                  pl.BlockSpec((1, d_model), lambda i: (0, 0)),
                  pl.BlockSpec((d_model, LANES), lambda i: (0, 0)),
                  pl.BlockSpec((EPG, LANES), lambda i: (0, 0))],
        out_specs=pl.BlockSpec((tm, d_model + XCOLS), lambda i: (i, 0)),
        out_shape=jax.ShapeDtypeStruct((nf, d_model + XCOLS), F32),
        compiler_params=_params(("parallel",)),
        name="outproj_norm_route",
    )(o1, o2, w_bf16, w_bf16, x, g, b, wr, br)


def _gather_rows(idx_ref, base, src_ref, dst_ref, sem, n_rows):
    def issue(r, c):
        pltpu.make_async_copy(src_ref.at[pl.ds(idx_ref[base + r], 1)], dst_ref.at[pl.ds(r, 1)], sem).start()
        return c

    lax.fori_loop(0, n_rows, issue, 0)

    def drain(r, c):
        pltpu.make_async_copy(src_ref.at[pl.ds(0, 1)], dst_ref.at[pl.ds(r, 1)], sem).wait()
        return c

    lax.fori_loop(0, n_rows, drain, 0)


def _moe_kernel(grp_ref, valid_ref, src_ref, xa_ref, wg_ref, wu_ref, wd_ref, g_ref, b_ref, o_ref,
                xs_ref, xb_ref, acc_ref, sem, *, d_model, tm):
    i = pl.program_id(0)
    e = pl.program_id(1)
    valid = valid_ref[i] == 1

    @pl.when(valid & (e == 0))
    def _():
        _gather_rows(src_ref, i * tm, xa_ref, xs_ref, sem, tm)
        xb_ref[...] = xs_ref[:, :d_model].astype(BF16)
        acc_ref[...] = jnp.zeros_like(acc_ref)

    @pl.when(valid)
    def _():
        xb = xb_ref[...]
        a = _dot(xb, wg_ref[0])
        h = a * jax.nn.sigmoid(a) * _dot(xb, wu_ref[0])
        extra = xs_ref[:, d_model:]
        lane = lax.broadcasted_iota(jnp.int32, extra.shape, 1)
        gate = jnp.sum(jnp.where(lane == e, extra, 0.0), axis=1, keepdims=True)
        acc_ref[...] += _dot((h * gate).astype(BF16), wd_ref[0])

    @pl.when(valid & (e == EPG - 1))
    def _():
        o_ref[...] = _layer_norm(ALPHA * xs_ref[:, :d_model] + acc_ref[...], g_ref[...], b_ref[...])

    @pl.when(jnp.logical_not(valid) & (e == EPG - 1))
    def _():
        o_ref[...] = jnp.zeros_like(o_ref)


def _moe(tile_grp, tile_valid, src, xa, wg, wu, wd, g, b, d_model, tm):
    n_tiles = tile_grp.shape[0]
    d_exp = wg.shape[2]

    def w_idx(i, e, grp, valid, src):
        return (grp[i] * EPG + jnp.where(valid[i] == 1, e, EPG - 1), 0, 0)

    return pl.pallas_call(
        functools.partial(_moe_kernel, d_model=d_model, tm=tm),
        grid_spec=pltpu.PrefetchScalarGridSpec(
            num_scalar_prefetch=3,
            grid=(n_tiles, EPG),
            in_specs=[pl.BlockSpec(memory_space=pl.ANY),
                      pl.BlockSpec((1, d_model, d_exp), w_idx),
                      pl.BlockSpec((1, d_model, d_exp), w_idx),
                      pl.BlockSpec((1, d_exp, d_model), w_idx),
                      pl.BlockSpec((1, d_model), lambda i, e, *_: (0, 0)),
                      pl.BlockSpec((1, d_model), lambda i, e, *_: (0, 0))],
            out_specs=pl.BlockSpec((tm, d_model), lambda i, e, *_: (i, 0)),
            scratch_shapes=[pltpu.VMEM((tm, d_model + XCOLS), F32),
                            pltpu.VMEM((tm, d_model), BF16),
                            pltpu.VMEM((tm, d_model), F32),
                            pltpu.SemaphoreType.DMA(())]),
        out_shape=jax.ShapeDtypeStruct((n_tiles * tm, d_model), F32),
        compiler_params=_params(("arbitrary", "arbitrary")),
        name="grouped_moe",
    )(tile_grp, tile_valid, src, xa, wg, wu, wd, g, b)


def _unpermute_kernel(idx_ref, src_ref, o_ref, sem, *, tg):
    _gather_rows(idx_ref, pl.program_id(0) * tg, src_ref, o_ref, sem, tg)


def _unpermute(dest_padded, ys, n_rows, tg):
    d = ys.shape[1]
    return pl.pallas_call(
        functools.partial(_unpermute_kernel, tg=tg),
        grid_spec=pltpu.PrefetchScalarGridSpec(
            num_scalar_prefetch=1,
            grid=(_cdiv(n_rows, tg),),
            in_specs=[pl.BlockSpec(memory_space=pl.ANY)],
            out_specs=pl.BlockSpec((tg, d), lambda i, idx: (i, 0)),
            scratch_shapes=[pltpu.SemaphoreType.DMA(())]),
        out_shape=jax.ShapeDtypeStruct((n_rows, d), F32),
        compiler_params=_params(("arbitrary",)),
        name="unpermute_rows",
    )(dest_padded, ys)


def _routing_plan(grp, tm):
    nf = grp.shape[0]
    n_tiles = _cdiv(nf + N_GROUPS * (tm - 1), tm)
    onehot = (grp[:, None] == jnp.arange(N_GROUPS, dtype=jnp.int32)[None, :]).astype(jnp.int32)
    counts = jnp.sum(onehot, axis=0)
    rank = jnp.sum((jnp.cumsum(onehot, axis=0) - onehot) * onehot, axis=1)
    padded = ((counts + tm - 1) // tm) * tm
    ends = jnp.cumsum(padded)
    dest = (ends - padded)[grp] + rank
    src = jnp.zeros((n_tiles * tm,), jnp.int32).at[dest].set(jnp.arange(nf, dtype=jnp.int32))
    starts = jnp.arange(n_tiles, dtype=jnp.int32) * tm
    tile_valid = (starts < ends[-1]).astype(jnp.int32)
    tile_grp = jnp.minimum(jnp.searchsorted(ends, starts, side="right"), N_GROUPS - 1).astype(jnp.int32)
    last_grp = tile_grp[jnp.maximum(ends[-1] // tm - 1, 0)]
    tile_grp = jnp.where(tile_valid == 1, tile_grp, last_grp)
    return tile_grp, tile_valid, src, dest


def _ffn(xa, wg, wu, wd, g, b, d_model, tm, tg):
    nf = xa.shape[0]
    grp = xa[:, d_model + EPG].astype(jnp.int32)
    tile_grp, tile_valid, src, dest = _routing_plan(grp, tm)
    ys = _moe(tile_grp, tile_valid, src, xa, wg, wu, wd, g, b, d_model, tm)
    dest_padded = jnp.zeros((_cdiv(nf, tg) * tg,), jnp.int32).at[:nf].set(dest)
    return _unpermute(dest_padded, ys, nf, tg)


def _rope_tables(pos):
    def table(dim):
        half = dim // 2
        inv = ROPE_THETA ** (-jnp.arange(half, dtype=F32) / half)
        ang = pos.astype(F32)[:, None] * inv[None, :]
        cos = jnp.cos(ang)
        sin = jnp.sin(ang)
        reps = TN // dim
        return jnp.tile(jnp.concatenate([cos, cos], axis=1), (1, reps)), \
            jnp.tile(jnp.concatenate([-sin, sin], axis=1), (1, reps))

    c128, s128 = table(HEAD_DIM)
    c64, s64 = table(IDX_DIM)
    return c128, s128, c64, s64


def _pad_rows(a, rows):
    return jnp.pad(a, [(0, 0), (0, rows - a.shape[1])] + [(0, 0)] * (a.ndim - 2))


def _head_rows(a, db, ds, heads):
    return a.reshape(db, ds, heads, HEAD_DIM).transpose(0, 2, 1, 3).reshape(db, heads * ds, HEAD_DIM)


def _token_rows(a, db, ds, heads):
    return a.reshape(db, heads, ds, HEAD_DIM).transpose(0, 2, 1, 3).reshape(db * ds, heads * HEAD_DIM)


def _new_block(a, db):
    slots = a.shape[2]
    return _pad_rows(a, LANES // slots).reshape(db, LANES, HEAD_DIM)


def kernel(x_prompt, x_sample, cache_l0_a_kv, cache_l0_idx_k, cache_l0_b_kv, cache_l1_c_kv, cache_l1_logf,
           page_table, meta_tokens, w_in_l0, w_out_l0, w_in_l1, b_forget_l1, w_out_l1, ln_mix_g, ln_mix_b,
           ln_ffn_g, ln_ffn_b, w_router, b_router, w_gate, w_up, w_down):
    n_batch, seq, d_model = x_prompt.shape
    db, ds, _ = x_sample.shape
    assert ds <= 8 and d_model % LANES == 0
    t_len = seq + N_META
    tp = _cdiv(t_len, BLK) * BLK
    n_prompt = n_batch * tp
    n_sample = db * ds
    nf = n_prompt + n_sample
    n_pool = cache_l0_a_kv.shape[0]
    n_pages = page_table.shape[1]
    past = n_pages * BLK
    topk_prompt = min(TOPK_MAX, seq // 4)
    topk_sample = min(TOPK_MAX, (past + ds) // 4)
    tm_proj = 536 if nf % 536 == 0 else 128
    tm_out = 256
    tm_moe = 512
    tg = 536 if nf % 536 == 0 else 8

    meta = jnp.broadcast_to(meta_tokens[None], (n_batch, N_META, d_model)).astype(x_prompt.dtype)
    hp = _pad_rows(jnp.concatenate([meta, x_prompt], axis=1), tp)
    x0 = jnp.concatenate([hp.reshape(n_prompt, d_model), x_sample.reshape(n_sample, d_model)], axis=0)
    pos = jnp.concatenate([jnp.tile(jnp.arange(tp), n_batch), jnp.tile(past + jnp.arange(ds), db)])
    tables = _rope_tables(pos)

    cuts = np.cumsum((0, H_A * HEAD_DIM, HKV_A * HEAD_DIM, HKV_A * HEAD_DIM, H_B * HEAD_DIM, H_B * HEAD_DIM,
                      H_B * HEAD_DIM, H_IDX * IDX_DIM, IDX_DIM, H_IDX))
    seg = [w_in_l0[:, cuts[i]:cuts[i + 1]] for i in range(9)]
    qa_w, ka_w, va_w, qb_w, kb_w, vb_w, iq_w, ik_w, iw_w = seg
    zeros = lambda n: jnp.zeros((d_model, n), w_in_l0.dtype)
    w0 = jnp.concatenate([qa_w, iq_w, ka_w, va_w, qb_w, kb_w, vb_w, ik_w, zeros(LANES - IDX_DIM),
                          iw_w, zeros(LANES - H_IDX)], axis=1).astype(BF16)
    modes0 = jnp.asarray([1] * 4 + [2] * 4 + [1] * 2 + [0] * 14 + [3], jnp.int32)
    n_qkv = 3 * H_C * HEAD_DIM
    w1 = jnp.concatenate([w_in_l1[:, :n_qkv], w_in_l1[:, n_qkv:], zeros(TN - H_C)], axis=1).astype(BF16)
    bf_row = jnp.zeros((1, LANES), F32).at[0, :H_C].set(b_forget_l1)
    w_out0 = w_out_l0.astype(BF16)
    w_out1 = w_out_l1.astype(BF16)
    wr_perm = w_router.reshape(d_model, N_GROUPS, EPG).transpose(0, 2, 1).reshape(d_model, N_EXPERTS)
    wr_hi = wr_perm.astype(BF16)
    wr_lo = (wr_perm - wr_hi.astype(F32)).astype(BF16)
    wr = jnp.concatenate([wr_hi, wr_lo, jnp.zeros((d_model, LANES - 2 * N_EXPERTS), BF16)], axis=1)
    br = jnp.zeros((EPG, LANES), F32).at[:, :N_GROUPS].set(b_router.reshape(N_GROUPS, EPG).T)
    wg = w_gate.astype(BF16)
    wu = w_up.astype(BF16)
    wd = w_down.astype(BF16)
    row = lambda v: v.reshape(1, d_model)

    p0 = _inproj(x0, w0, d_model, tm_proj, modes0, tables)
    col = lambda blk0, n: slice(blk0 * LANES, (blk0 + n) * LANES)
    ps = p0[n_prompt:]

    bias_p = _idx_prompt(p0, n_batch, tp, topk_prompt)
    oa_p = _attn_a_prompt(p0, bias_p, n_batch, tp)
    ob_p = _attn_b_prompt(p0, n_batch, tp)

    grp_q = H_A // HKV_A
    iq_th = jnp.tile(ps[:, col(8, 8)].reshape(db, ds * H_IDX, IDX_DIM), (1, grp_q, 1))
    iw_th = jnp.tile(ps[:, 49 * LANES:49 * LANES + H_IDX].reshape(db, ds * H_IDX, 1), (1, grp_q, 1))
    ik_s = ps[:, 48 * LANES:48 * LANES + IDX_DIM].reshape(db, ds, IDX_DIM)
    bias_s = _idx_sample(page_table, iq_th, iw_th, _pad_rows(ik_s, BLK), cache_l0_idx_k, topk_sample, ds)

    ka_s = ps[:, col(16, 4)].reshape(db, ds, HKV_A, HEAD_DIM)
    va_s = ps[:, col(20, 4)].reshape(db, ds, HKV_A, HEAD_DIM)
    kv_a_s = jnp.stack([ka_s, va_s], axis=2)
    oa_s = _attn_a_sample(page_table, _head_rows(ps[:, col(0, 8)], db, ds, H_A), bias_s,
                          _new_block(kv_a_s.reshape(db, ds, 2 * HKV_A, HEAD_DIM), db),
                          cache_l0_a_kv.reshape(n_pool, BLK * 2 * HKV_A, HEAD_DIM), ds)
    oa_s = _token_rows(oa_s, db, ds, H_A)

    kb_s = ps[:, col(32, 8)].reshape(db, ds, H_B, HEAD_DIM)
    vb_s = ps[:, col(40, 8)].reshape(db, ds, H_B, HEAD_DIM)
    kv_b_s = jnp.stack([kb_s, vb_s], axis=2)
    ob_s = _attn_b_sample(page_table, _head_rows(ps[:, col(24, 8)], db, ds, H_B),
                          _new_block(kb_s, db), _new_block(vb_s, db), cache_l0_b_kv, ds)
    ob_s = _token_rows(ob_s, db, ds, H_B)

    oa = jnp.concatenate([oa_p, oa_s], axis=0)
    ob = jnp.concatenate([ob_p, ob_s], axis=0)
    xa1 = _outproj(oa, ob, 0, w_out0, x0, row(ln_mix_g[0]), row(ln_mix_b[0]), wr, br, d_model, tm_out)
    x1 = _ffn(xa1, wg[0], wu[0], wd[0], row(ln_ffn_g[0]), row(ln_ffn_b[0]), d_model, tm_moe, tg)

    p1 = _inproj(x1, w1, d_model, tm_proj)
    logf, cum = _logf(p1, bf_row, nf, tp // BLK)
    cum_t = cum[:n_prompt, :H_C].reshape(n_batch, tp, H_C).transpose(0, 2, 1)
    oc_p = _attn_c_prompt(p1, cum_t[..., None], cum_t[:, :, None, :], n_batch, tp)

    ps1 = p1[n_prompt:]
    kc_s = ps1[:, col(16, 16)].reshape(db, ds, H_C, HEAD_DIM)
    vc_s = ps1[:, col(32, 16)].reshape(db, ds, H_C, HEAD_DIM)
    kv_c_s = jnp.stack([kc_s, vc_s], axis=2)
    logf_s = logf[n_prompt:, :H_C].reshape(db, ds, H_C)
    lf_new = _pad_rows(logf_s, LANES // H_C).reshape(db, 1, LANES)
    n_lf = n_pool * H_C
    tn_lf = min(2048, _cdiv(n_lf, LANES) * LANES)
    n_lf_pad = _cdiv(n_lf, tn_lf) * tn_lf
    lf_t = jnp.pad(cache_l1_logf.astype(F32).transpose(1, 0, 2).reshape(BLK, n_lf), [(0, 0), (0, n_lf_pad - n_lf)])
    sfx_t, tot = _page_suffix(lf_t, tn_lf)
    sfx_flat = sfx_t[:, :n_lf].reshape(BLK, n_pool, H_C).transpose(1, 0, 2).reshape(n_pool, 1, BLK * H_C)
    tot_col = jnp.repeat(tot[0, :n_lf].reshape(n_pool, H_C), ds, axis=1)[..., None]
    oc_s = _attn_c_sample(page_table, _head_rows(ps1[:, col(0, 16)], db, ds, H_C), lf_new,
                          _new_block(kc_s, db), _new_block(vc_s, db), sfx_flat, tot_col, cache_l1_c_kv, ds)
    oc_s = _token_rows(oc_s, db, ds, H_C)

    oc = jnp.concatenate([oc_p, oc_s], axis=0)
    xa2 = _outproj(oc, oc, 1, w_out1, x1, row(ln_mix_g[1]), row(ln_mix_b[1]), wr, br, d_model, tm_out)
    x2 = _ffn(xa2, wg[1], wu[1], wd[1], row(ln_ffn_g[1]), row(ln_ffn_b[1]), d_model, tm_moe, tg)

    def prompt_rows(a, blk0, heads):
        return a[:n_prompt, col(blk0, heads)].reshape(n_batch, tp, heads, HEAD_DIM)[:, :t_len]

    y_prompt = x2[:n_prompt].reshape(n_batch, tp, d_model)[:, N_META:t_len]
    y_sample = x2[n_prompt:].reshape(db, ds, d_model)
    a_kv_p = jnp.stack([prompt_rows(p0, 16, HKV_A), prompt_rows(p0, 20, HKV_A)], axis=2)
    idx_k_p = p0[:n_prompt, 48 * LANES:48 * LANES + IDX_DIM].reshape(n_batch, tp, IDX_DIM)[:, :t_len]
    b_kv_p = jnp.stack([prompt_rows(p0, 32, H_B), prompt_rows(p0, 40, H_B)], axis=2)
    c_kv_p = jnp.stack([prompt_rows(p1, 16, H_C), prompt_rows(p1, 32, H_C)], axis=2)
    logf_p = logf[:n_prompt, :H_C].reshape(n_batch, tp, H_C)[:, :t_len]
    return (y_prompt, y_sample, a_kv_p, kv_a_s, idx_k_p, ik_s, b_kv_p, kv_b_s, c_kv_p, kv_c_s, logf_p, logf_s)
```

```python
import functools

import numpy as np
import jax
import jax.numpy as jnp
from jax import lax
from jax.experimental import pallas as pl
from jax.experimental.pallas import tpu as pltpu

HEAD_DIM = 128
H_A = 8
HKV_A = 4
H_B = 8
H_C = 16
H_IDX = 16
IDX_DIM = 64
TOPK_MAX = 256
N_META = 16
BLK = 128
ROPE_THETA = 10000.0
N_EXPERTS = 32
N_GROUPS = 8
EPG = N_EXPERTS // N_GROUPS
LN_EPS = 1e-5
DEPTH = 2
ALPHA = (2 * DEPTH) ** 0.25
NEG = -1e30
INT_MIN = -2 ** 31
KEY_NEG_INF = -2139095041
LANES = 128
TN = 1280
MODE_COLS = 256
XCOLS = 128
VMEM_LIMIT = 56 * 1024 * 1024
SCALE = HEAD_DIM ** -0.5

F32 = jnp.float32
BF16 = jnp.bfloat16
NT_DIMS = (((1,), (1,)), ((), ()))


def _cdiv(a, b):
    return (a + b - 1) // b


def _dot(a, b):
    return jnp.dot(a, b, preferred_element_type=F32)


def _dot_nt(a, b):
    return lax.dot_general(a, b, NT_DIMS, preferred_element_type=F32)


def _split2(x):
    hi = x.astype(BF16)
    lo = (x - hi.astype(F32)).astype(BF16)
    return hi, lo


def _split3(x):
    hi = x.astype(BF16)
    r = x - hi.astype(F32)
    mid = r.astype(BF16)
    lo = (r - mid.astype(F32)).astype(BF16)
    return hi, mid, lo


def _dot3(a_bf16, x):
    hi, mid, lo = _split3(x)
    return _dot(a_bf16, hi) + _dot(a_bf16, mid) + _dot(a_bf16, lo)


def _dot3_left(x, a_bf16):
    hi, mid, lo = _split3(x)
    return _dot(hi, a_bf16) + _dot(mid, a_bf16) + _dot(lo, a_bf16)


def _neg_softplus(z):
    return -(jnp.maximum(z, 0.0) + jnp.log1p(jnp.exp(-jnp.abs(z))))


def _neg_softplus_bulk(z):
    return -(jnp.maximum(z, 0.0) + jnp.log(1.0 + jnp.exp(-jnp.abs(z))))


def _lane_chunks(x):
    return [x[:, c * LANES:(c + 1) * LANES] for c in range(x.shape[1] // LANES)]


def _params(sem, vmem=VMEM_LIMIT):
    return pltpu.CompilerParams(dimension_semantics=sem, vmem_limit_bytes=vmem)


def _rope128(a, c, s):
    return a * c + pltpu.roll(a, 64, 1) * s


def _rope64(a, c, s):
    lane = lax.broadcasted_iota(jnp.int32, a.shape, 1)
    first = (lane % 64) < 32
    partner = jnp.where(first, pltpu.roll(a, 96, 1), pltpu.roll(a, 32, 1))
    return a * c + partner * s


def _inproj_rope_kernel(modes_ref, x_ref, w_ref, c128_ref, s128_ref, c64_ref, s64_ref, o_ref, xb_ref, *, tn):
    j = pl.program_id(1)

    @pl.when(j == 0)
    def _():
        xb_ref[...] = x_ref[...].astype(BF16)

    acc = _dot(xb_ref[...], w_ref[...])
    chunk = lambda c: slice(c * LANES, (c + 1) * LANES)
    per_tile = tn // MODE_COLS
    for t in range(per_tile):
        mode = modes_ref[j * per_tile + t]
        c0 = t * (MODE_COLS // LANES)
        span = slice(t * MODE_COLS, (t + 1) * MODE_COLS)

        @pl.when(mode == 0)
        def _():
            o_ref[:, span] = acc[:, span]

        @pl.when(mode == 1)
        def _():
            for c in range(c0, c0 + MODE_COLS // LANES):
                o_ref[:, chunk(c)] = _rope128(acc[:, chunk(c)], c128_ref[...], s128_ref[...])

        @pl.when(mode == 2)
        def _():
            for c in range(c0, c0 + MODE_COLS // LANES):
                o_ref[:, chunk(c)] = _rope64(acc[:, chunk(c)], c64_ref[...], s64_ref[...])

        @pl.when(mode == 3)
        def _():
            o_ref[:, chunk(c0)] = _rope64(acc[:, chunk(c0)], c64_ref[...], s64_ref[...])
            rest = slice((c0 + 1) * LANES, (t + 1) * MODE_COLS)
            o_ref[:, rest] = acc[:, rest]


def _inproj_plain_kernel(x_ref, w_ref, o_ref, xb_ref):
    @pl.when(pl.program_id(1) == 0)
    def _():
        xb_ref[...] = x_ref[...].astype(BF16)

    o_ref[...] = _dot(xb_ref[...], w_ref[...])


def _inproj(x, w_bf16, d_model, tm, modes=None, tables=None):
    nf = x.shape[0]
    ncols = w_bf16.shape[1]
    tn = TN if ncols % TN == 0 else MODE_COLS
    grid = (_cdiv(nf, tm), ncols // tn)
    scratch = [pltpu.VMEM((tm, d_model), BF16)]
    out_shape = jax.ShapeDtypeStruct((nf, ncols), F32)
    if modes is None:
        return pl.pallas_call(
            _inproj_plain_kernel,
            grid=grid,
            in_specs=[pl.BlockSpec((tm, d_model), lambda i, j: (i, 0)),
                      pl.BlockSpec((d_model, tn), lambda i, j: (0, j))],
            out_specs=pl.BlockSpec((tm, tn), lambda i, j: (i, j)),
            out_shape=out_shape,
            scratch_shapes=scratch,
            compiler_params=_params(("parallel", "arbitrary")),
            name="inproj_plain",
        )(x, w_bf16)
    tab_spec = pl.BlockSpec((tm, LANES), lambda i, j, m: (i, 0))
    return pl.pallas_call(
        functools.partial(_inproj_rope_kernel, tn=tn),
        grid_spec=pltpu.PrefetchScalarGridSpec(
            num_scalar_prefetch=1,
            grid=grid,
            in_specs=[pl.BlockSpec((tm, d_model), lambda i, j, m: (i, 0)),
                      pl.BlockSpec((d_model, tn), lambda i, j, m: (0, j)),
                      tab_spec, tab_spec, tab_spec, tab_spec],
            out_specs=pl.BlockSpec((tm, tn), lambda i, j, m: (i, j)),
            scratch_shapes=scratch),
        out_shape=out_shape,
        compiler_params=_params(("parallel", "arbitrary")),
        name="inproj_rope",
    )(modes, x, w_bf16, *tables)


def _sortable_key(score):
    score = jnp.where(score == 0.0, 0.0, score)
    bits = lax.bitcast_convert_type(score, jnp.int32)
    return bits ^ ((bits >> 31) & 0x7FFFFFFF)


def _count(mask):
    return jnp.sum(mask.astype(F32), axis=1, keepdims=True)


def _topk_select(key_ref, vis, s_pos, topk, n_cols, thr_ref, need_ref, jb_ref):
    kf = float(topk)
    c0 = _count(key_ref[...] >= 0)
    ans0 = jnp.where(c0 >= kf, 0, INT_MIN).astype(jnp.int32)

    def body(i, ans):
        cand = ans | jnp.left_shift(jnp.int32(1), 30 - i)
        cnt = _count(key_ref[...] >= cand)
        return jnp.where(cnt >= kf, cand, ans)

    thr = lax.fori_loop(0, 31, body, ans0)
    key = key_ref[...]
    need = kf - _count(key > thr)
    n_eq = _count((key == thr) & vis)
    thr_ref[...] = thr
    need_ref[...] = need
    jb_ref[...] = jnp.full(jb_ref.shape, n_cols, jnp.int32)
    n_bits = int(np.ceil(np.log2(n_cols))) + 1

    @pl.when(jnp.max(n_eq - need) > 0.0)
    def _():
        def body2(i, ans):
            cand = ans | jnp.left_shift(jnp.int32(1), n_bits - 1 - i)
            eqv = (key_ref[...] == thr_ref[...]) & vis
            c = _count(eqv & (s_pos < cand))
            return jnp.where(c < need_ref[...], cand, ans)

        jb_ref[...] = lax.fori_loop(0, n_bits, body2, jnp.zeros(jb_ref.shape, jnp.int32))

    return vis & ((key > thr) | ((key == thr) & (s_pos <= jb_ref[...])))


def _causal_extents(tp, n=4):
    nb = tp // BLK
    return [BLK * e for e in sorted({_cdiv(nb * (k + 1), n) for k in range(n)})]


def _for_causal_extent(qi, tp, body):
    lo = 0
    for extent in _causal_extents(tp):
        need = (qi + 1) * BLK
        pl.when((need > lo) & (need <= extent))(functools.partial(body, extent))
        lo = extent


def _idx_prompt_kernel(iq_ref, iw_ref, ik_ref, bias_ref, ikd_ref, key_ref, thr_ref, need_ref, jb_ref,
                       *, topk, tp):
    qi = pl.program_id(1)

    @pl.when(qi == 0)
    def _():
        ik = ik_ref[...]
        ikd_ref[...] = (ik + pltpu.roll(ik, 64, 1)).astype(BF16)

    def select(n_keys):
        lane = lax.broadcasted_iota(jnp.int32, (1, LANES), 1)
        keys = ikd_ref[:n_keys, :]
        score = jnp.zeros((BLK, n_keys), F32)
        for h in range(H_IDX):
            pair = iq_ref[:, (h // 2) * LANES:(h // 2 + 1) * LANES]
            lo = (h % 2) * IDX_DIM
            qh = jnp.where((lane >= lo) & (lane < lo + IDX_DIM), pair, 0.0).astype(BF16)
            w = iw_ref[:, h:h + 1] * (H_IDX ** -0.5 * IDX_DIM ** -0.5)
            score = score + jnp.maximum(_dot_nt(qh, keys), 0.0) * w

        t_pos = qi * BLK + lax.broadcasted_iota(jnp.int32, (BLK, 1), 0)
        s_pos = lax.broadcasted_iota(jnp.int32, (1, n_keys), 1)
        vis = s_pos <= t_pos
        keys_view = key_ref.at[:, :n_keys]
        keys_view[...] = jnp.where(vis, _sortable_key(score), KEY_NEG_INF)
        sel = _topk_select(keys_view, vis, s_pos, topk, n_keys, thr_ref, need_ref, jb_ref)
        bias_ref[:, :n_keys] = jnp.where(sel, 0.0, NEG).astype(BF16)
        if n_keys < tp:
            bias_ref[:, n_keys:] = jnp.full((BLK, tp - n_keys), NEG, BF16)

    _for_causal_extent(qi, tp, select)


def _idx_prompt(p0, n_batch, tp, topk):
    nqb = tp // BLK
    return pl.pallas_call(
        functools.partial(_idx_prompt_kernel, topk=topk, tp=tp),
        grid=(n_batch, nqb),
        in_specs=[pl.BlockSpec((BLK, 1024), lambda b, q: (b * nqb + q, 1)),
                  pl.BlockSpec((BLK, LANES), lambda b, q: (b * nqb + q, 49)),
                  pl.BlockSpec((tp, LANES), lambda b, q: (b, 48))],
        out_specs=pl.BlockSpec((BLK, tp), lambda b, q: (b * nqb + q, 0)),
        out_shape=jax.ShapeDtypeStruct((n_batch * tp, tp), BF16),
        scratch_shapes=[pltpu.VMEM((tp, LANES), BF16),
                        pltpu.VMEM((BLK, tp), jnp.int32),
                        pltpu.VMEM((BLK, 1), jnp.int32),
                        pltpu.VMEM((BLK, 1), F32),
                        pltpu.VMEM((BLK, 1), jnp.int32)],
        compiler_params=_params(("parallel", "arbitrary")),
        name="dsa_index_prompt",
    )(p0, p0, p0)


def _attn_a_prompt_kernel(q_ref, k_ref, v_ref, bias_ref, o_ref, kb_ref, vb_ref, *, tp):
    qi = pl.program_id(2)

    @pl.when(qi == 0)
    def _():
        kb_ref[...] = k_ref[...].astype(BF16)
        vb_ref[...] = v_ref[...].astype(BF16)

    def attend(n_keys):
        bias = bias_ref[:, :n_keys].astype(F32)
        for g in range(H_A // HKV_A):
            sl = slice(g * HEAD_DIM, (g + 1) * HEAD_DIM)
            s = _dot_nt(q_ref[:, sl].astype(BF16), kb_ref[:n_keys, :]) * SCALE + bias
            m = jnp.max(s, axis=1, keepdims=True)
            p = jnp.exp(s - m)
            l = jnp.sum(p, axis=1, keepdims=True)
            o_ref[:, sl] = _dot(p.astype(BF16), vb_ref[:n_keys, :]) / l

    _for_causal_extent(qi, tp, attend)


def _attn_a_prompt(p0, bias, n_batch, tp):
    nqb = tp // BLK
    gw = (H_A // HKV_A) * HEAD_DIM
    return pl.pallas_call(
        functools.partial(_attn_a_prompt_kernel, tp=tp),
        grid=(n_batch, HKV_A, nqb),
        in_specs=[pl.BlockSpec((BLK, gw), lambda b, k, q: (b * nqb + q, k)),
                  pl.BlockSpec((tp, HEAD_DIM), lambda b, k, q: (b, 16 + k)),
                  pl.BlockSpec((tp, HEAD_DIM), lambda b, k, q: (b, 20 + k)),
                  pl.BlockSpec((BLK, tp), lambda b, k, q: (b * nqb + q, 0))],
        out_specs=pl.BlockSpec((BLK, gw), lambda b, k, q: (b * nqb + q, k)),
        out_shape=jax.ShapeDtypeStruct((n_batch * tp, H_A * HEAD_DIM), F32),
        scratch_shapes=[pltpu.VMEM((tp, HEAD_DIM), BF16), pltpu.VMEM((tp, HEAD_DIM), BF16)],
        compiler_params=_params(("parallel", "parallel", "arbitrary")),
        name="dsa_attend_prompt",
    )(p0, p0, p0, bias)


def _suffix_and_ones(group):
    r = lax.broadcasted_iota(jnp.int32, (LANES, 2 * LANES), 0)
    c = lax.broadcasted_iota(jnp.int32, (LANES, 2 * LANES), 1)
    return ((c >= LANES) | (r // group > c // group)).astype(BF16)


def _pick_tq(tp):
    return 384 if tp % 384 == 0 else BLK


def _stick_weights(z, strict_fn, tail, sums):
    rows = z.shape[0]
    pieces = _lane_chunks(z)
    n = len(pieces)
    lsn = [_neg_softplus_bulk(p) for p in pieces]
    ok = [strict_fn(i) for i in range(n)]
    hi, lo = _split2(jnp.concatenate([jnp.where(ok[i], lsn[i], 0.0) for i in range(n)], axis=0))
    ar = _dot(hi, sums) + _dot(lo, sums)
    w = [None] * n
    for i in reversed(range(n)):
        blk = ar[i * rows:(i + 1) * rows]
        w[i] = jnp.where(ok[i], jnp.exp(pieces[i] + lsn[i] + tail + blk[:, :LANES]), 0.0)
        tail = tail + blk[:, LANES:]
    return jnp.concatenate(w, axis=1), tail


def _attn_b_prompt_kernel(q_ref, k_ref, v_ref, o_ref, kb_ref, vb_ref, acc_ref, carry_ref, *, tq):
    qi = pl.program_id(2)

    @pl.when(qi == 0)
    def _():
        kb_ref[...] = k_ref[...].astype(BF16)
        vb_ref[...] = v_ref[...].astype(BF16)

    q = q_ref[...].astype(BF16)
    acc_ref[...] = jnp.zeros_like(acc_ref)
    carry_ref[...] = jnp.zeros_like(carry_ref)
    r = lax.broadcasted_iota(jnp.int32, (tq, BLK), 0)
    c = lax.broadcasted_iota(jnp.int32, (tq, BLK), 1)
    sums = _suffix_and_ones(1)

    def body(it, carry_unused):
        kb = qi - it
        off = pl.multiple_of(kb * tq, BLK)
        z = _dot_nt(q, kb_ref[pl.ds(off, tq), :]) * SCALE
        strict = lambda i: (kb * tq + i * BLK + c) < (qi * tq + r)
        w, tail = _stick_weights(z, strict, carry_ref[...], sums)
        acc_ref[...] += _dot(w.astype(BF16), vb_ref[pl.ds(off, tq), :])
        carry_ref[...] = tail
        return carry_unused

    lax.fori_loop(0, qi + 1, body, 0)
    o_ref[...] = acc_ref[...]


def _attn_b_prompt(p0, n_batch, tp):
    tq = _pick_tq(tp)
    nqb = tp // tq
    return pl.pallas_call(
        functools.partial(_attn_b_prompt_kernel, tq=tq),
        grid=(n_batch, H_B, nqb),
        in_specs=[pl.BlockSpec((tq, HEAD_DIM), lambda b, h, q: (b * nqb + q, 24 + h)),
                  pl.BlockSpec((tp, HEAD_DIM), lambda b, h, q: (b, 32 + h)),
                  pl.BlockSpec((tp, HEAD_DIM), lambda b, h, q: (b, 40 + h))],
        out_specs=pl.BlockSpec((tq, HEAD_DIM), lambda b, h, q: (b * nqb + q, h)),
        out_shape=jax.ShapeDtypeStruct((n_batch * tp, H_B * HEAD_DIM), F32),
        scratch_shapes=[pltpu.VMEM((tp, HEAD_DIM), BF16), pltpu.VMEM((tp, HEAD_DIM), BF16),
                        pltpu.VMEM((tq, HEAD_DIM), F32), pltpu.VMEM((tq, LANES), F32)],
        compiler_params=_params(("parallel", "parallel", "arbitrary")),
        name="stickbreak_prompt",
    )(p0, p0, p0)


def _logf_kernel(f_ref, bf_ref, logf_ref, cum_ref, carry_ref, *, blocks_per_seq):
    i = pl.program_id(0)

    @pl.when(i % blocks_per_seq == 0)
    def _():
        carry_ref[...] = jnp.zeros_like(carry_ref)

    x = f_ref[...] + bf_ref[...]
    logf = _neg_softplus(-x)
    logf_ref[...] = logf
    r = lax.broadcasted_iota(jnp.int32, (BLK, BLK), 0)
    c = lax.broadcasted_iota(jnp.int32, (BLK, BLK), 1)
    cum = carry_ref[...] + _dot3((c <= r).astype(BF16), logf)
    cum_ref[...] = cum
    carry_ref[...] = cum[BLK - 1:BLK, :]


def _logf(p1, bf_row, n_rows, blocks_per_seq):
    return pl.pallas_call(
        functools.partial(_logf_kernel, blocks_per_seq=blocks_per_seq),
        grid=(_cdiv(n_rows, BLK),),
        in_specs=[pl.BlockSpec((BLK, LANES), lambda i: (i, 48)),
                  pl.BlockSpec((1, LANES), lambda i: (0, 0))],
        out_specs=[pl.BlockSpec((BLK, LANES), lambda i: (i, 0)),
                   pl.BlockSpec((BLK, LANES), lambda i: (i, 0))],
        out_shape=[jax.ShapeDtypeStruct((n_rows, LANES), F32),
                   jax.ShapeDtypeStruct((n_rows, LANES), F32)],
        scratch_shapes=[pltpu.VMEM((1, LANES), F32)],
        compiler_params=_params(("arbitrary",)),
        name="log_forget_cumsum",
    )(p1, bf_row)


def _attn_c_prompt_kernel(q_ref, k_ref, v_ref, cq_ref, ck_ref, o_ref, kb_ref, vb_ref, s_ref, m_ref, l_ref,
                          acc_ref, *, tq):
    qi = pl.program_id(2)

    @pl.when(qi == 0)
    def _():
        kb_ref[...] = k_ref[...].astype(BF16)
        vb_ref[...] = v_ref[...].astype(BF16)

    q = q_ref[...].astype(BF16)
    cq = cq_ref[0, 0]
    m_ref[...] = jnp.full(m_ref.shape, NEG, F32)

    def logits(kb):
        off = pl.multiple_of(kb * tq, LANES)
        ck = ck_ref[0, 0, :, pl.ds(off, tq)]
        return off, _dot_nt(q, kb_ref[pl.ds(off, tq), :]) * SCALE + (cq - ck)

    def keep(off, s):
        s_ref[:, pl.ds(off, tq)] = s
        m = m_ref[...]
        for piece in _lane_chunks(s):
            m = jnp.maximum(m, piece)
        m_ref[...] = m

    def pass1(kb, carry_unused):
        keep(*logits(kb))
        return carry_unused

    lax.fori_loop(0, qi, pass1, 0)
    off, s = logits(qi)
    r = lax.broadcasted_iota(jnp.int32, (tq, tq), 0)
    c = lax.broadcasted_iota(jnp.int32, (tq, tq), 1)
    keep(off, jnp.where(c <= r, s, NEG))

    m = jnp.max(m_ref[...], axis=1, keepdims=True)
    l_ref[...] = jnp.zeros_like(l_ref)
    acc_ref[...] = jnp.zeros_like(acc_ref)

    def pass2(kb, carry_unused):
        off = pl.multiple_of(kb * tq, LANES)
        p = jnp.exp(s_ref[:, pl.ds(off, tq)] - m)
        l = l_ref[...]
        for piece in _lane_chunks(p):
            l = l + piece
        l_ref[...] = l
        acc_ref[...] += _dot(p.astype(BF16), vb_ref[pl.ds(off, tq), :])
        return carry_unused

    lax.fori_loop(0, qi + 1, pass2, 0)
    o_ref[...] = acc_ref[...] / jnp.sum(l_ref[...], axis=1, keepdims=True)


def _attn_c_prompt(p1, cum_col, cum_row, n_batch, tp):
    tq = _pick_tq(tp)
    nqb = tp // tq
    return pl.pallas_call(
        functools.partial(_attn_c_prompt_kernel, tq=tq),
        grid=(n_batch, H_C, nqb),
        in_specs=[pl.BlockSpec((tq, HEAD_DIM), lambda b, h, q: (b * nqb + q, h)),
                  pl.BlockSpec((tp, HEAD_DIM), lambda b, h, q: (b, 16 + h)),
                  pl.BlockSpec((tp, HEAD_DIM), lambda b, h, q: (b, 32 + h)),
                  pl.BlockSpec((1, 1, tq, 1), lambda b, h, q: (b, h, q, 0)),
                  pl.BlockSpec((1, 1, 1, tp), lambda b, h, q: (b, h, 0, 0))],
        out_specs=pl.BlockSpec((tq, HEAD_DIM), lambda b, h, q: (b * nqb + q, h)),
        out_shape=jax.ShapeDtypeStruct((n_batch * tp, H_C * HEAD_DIM), F32),
        scratch_shapes=[pltpu.VMEM((tp, HEAD_DIM), BF16), pltpu.VMEM((tp, HEAD_DIM), BF16),
                        pltpu.VMEM((tq, tp), F32), pltpu.VMEM((tq, LANES), F32),
                        pltpu.VMEM((tq, LANES), F32), pltpu.VMEM((tq, HEAD_DIM), F32)],
        compiler_params=_params(("parallel", "parallel", "arbitrary")),
        name="forget_attend_prompt",
    )(p1, p1, p1, cum_col, cum_row)


def _pages_per_step(n_pages, want):
    while n_pages % want:
        want //= 2
    return want


def _idx_sample_kernel(pt_ref, iq_ref, iw_ref, ikn_ref, *rest, topk, n_pages, pps, ds):
    page_refs = rest[:pps]
    bias_ref, score_ref, key_ref, thr_ref, need_ref, jb_ref = rest[pps:]
    j = pl.program_id(1)
    past = n_pages * BLK
    ncol = past + BLK
    nq = iq_ref.shape[1] // H_IDX
    q = iq_ref[0].astype(BF16)
    wgt = iw_ref[0] * (H_IDX ** -0.5)

    def scores(keys):
        dots = _dot_nt(q, keys.astype(BF16)) * (IDX_DIM ** -0.5)
        wd = jnp.maximum(dots, 0.0) * wgt
        return jnp.sum(wd.reshape(nq, H_IDX, BLK), axis=1)

    for i, page_ref in enumerate(page_refs):
        score_ref[:, pl.ds(pl.multiple_of((j * pps + i) * BLK, BLK), BLK)] = scores(page_ref[0])

    @pl.when(j == n_pages // pps - 1)
    def _():
        score_ref[:, past:] = scores(ikn_ref[0])
        t_idx = lax.broadcasted_iota(jnp.int32, (nq, 1), 0) % ds
        s_pos = lax.broadcasted_iota(jnp.int32, (1, ncol), 1)
        vis = s_pos <= past + t_idx
        key_ref[...] = jnp.where(vis, _sortable_key(score_ref[...]), KEY_NEG_INF)
        sel = _topk_select(key_ref, vis, s_pos, topk, ncol, thr_ref, need_ref, jb_ref)
        bias_ref[0] = jnp.where(sel, 0.0, NEG)


def _idx_sample(page_table, iq_th, iw_th, ik_new, cache_idx, topk, ds):
    db, n_pages = page_table.shape
    ncol = n_pages * BLK + BLK
    rows = iq_th.shape[1]
    nq = rows // H_IDX
    pps = _pages_per_step(n_pages, 8)

    def page_spec(i):
        return pl.BlockSpec((1, BLK, IDX_DIM), lambda b, j, pt: (pt[b, j * pps + i], 0, 0))

    return pl.pallas_call(
        functools.partial(_idx_sample_kernel, topk=topk, n_pages=n_pages, pps=pps, ds=ds),
        grid_spec=pltpu.PrefetchScalarGridSpec(
            num_scalar_prefetch=1,
            grid=(db, n_pages // pps),
            in_specs=[pl.BlockSpec((1, rows, IDX_DIM), lambda b, j, pt: (b, 0, 0)),
                      pl.BlockSpec((1, rows, 1), lambda b, j, pt: (b, 0, 0)),
                      pl.BlockSpec((1, BLK, IDX_DIM), lambda b, j, pt: (b, 0, 0))]
                     + [page_spec(i) for i in range(pps)],
            out_specs=pl.BlockSpec((1, nq, ncol), lambda b, j, pt: (b, 0, 0)),
            scratch_shapes=[pltpu.VMEM((nq, ncol), F32),
                            pltpu.VMEM((nq, ncol), jnp.int32),
                            pltpu.VMEM((nq, 1), jnp.int32),
                            pltpu.VMEM((nq, 1), F32),
                            pltpu.VMEM((nq, 1), jnp.int32)]),
        out_shape=jax.ShapeDtypeStruct((db, nq, ncol), F32),
        compiler_params=_params(("parallel", "arbitrary")),
        name="dsa_index_sample",
    )(page_table, iq_th, iw_th, ik_new, *([cache_idx] * pps))


def _softmax_update(s, pv_fn, m_ref, l_ref, acc_ref):
    m_old = m_ref[...]
    m_new = jnp.maximum(m_old, jnp.max(s, axis=1, keepdims=True))
    corr = jnp.exp(m_old - m_new)
    p = jnp.exp(s - m_new)
    l_ref[...] = l_ref[...] * corr + jnp.sum(p, axis=1, keepdims=True)
    acc_ref[...] = acc_ref[...] * corr + pv_fn(p)
    m_ref[...] = m_new


def _head_match(n_rows, heads_per_lane_group, rows_per_head):
    row = lax.broadcasted_iota(jnp.int32, (n_rows, LANES), 0)
    lane = lax.broadcasted_iota(jnp.int32, (n_rows, LANES), 1)
    return (lane % heads_per_lane_group) == (row // rows_per_head), lane, row


def _attn_a_sample_kernel(pt_ref, q_ref, bias_ref, new_ref, *rest, n_steps, pps, ds):
    page_refs = rest[:pps]
    o_ref, rep_ref, m_ref, l_ref, acc_ref = rest[pps:]
    b = pl.program_id(0)
    j = pl.program_id(1)
    n_slots = 2 * HKV_A
    n_rows = H_A * ds
    n_tile = n_rows // bias_ref.shape[1]

    @pl.when((b == 0) & (j == 0))
    def _():
        s_i = lax.broadcasted_iota(jnp.int32, rep_ref.shape, 0)
        c_i = lax.broadcasted_iota(jnp.int32, rep_ref.shape, 1)
        rep_ref[...] = (c_i // n_slots == s_i).astype(BF16)

    @pl.when(j == 0)
    def _():
        m_ref[...] = jnp.full(m_ref.shape, NEG, F32)
        l_ref[...] = jnp.zeros_like(l_ref)
        acc_ref[...] = jnp.zeros_like(acc_ref)

    q = q_ref[0].astype(BF16)
    match, _, _ = _head_match(n_rows, n_slots, ds * (H_A // HKV_A))

    def attend(blocks):
        pgs = [rows.astype(BF16) for rows, _ in blocks]
        pieces = []
        for pg, (_, picked) in zip(pgs, blocks):
            cols = pg.shape[0]
            pick = jnp.concatenate([picked.astype(BF16)] * n_tile, axis=0)
            pick = _dot(pick, rep_ref[:, :cols])
            pieces += [jnp.where(match & (pk > 0.5), sc, NEG)
                       for sc, pk in zip(_lane_chunks(_dot_nt(q, pg) * SCALE), _lane_chunks(pick))]

        def pv(p):
            out, at = 0.0, 0
            for pg in pgs:
                part = p[:, at:at + pg.shape[0]]
                moved = jnp.concatenate([pltpu.roll(x, HKV_A, 1) for x in _lane_chunks(part)], axis=1)
                out = out + _dot(moved.astype(BF16), pg)
                at += pg.shape[0]
            return out

        _softmax_update(jnp.concatenate(pieces, axis=1), pv, m_ref, l_ref, acc_ref)

    def picked(i):
        return jnp.where(bias_ref[0][:, i * BLK:(i + 1) * BLK] == 0.0, 1.0, 0.0)

    @pl.when(j < n_steps)
    def _():
        attend([(page_ref[0], picked(i)) for i, page_ref in enumerate(page_refs)])

    @pl.when(j == n_steps)
    def _():
        attend([(new_ref[0], picked(0))])
        o_ref[0] = acc_ref[...] / l_ref[...]


def _attn_a_sample(page_table, q_rows, bias, new_rows, cache_flat, ds):
    db, n_pages = page_table.shape
    n_slots = 2 * HKV_A
    n_rows = H_A * ds
    pps = _pages_per_step(n_pages, 8)
    n_steps = n_pages // pps
    last = n_steps - 1

    def page_spec(i):
        return pl.BlockSpec((1, BLK * n_slots, HEAD_DIM),
                            lambda b, j, pt: (pt[b, jnp.minimum(j, last) * pps + i], 0, 0))

    return pl.pallas_call(
        functools.partial(_attn_a_sample_kernel, n_steps=n_steps, pps=pps, ds=ds),
        grid_spec=pltpu.PrefetchScalarGridSpec(
            num_scalar_prefetch=1,
            grid=(db, n_steps + 1),
            in_specs=[pl.BlockSpec((1, n_rows, HEAD_DIM), lambda b, j, pt: (b, 0, 0)),
                      pl.BlockSpec((1, bias.shape[1], BLK * pps), lambda b, j, pt: (b, 0, j)),
                      pl.BlockSpec((1, LANES, HEAD_DIM), lambda b, j, pt: (b, 0, 0))]
                     + [page_spec(i) for i in range(pps)],
            out_specs=pl.BlockSpec((1, n_rows, HEAD_DIM), lambda b, j, pt: (b, 0, 0)),
            scratch_shapes=[pltpu.VMEM((BLK, BLK * n_slots), BF16),
                            pltpu.VMEM((n_rows, 1), F32), pltpu.VMEM((n_rows, 1), F32),
                            pltpu.VMEM((n_rows, HEAD_DIM), F32)]),
        out_shape=jax.ShapeDtypeStruct((db, n_rows, HEAD_DIM), F32),
        compiler_params=_params(("arbitrary", "arbitrary")),
        name="dsa_attend_sample",
    )(page_table, q_rows, bias, new_rows, *([cache_flat] * pps))


def _attn_b_sample_kernel(pt_ref, q_ref, knew_ref, vnew_ref, *rest, n_steps, pps, ds):
    page_refs = rest[:pps]
    o_ref, acc_ref, carry_ref = rest[pps:]
    j = pl.program_id(1)
    n_rows = H_B * ds
    q = q_ref[0].astype(BF16)
    match, lane, row = _head_match(n_rows, H_B, ds)
    sums = _suffix_and_ones(H_B)

    def attend(blocks, strict_fn):
        z = jnp.concatenate([_dot_nt(q, k.astype(BF16)) * SCALE for k, _ in blocks], axis=1)
        w, tail = _stick_weights(z, strict_fn, carry_ref[...], sums)
        carry_ref[...] = tail
        out, at = 0.0, 0
        for k, v in blocks:
            out = out + _dot(w[:, at:at + k.shape[0]].astype(BF16), v.astype(BF16))
            at += k.shape[0]
        acc_ref[...] += out

    @pl.when(j == 0)
    def _():
        acc_ref[...] = jnp.zeros_like(acc_ref)
        carry_ref[...] = jnp.zeros_like(carry_ref)
        s_new = lane // H_B
        attend([(knew_ref[0], vnew_ref[0])], lambda i: match & (s_new < row % ds))

    @pl.when(j > 0)
    def _():
        attend([(page_ref[0, :, 0].reshape(BLK * H_B, HEAD_DIM), page_ref[0, :, 1].reshape(BLK * H_B, HEAD_DIM))
                for page_ref in reversed(page_refs)], lambda i: match)

    @pl.when(j == n_steps)
    def _():
        o_ref[0] = acc_ref[...]


def _attn_b_sample(page_table, q_rows, k_new, v_new, cache, ds):
    db, n_pages = page_table.shape
    n_rows = H_B * ds
    pps = _pages_per_step(n_pages, 4)
    n_steps = n_pages // pps

    def page_spec(i):
        return pl.BlockSpec((1, BLK, 2, H_B, HEAD_DIM),
                            lambda b, j, pt: (pt[b, n_pages - 1 - (jnp.maximum(j, 1) - 1) * pps - i], 0, 0, 0, 0))

    new_spec = pl.BlockSpec((1, LANES, HEAD_DIM), lambda b, j, pt: (b, 0, 0))
    return pl.pallas_call(
        functools.partial(_attn_b_sample_kernel, n_steps=n_steps, pps=pps, ds=ds),
        grid_spec=pltpu.PrefetchScalarGridSpec(
            num_scalar_prefetch=1,
            grid=(db, n_steps + 1),
            in_specs=[pl.BlockSpec((1, n_rows, HEAD_DIM), lambda b, j, pt: (b, 0, 0)), new_spec, new_spec]
                     + [page_spec(i) for i in range(pps)],
            out_specs=pl.BlockSpec((1, n_rows, HEAD_DIM), lambda b, j, pt: (b, 0, 0)),
            scratch_shapes=[pltpu.VMEM((n_rows, HEAD_DIM), F32), pltpu.VMEM((n_rows, LANES), F32)]),
        out_shape=jax.ShapeDtypeStruct((db, n_rows, HEAD_DIM), F32),
        compiler_params=_params(("parallel", "arbitrary")),
        name="stickbreak_sample",
    )(page_table, q_rows, k_new, v_new, *([cache] * pps))


def _page_suffix_kernel(lf_ref, sfx_ref, tot_ref):
    x = lf_ref[...]
    r = lax.broadcasted_iota(jnp.int32, (BLK, BLK), 0)
    c = lax.broadcasted_iota(jnp.int32, (BLK, BLK), 1)
    sfx_ref[...] = _dot3((c > r).astype(BF16), x)
    tot_ref[...] = _dot3(jnp.ones((8, BLK), BF16), x)


def _page_suffix(lf_t, tn):
    n = lf_t.shape[1]
    return pl.pallas_call(
        _page_suffix_kernel,
        grid=(n // tn,),
        in_specs=[pl.BlockSpec((BLK, tn), lambda i: (0, i))],
        out_specs=[pl.BlockSpec((BLK, tn), lambda i: (0, i)), pl.BlockSpec((8, tn), lambda i: (0, i))],
        out_shape=[jax.ShapeDtypeStruct((BLK, n), F32), jax.ShapeDtypeStruct((8, n), F32)],
        compiler_params=_params(("parallel",)),
        name="log_forget_page_suffix",
    )(lf_t)


def _attn_c_sample_kernel(pt_ref, q_ref, lfnew_ref, knew_ref, vnew_ref, *rest, n_steps, pps, ds):
    sfx_refs, tot_refs, page_refs = rest[:pps], rest[pps:2 * pps], rest[2 * pps:3 * pps]
    o_ref, m_ref, l_ref, acc_ref, carry_ref, cn_ref = rest[3 * pps:]
    j = pl.program_id(1)
    n_rows = H_C * ds
    q = q_ref[0].astype(BF16)
    match, lane, row = _head_match(n_rows, H_C, ds)

    @pl.when(j == 0)
    def _():
        m_ref[...] = jnp.full(m_ref.shape, NEG, F32)
        l_ref[...] = jnp.zeros_like(l_ref)
        acc_ref[...] = jnp.zeros_like(acc_ref)
        carry_ref[...] = jnp.zeros_like(carry_ref)
        lf = lfnew_ref[0]
        s_new = lane // H_C
        cn_ref[...] = jnp.sum(jnp.where(match & (s_new <= row % ds), lf, 0.0), axis=1, keepdims=True)
        r2 = lax.broadcasted_iota(jnp.int32, (LANES, LANES), 0)
        c2 = lax.broadcasted_iota(jnp.int32, (LANES, LANES), 1)
        upto = ((r2 % H_C == c2 % H_C) & (r2 // H_C <= c2 // H_C)).astype(BF16)
        cum_keys = _dot3_left(jnp.broadcast_to(lf, (8, LANES)), upto)[0:1]
        k_rows = knew_ref[0].astype(BF16)
        s = _dot_nt(q, k_rows) * SCALE + (cn_ref[...] - cum_keys)
        s = jnp.where(match & (s_new <= row % ds), s, NEG)
        _softmax_update(s, lambda p: _dot(p.astype(BF16), vnew_ref[0].astype(BF16)), m_ref, l_ref, acc_ref)

    @pl.when(j > 0)
    def _():
        later = carry_ref[...]
        pieces, values = [], []
        for sfx_ref, tot_ref, page_ref in zip(sfx_refs, tot_refs, page_refs):
            k_rows = page_ref[0, :, 0].reshape(BLK * H_C, HEAD_DIM).astype(BF16)
            values.append(page_ref[0, :, 1].reshape(BLK * H_C, HEAD_DIM).astype(BF16))
            s = _dot_nt(q, k_rows) * SCALE + (sfx_ref[0] + (cn_ref[...] + later))
            pieces += [jnp.where(match, piece, NEG) for piece in _lane_chunks(s)]
            later = later + tot_ref[0]
        carry_ref[...] = later
        cols = BLK * H_C

        def pv(p):
            out = 0.0
            for i, v_rows in enumerate(values):
                out = out + _dot(p[:, i * cols:(i + 1) * cols].astype(BF16), v_rows)
            return out

        _softmax_update(jnp.concatenate(pieces, axis=1), pv, m_ref, l_ref, acc_ref)

    @pl.when(j == n_steps)
    def _():
        o_ref[0] = acc_ref[...] / l_ref[...]


def _attn_c_sample(page_table, q_rows, lf_new, k_new, v_new, sfx_flat, tot_col, cache, ds):
    db, n_pages = page_table.shape
    n_rows = H_C * ds
    pps = _pages_per_step(n_pages, 4)
    n_steps = n_pages // pps

    def page_of(i):
        return lambda b, j, pt: pt[b, n_pages - 1 - (jnp.maximum(j, 1) - 1) * pps - i]

    def specs(shape):
        zeros = (0,) * (len(shape) - 1)
        return [pl.BlockSpec(shape, (lambda f: lambda b, j, pt: (f(b, j, pt),) + zeros)(page_of(i)))
                for i in range(pps)]

    new_spec = pl.BlockSpec((1, LANES, HEAD_DIM), lambda b, j, pt: (b, 0, 0))
    return pl.pallas_call(
        functools.partial(_attn_c_sample_kernel, n_steps=n_steps, pps=pps, ds=ds),
        grid_spec=pltpu.PrefetchScalarGridSpec(
            num_scalar_prefetch=1,
            grid=(db, n_steps + 1),
            in_specs=[pl.BlockSpec((1, n_rows, HEAD_DIM), lambda b, j, pt: (b, 0, 0)),
                      pl.BlockSpec((1, 1, LANES), lambda b, j, pt: (b, 0, 0)),
                      new_spec, new_spec]
                     + specs((1, 1, BLK * H_C)) + specs((1, n_rows, 1)) + specs((1, BLK, 2, H_C, HEAD_DIM)),
            out_specs=pl.BlockSpec((1, n_rows, HEAD_DIM), lambda b, j, pt: (b, 0, 0)),
            scratch_shapes=[pltpu.VMEM((n_rows, 1), F32), pltpu.VMEM((n_rows, 1), F32),
                            pltpu.VMEM((n_rows, HEAD_DIM), F32), pltpu.VMEM((n_rows, 1), F32),
                            pltpu.VMEM((n_rows, 1), F32)]),
        out_shape=jax.ShapeDtypeStruct((db, n_rows, HEAD_DIM), F32),
        compiler_params=_params(("parallel", "arbitrary")),
        name="forget_attend_sample",
    )(page_table, q_rows, lf_new, k_new, v_new, *([sfx_flat] * pps), *([tot_col] * pps), *([cache] * pps))


def _layer_norm(xf, g, b):
    mu = jnp.mean(xf, axis=1, keepdims=True)
    d = xf - mu
    var = jnp.mean(d * d, axis=1, keepdims=True)
    return d * lax.rsqrt(var + LN_EPS) * g + b


def _route(y, wr_ref, br_ref):
    y_hi, y_lo = _split2(y)
    w = wr_ref[...]
    p1 = _dot(y_hi, w)
    p2 = _dot(y_lo, w)
    logits = p1 + pltpu.roll(p1, LANES - N_EXPERTS, 1) + p2
    lane = lax.broadcasted_iota(jnp.int32, logits.shape, 1)
    in_grp = lane < N_GROUPS
    s = [jax.nn.sigmoid(logits if j == 0 else pltpu.roll(logits, LANES - j * N_GROUPS, 1))
         for j in range(EPG)]
    sel = [jnp.where(in_grp, s[j] + br_ref[j:j + 1, :], NEG) for j in range(EPG)]
    top2 = None
    for a in range(EPG):
        for b in range(a + 1, EPG):
            pair = sel[a] + sel[b]
            top2 = pair if top2 is None else jnp.maximum(top2, pair)
    top2 = jnp.where(in_grp, top2, -jnp.inf)
    best_val = jnp.max(top2, axis=1, keepdims=True)
    g_best = jnp.min(jnp.where(top2 == best_val, lane, LANES), axis=1, keepdims=True)
    mine = lane == g_best
    picked = []
    for j in range(EPG):
        rank = jnp.zeros(logits.shape, F32)
        for i in range(EPG):
            if i == j:
                continue
            ahead = (sel[i] >= sel[j]) if i < j else (sel[i] > sel[j])
            rank = rank + ahead.astype(F32)
        picked.append(jnp.sum(jnp.where(mine & (rank < 2.0), s[j], 0.0), axis=1, keepdims=True))
    denom = picked[0] + picked[1] + picked[2] + picked[3]
    extra = jnp.where(lane == EPG, g_best.astype(F32), 0.0)
    for j in range(EPG):
        extra = jnp.where(lane == j, picked[j] / denom, extra)
    return extra


def _outproj_kernel(o1_ref, o2_ref, w1_ref, w2_ref, x_ref, g_ref, b_ref, wr_ref, br_ref, out_ref, *, d_model):
    mix = _dot(o1_ref[...].astype(BF16), w1_ref[...]) + _dot(o2_ref[...].astype(BF16), w2_ref[...])
    y = _layer_norm(ALPHA * x_ref[...] + mix, g_ref[...], b_ref[...])
    out_ref[:, :d_model] = y
    out_ref[:, d_model:] = _route(y, wr_ref, br_ref)


def _outproj(o1, o2, o2_block, w_bf16, x, g, b, wr, br, d_model, tm):
    nf = x.shape[0]
    half = w_bf16.shape[0] // 2
    return pl.pallas_call(
        functools.partial(_outproj_kernel, d_model=d_model),
        grid=(_cdiv(nf, tm),),
        in_specs=[pl.BlockSpec((tm, half), lambda i: (i, 0)),
                  pl.BlockSpec((tm, half), lambda i: (i, o2_block)),
                  pl.BlockSpec((half, d_model), lambda i: (0, 0)),
                  pl.BlockSpec((half, d_model), lambda i: (1, 0)),
                  pl.BlockSpec((tm, d_model), lambda i: (i, 0)),
                  pl.BlockSpec((1, d_model), lambda i: (0, 0)),
                  pl.BlockSpec((1, d_model), lambda i: (0, 0)),
                  pl.BlockSpec((d_model, LANES), lambda i: (0, 0)),
                  pl.BlockSpec((EPG, LANES), lambda i: (0, 0))],
        out_specs=pl.BlockSpec((tm, d_model + XCOLS), lambda i: (i, 0)),
        out_shape=jax.ShapeDtypeStruct((nf, d_model + XCOLS), F32),
        compiler_params=_params(("parallel",)),
        name="outproj_norm_route",
    )(o1, o2, w_bf16, w_bf16, x, g, b, wr, br)


def _gather_rows(idx_ref, base, src_ref, dst_ref, sem, n_rows):
    def issue(r, c):
        pltpu.make_async_copy(src_ref.at[pl.ds(idx_ref[base + r], 1)], dst_ref.at[pl.ds(r, 1)], sem).start()
        return c

    lax.fori_loop(0, n_rows, issue, 0)

    def drain(r, c):
        pltpu.make_async_copy(src_ref.at[pl.ds(0, 1)], dst_ref.at[pl.ds(r, 1)], sem).wait()
        return c

    lax.fori_loop(0, n_rows, drain, 0)


def _moe_kernel(grp_ref, valid_ref, src_ref, xa_ref, wg_ref, wu_ref, wd_ref, g_ref, b_ref, o_ref,
                xs_ref, xb_ref, acc_ref, sem, *, d_model, tm):
    i = pl.program_id(0)
    e = pl.program_id(1)
    valid = valid_ref[i] == 1

    @pl.when(valid & (e == 0))
    def _():
        _gather_rows(src_ref, i * tm, xa_ref, xs_ref, sem, tm)
        xb_ref[...] = xs_ref[:, :d_model].astype(BF16)
        acc_ref[...] = jnp.zeros_like(acc_ref)

    @pl.when(valid)
    def _():
        xb = xb_ref[...]
        a = _dot(xb, wg_ref[0])
        h = a * jax.nn.sigmoid(a) * _dot(xb, wu_ref[0])
        extra = xs_ref[:, d_model:]
        lane = lax.broadcasted_iota(jnp.int32, extra.shape, 1)
        gate = jnp.sum(jnp.where(lane == e, extra, 0.0), axis=1, keepdims=True)
        acc_ref[...] += _dot((h * gate).astype(BF16), wd_ref[0])

    @pl.when(valid & (e == EPG - 1))
    def _():
        o_ref[...] = _layer_norm(ALPHA * xs_ref[:, :d_model] + acc_ref[...], g_ref[...], b_ref[...])

    @pl.when(jnp.logical_not(valid) & (e == EPG - 1))
    def _():
        o_ref[...] = jnp.zeros_like(o_ref)


def _moe(tile_grp, tile_valid, src, xa, wg, wu, wd, g, b, d_model, tm):
    n_tiles = tile_grp.shape[0]
    d_exp = wg.shape[2]

    def w_idx(i, e, grp, valid, src):
        return (grp[i] * EPG + jnp.where(valid[i] == 1, e, EPG - 1), 0, 0)

    return pl.pallas_call(
        functools.partial(_moe_kernel, d_model=d_model, tm=tm),
        grid_spec=pltpu.PrefetchScalarGridSpec(
            num_scalar_prefetch=3,
            grid=(n_tiles, EPG),
            in_specs=[pl.BlockSpec(memory_space=pl.ANY),
                      pl.BlockSpec((1, d_model, d_exp), w_idx),
                      pl.BlockSpec((1, d_model, d_exp), w_idx),
                      pl.BlockSpec((1, d_exp, d_model), w_idx),
                      pl.BlockSpec((1, d_model), lambda i, e, *_: (0, 0)),
                      pl.BlockSpec((1, d_model), lambda i, e, *_: (0, 0))],
            out_specs=pl.BlockSpec((tm, d_model), lambda i, e, *_: (i, 0)),
            scratch_shapes=[pltpu.VMEM((tm, d_model + XCOLS), F32),
                            pltpu.VMEM((tm, d_model), BF16),
                            pltpu.VMEM((tm, d_model), F32),
                            pltpu.SemaphoreType.DMA(())]),
        out_shape=jax.ShapeDtypeStruct((n_tiles * tm, d_model), F32),
        compiler_params=_params(("arbitrary", "arbitrary")),
        name="grouped_moe",
    )(tile_grp, tile_valid, src, xa, wg, wu, wd, g, b)


def _unpermute_kernel(idx_ref, src_ref, o_ref, sem, *, tg):
    _gather_rows(idx_ref, pl.program_id(0) * tg, src_ref, o_ref, sem, tg)


def _unpermute(dest_padded, ys, n_rows, tg):
    d = ys.shape[1]
    return pl.pallas_call(
        functools.partial(_unpermute_kernel, tg=tg),
        grid_spec=pltpu.PrefetchScalarGridSpec(
            num_scalar_prefetch=1,
            grid=(_cdiv(n_rows, tg),),
            in_specs=[pl.BlockSpec(memory_space=pl.ANY)],
            out_specs=pl.BlockSpec((tg, d), lambda i, idx: (i, 0)),
            scratch_shapes=[pltpu.SemaphoreType.DMA(())]),
        out_shape=jax.ShapeDtypeStruct((n_rows, d), F32),
        compiler_params=_params(("arbitrary",)),
        name="unpermute_rows",
    )(dest_padded, ys)


def _routing_plan(grp, tm):
    nf = grp.shape[0]
    n_tiles = _cdiv(nf + N_GROUPS * (tm - 1), tm)
    onehot = (grp[:, None] == jnp.arange(N_GROUPS, dtype=jnp.int32)[None, :]).astype(jnp.int32)
    counts = jnp.sum(onehot, axis=0)
    rank = jnp.sum((jnp.cumsum(onehot, axis=0) - onehot) * onehot, axis=1)
    padded = ((counts + tm - 1) // tm) * tm
    ends = jnp.cumsum(padded)
    dest = (ends - padded)[grp] + rank
    src = jnp.zeros((n_tiles * tm,), jnp.int32).at[dest].set(jnp.arange(nf, dtype=jnp.int32))
    starts = jnp.arange(n_tiles, dtype=jnp.int32) * tm
    tile_valid = (starts < ends[-1]).astype(jnp.int32)
    tile_grp = jnp.minimum(jnp.searchsorted(ends, starts, side="right"), N_GROUPS - 1).astype(jnp.int32)
    last_grp = tile_grp[jnp.maximum(ends[-1] // tm - 1, 0)]
    tile_grp = jnp.where(tile_valid == 1, tile_grp, last_grp)
    return tile_grp, tile_valid, src, dest


def _row_tile(n):
    return next(t for t in range(1024, 7, -8) if n % t == 0)


def _ffn(xa, wg, wu, wd, g, b, d_model, tm, row_sets):
    grp = xa[:, d_model + EPG].astype(jnp.int32)
    tile_grp, tile_valid, src, dest = _routing_plan(grp, tm)
    ys = _moe(tile_grp, tile_valid, src, xa, wg, wu, wd, g, b, d_model, tm)
    return [_unpermute(dest[rows], ys, rows.shape[0], _row_tile(rows.shape[0])) for rows in row_sets]


def _rope_tables(pos):
    def table(dim):
        half = dim // 2
        inv = ROPE_THETA ** (-jnp.arange(half, dtype=F32) / half)
        ang = pos.astype(F32)[:, None] * inv[None, :]
        cos = jnp.cos(ang)
        sin = jnp.sin(ang)
        reps = LANES // dim
        return jnp.tile(jnp.concatenate([cos, cos], axis=1), (1, reps)), \
            jnp.tile(jnp.concatenate([-sin, sin], axis=1), (1, reps))

    c128, s128 = table(HEAD_DIM)
    c64, s64 = table(IDX_DIM)
    return c128, s128, c64, s64


def _pad_rows(a, rows):
    return jnp.pad(a, [(0, 0), (0, rows - a.shape[1])] + [(0, 0)] * (a.ndim - 2))


def _head_rows(a, db, ds, heads):
    return a.reshape(db, ds, heads, HEAD_DIM).transpose(0, 2, 1, 3).reshape(db, heads * ds, HEAD_DIM)


def _token_rows(a, db, ds, heads):
    return a.reshape(db, heads, ds, HEAD_DIM).transpose(0, 2, 1, 3).reshape(db * ds, heads * HEAD_DIM)


def _new_block(a, db):
    slots = a.shape[2]
    return _pad_rows(a, LANES // slots).reshape(db, LANES, HEAD_DIM)


def kernel(x_prompt, x_sample, cache_l0_a_kv, cache_l0_idx_k, cache_l0_b_kv, cache_l1_c_kv, cache_l1_logf,
           page_table, meta_tokens, w_in_l0, w_out_l0, w_in_l1, b_forget_l1, w_out_l1, ln_mix_g, ln_mix_b,
           ln_ffn_g, ln_ffn_b, w_router, b_router, w_gate, w_up, w_down):
    n_batch, seq, d_model = x_prompt.shape
    db, ds, _ = x_sample.shape
    assert ds <= 8 and d_model % LANES == 0
    t_len = seq + N_META
    tp = _cdiv(t_len, BLK) * BLK
    n_prompt = n_batch * tp
    n_sample = db * ds
    nf = n_prompt + n_sample
    n_pool = cache_l0_a_kv.shape[0]
    n_pages = page_table.shape[1]
    past = n_pages * BLK
    topk_prompt = min(TOPK_MAX, seq // 4)
    topk_sample = min(TOPK_MAX, (past + ds) // 4)
    tm_proj = 1072 if nf % 1072 == 0 else 128
    tm_out = 256
    tm_moe = 512
    all_rows = jnp.arange(nf, dtype=jnp.int32)
    prompt_out_rows = (jnp.arange(n_batch, dtype=jnp.int32)[:, None] * tp + N_META
                       + jnp.arange(seq, dtype=jnp.int32)[None, :]).reshape(-1)

    meta = jnp.broadcast_to(meta_tokens[None], (n_batch, N_META, d_model)).astype(x_prompt.dtype)
    hp = _pad_rows(jnp.concatenate([meta, x_prompt], axis=1), tp)
    x0 = jnp.concatenate([hp.reshape(n_prompt, d_model), x_sample.reshape(n_sample, d_model)], axis=0)
    pos = jnp.concatenate([jnp.tile(jnp.arange(tp), n_batch), jnp.tile(past + jnp.arange(ds), db)])
    tables = _rope_tables(pos)

    cuts = np.cumsum((0, H_A * HEAD_DIM, HKV_A * HEAD_DIM, HKV_A * HEAD_DIM, H_B * HEAD_DIM, H_B * HEAD_DIM,
                      H_B * HEAD_DIM, H_IDX * IDX_DIM, IDX_DIM, H_IDX))
    seg = [w_in_l0[:, cuts[i]:cuts[i + 1]] for i in range(9)]
    qa_w, ka_w, va_w, qb_w, kb_w, vb_w, iq_w, ik_w, iw_w = seg
    zeros = lambda n: jnp.zeros((d_model, n), w_in_l0.dtype)
    w0 = jnp.concatenate([qa_w, iq_w, ka_w, va_w, qb_w, kb_w, vb_w, ik_w, zeros(LANES - IDX_DIM),
                          iw_w, zeros(LANES - H_IDX)], axis=1).astype(BF16)
    modes0 = jnp.asarray([1] * 4 + [2] * 4 + [1] * 2 + [0] * 14 + [3], jnp.int32)
    n_qkv = 3 * H_C * HEAD_DIM
    w1 = jnp.concatenate([w_in_l1[:, :n_qkv], w_in_l1[:, n_qkv:], zeros(MODE_COLS - H_C)], axis=1).astype(BF16)
    bf_row = jnp.zeros((1, LANES), F32).at[0, :H_C].set(b_forget_l1)
    w_out0 = w_out_l0.astype(BF16)
    w_out1 = w_out_l1.astype(BF16)
    wr_perm = w_router.reshape(d_model, N_GROUPS, EPG).transpose(0, 2, 1).reshape(d_model, N_EXPERTS)
    wr_hi = wr_perm.astype(BF16)
    wr_lo = (wr_perm - wr_hi.astype(F32)).astype(BF16)
    wr = jnp.concatenate([wr_hi, wr_lo, jnp.zeros((d_model, LANES - 2 * N_EXPERTS), BF16)], axis=1)
    br = jnp.zeros((EPG, LANES), F32).at[:, :N_GROUPS].set(b_router.reshape(N_GROUPS, EPG).T)
    wg = w_gate.astype(BF16)
    wu = w_up.astype(BF16)
    wd = w_down.astype(BF16)
    row = lambda v: v.reshape(1, d_model)

    p0 = _inproj(x0, w0, d_model, tm_proj, modes0, tables)
    col = lambda blk0, n: slice(blk0 * LANES, (blk0 + n) * LANES)
    ps = p0[n_prompt:]

    bias_p = _idx_prompt(p0, n_batch, tp, topk_prompt)
    oa_p = _attn_a_prompt(p0, bias_p, n_batch, tp)
    ob_p = _attn_b_prompt(p0, n_batch, tp)

    grp_q = H_A // HKV_A
    iq_th = jnp.tile(ps[:, col(8, 8)].reshape(db, ds * H_IDX, IDX_DIM), (1, grp_q, 1))
    iw_th = jnp.tile(ps[:, 49 * LANES:49 * LANES + H_IDX].reshape(db, ds * H_IDX, 1), (1, grp_q, 1))
    ik_s = ps[:, 48 * LANES:48 * LANES + IDX_DIM].reshape(db, ds, IDX_DIM)
    bias_s = _idx_sample(page_table, iq_th, iw_th, _pad_rows(ik_s, BLK), cache_l0_idx_k, topk_sample, ds)

    ka_s = ps[:, col(16, 4)].reshape(db, ds, HKV_A, HEAD_DIM)
    va_s = ps[:, col(20, 4)].reshape(db, ds, HKV_A, HEAD_DIM)
    kv_a_s = jnp.stack([ka_s, va_s], axis=2)
    oa_s = _attn_a_sample(page_table, _head_rows(ps[:, col(0, 8)], db, ds, H_A), bias_s,
                          _new_block(kv_a_s.reshape(db, ds, 2 * HKV_A, HEAD_DIM), db),
                          cache_l0_a_kv.reshape(n_pool, BLK * 2 * HKV_A, HEAD_DIM), ds)
    oa_s = _token_rows(oa_s, db, ds, H_A)

    kb_s = ps[:, col(32, 8)].reshape(db, ds, H_B, HEAD_DIM)
    vb_s = ps[:, col(40, 8)].reshape(db, ds, H_B, HEAD_DIM)
    kv_b_s = jnp.stack([kb_s, vb_s], axis=2)
    ob_s = _attn_b_sample(page_table, _head_rows(ps[:, col(24, 8)], db, ds, H_B),
                          _new_block(kb_s, db), _new_block(vb_s, db), cache_l0_b_kv, ds)
    ob_s = _token_rows(ob_s, db, ds, H_B)

    oa = jnp.concatenate([oa_p, oa_s], axis=0)
    ob = jnp.concatenate([ob_p, ob_s], axis=0)
    xa1 = _outproj(oa, ob, 0, w_out0, x0, row(ln_mix_g[0]), row(ln_mix_b[0]), wr, br, d_model, tm_out)
    x1, = _ffn(xa1, wg[0], wu[0], wd[0], row(ln_ffn_g[0]), row(ln_ffn_b[0]), d_model, tm_moe, [all_rows])

    p1 = _inproj(x1, w1, d_model, tm_proj)
    logf, cum = _logf(p1, bf_row, nf, tp // BLK)
    cum_t = cum[:n_prompt, :H_C].reshape(n_batch, tp, H_C).transpose(0, 2, 1)
    oc_p = _attn_c_prompt(p1, cum_t[..., None], cum_t[:, :, None, :], n_batch, tp)

    ps1 = p1[n_prompt:]
    kc_s = ps1[:, col(16, 16)].reshape(db, ds, H_C, HEAD_DIM)
    vc_s = ps1[:, col(32, 16)].reshape(db, ds, H_C, HEAD_DIM)
    kv_c_s = jnp.stack([kc_s, vc_s], axis=2)
    logf_s = logf[n_prompt:, :H_C].reshape(db, ds, H_C)
    lf_new = _pad_rows(logf_s, LANES // H_C).reshape(db, 1, LANES)
    n_lf = n_pool * H_C
    tn_lf = min(2048, _cdiv(n_lf, LANES) * LANES)
    n_lf_pad = _cdiv(n_lf, tn_lf) * tn_lf
    lf_t = jnp.pad(cache_l1_logf.astype(F32).transpose(1, 0, 2).reshape(BLK, n_lf), [(0, 0), (0, n_lf_pad - n_lf)])
    sfx_t, tot = _page_suffix(lf_t, tn_lf)
    sfx_flat = sfx_t[:, :n_lf].reshape(BLK, n_pool, H_C).transpose(1, 0, 2).reshape(n_pool, 1, BLK * H_C)
    tot_col = jnp.repeat(tot[0, :n_lf].reshape(n_pool, H_C), ds, axis=1)[..., None]
    oc_s = _attn_c_sample(page_table, _head_rows(ps1[:, col(0, 16)], db, ds, H_C), lf_new,
                          _new_block(kc_s, db), _new_block(vc_s, db), sfx_flat, tot_col, cache_l1_c_kv, ds)
    oc_s = _token_rows(oc_s, db, ds, H_C)

    oc = jnp.concatenate([oc_p, oc_s], axis=0)
    xa2 = _outproj(oc, oc, 1, w_out1, x1, row(ln_mix_g[1]), row(ln_mix_b[1]), wr, br, d_model, tm_out)
    y_prompt, y_sample = _ffn(xa2, wg[1], wu[1], wd[1], row(ln_ffn_g[1]), row(ln_ffn_b[1]), d_model, tm_moe,
                              [prompt_out_rows, all_rows[n_prompt:]])

    def prompt_rows(a, blk0, heads):
        return a[:n_prompt, col(blk0, heads)].reshape(n_batch, tp, heads, HEAD_DIM)[:, :t_len]

    y_prompt = y_prompt.reshape(n_batch, seq, d_model)
    y_sample = y_sample.reshape(db, ds, d_model)
    a_kv_p = jnp.stack([prompt_rows(p0, 16, HKV_A), prompt_rows(p0, 20, HKV_A)], axis=2)
    idx_k_p = p0[:n_prompt, 48 * LANES:48 * LANES + IDX_DIM].reshape(n_batch, tp, IDX_DIM)[:, :t_len]
    b_kv_p = jnp.stack([prompt_rows(p0, 32, H_B), prompt_rows(p0, 40, H_B)], axis=2)
    c_kv_p = jnp.stack([prompt_rows(p1, 16, H_C), prompt_rows(p1, 32, H_C)], axis=2)
    logf_p = logf[:n_prompt, :H_C].reshape(n_batch, tp, H_C)[:, :t_len]
    return (y_prompt, y_sample, a_kv_p, kv_a_s, idx_k_p, ik_s, b_kv_p, kv_b_s, c_kv_p, kv_c_s, logf_p, logf_s)
```

```python
import functools

import numpy as np
import jax
import jax.numpy as jnp
from jax import lax
from jax.experimental import pallas as pl
from jax.experimental.pallas import tpu as pltpu

HEAD_DIM = 128
H_A = 8
HKV_A = 4
H_B = 8
H_C = 16
H_IDX = 16
IDX_DIM = 64
TOPK_MAX = 256
N_META = 16
BLK = 128
ROPE_THETA = 10000.0
N_EXPERTS = 32
N_GROUPS = 8
EPG = N_EXPERTS // N_GROUPS
LN_EPS = 1e-5
DEPTH = 2
ALPHA = (2 * DEPTH) ** 0.25
NEG = -1e30
INT_MIN = -2 ** 31
KEY_NEG_INF = -2139095041
LANES = 128
TN = 1280
MODE_COLS = 256
XCOLS = 128
VMEM_LIMIT = 56 * 1024 * 1024
SCALE = HEAD_DIM ** -0.5

F32 = jnp.float32
BF16 = jnp.bfloat16
NT_DIMS = (((1,), (1,)), ((), ()))


def _cdiv(a, b):
    return (a + b - 1) // b


def _dot(a, b):
    return jnp.dot(a, b, preferred_element_type=F32)


def _dot_nt(a, b):
    return lax.dot_general(a, b, NT_DIMS, preferred_element_type=F32)


def _split2(x):
    hi = x.astype(BF16)
    lo = (x - hi.astype(F32)).astype(BF16)
    return hi, lo


def _split3(x):
    hi = x.astype(BF16)
    r = x - hi.astype(F32)
    mid = r.astype(BF16)
    lo = (r - mid.astype(F32)).astype(BF16)
    return hi, mid, lo


def _dot3(a_bf16, x):
    hi, mid, lo = _split3(x)
    return _dot(a_bf16, hi) + _dot(a_bf16, mid) + _dot(a_bf16, lo)


def _dot3_left(x, a_bf16):
    hi, mid, lo = _split3(x)
    return _dot(hi, a_bf16) + _dot(mid, a_bf16) + _dot(lo, a_bf16)


def _neg_softplus(z):
    return -(jnp.maximum(z, 0.0) + jnp.log1p(jnp.exp(-jnp.abs(z))))


def _neg_softplus_bulk(z):
    return -(jnp.maximum(z, 0.0) + jnp.log(1.0 + jnp.exp(-jnp.abs(z))))


def _lane_chunks(x):
    return [x[:, c * LANES:(c + 1) * LANES] for c in range(x.shape[1] // LANES)]


def _params(sem, vmem=VMEM_LIMIT):
    return pltpu.CompilerParams(dimension_semantics=sem, vmem_limit_bytes=vmem)


def _rope128(a, c, s):
    return a * c + pltpu.roll(a, 64, 1) * s


def _rope64(a, c, s):
    lane = lax.broadcasted_iota(jnp.int32, a.shape, 1)
    first = (lane % 64) < 32
    partner = jnp.where(first, pltpu.roll(a, 96, 1), pltpu.roll(a, 32, 1))
    return a * c + partner * s


def _inproj_rope_kernel(modes_ref, x_ref, w_ref, c128_ref, s128_ref, c64_ref, s64_ref, o_ref, xb_ref, *, tn):
    j = pl.program_id(1)

    @pl.when(j == 0)
    def _():
        xb_ref[...] = x_ref[...].astype(BF16)

    acc = _dot(xb_ref[...], w_ref[...])
    chunk = lambda c: slice(c * LANES, (c + 1) * LANES)
    per_tile = tn // MODE_COLS
    for t in range(per_tile):
        mode = modes_ref[j * per_tile + t]
        c0 = t * (MODE_COLS // LANES)
        span = slice(t * MODE_COLS, (t + 1) * MODE_COLS)

        @pl.when(mode == 0)
        def _():
            o_ref[:, span] = acc[:, span]

        @pl.when(mode == 1)
        def _():
            for c in range(c0, c0 + MODE_COLS // LANES):
                o_ref[:, chunk(c)] = _rope128(acc[:, chunk(c)], c128_ref[...], s128_ref[...])

        @pl.when(mode == 2)
        def _():
            for c in range(c0, c0 + MODE_COLS // LANES):
                o_ref[:, chunk(c)] = _rope64(acc[:, chunk(c)], c64_ref[...], s64_ref[...])

        @pl.when(mode == 3)
        def _():
            o_ref[:, chunk(c0)] = _rope64(acc[:, chunk(c0)], c64_ref[...], s64_ref[...])
            rest = slice((c0 + 1) * LANES, (t + 1) * MODE_COLS)
            o_ref[:, rest] = acc[:, rest]


def _inproj_plain_kernel(x_ref, w_ref, o_ref, xb_ref):
    @pl.when(pl.program_id(1) == 0)
    def _():
        xb_ref[...] = x_ref[...].astype(BF16)

    o_ref[...] = _dot(xb_ref[...], w_ref[...])


def _inproj(x, w_bf16, d_model, tm, modes=None, tables=None):
    nf = x.shape[0]
    ncols = w_bf16.shape[1]
    tn = TN if ncols % TN == 0 else MODE_COLS
    grid = (_cdiv(nf, tm), ncols // tn)
    scratch = [pltpu.VMEM((tm, d_model), BF16)]
    out_shape = jax.ShapeDtypeStruct((nf, ncols), F32)
    if modes is None:
        return pl.pallas_call(
            _inproj_plain_kernel,
            grid=grid,
            in_specs=[pl.BlockSpec((tm, d_model), lambda i, j: (i, 0)),
                      pl.BlockSpec((d_model, tn), lambda i, j: (0, j))],
            out_specs=pl.BlockSpec((tm, tn), lambda i, j: (i, j)),
            out_shape=out_shape,
            scratch_shapes=scratch,
            compiler_params=_params(("parallel", "arbitrary")),
            name="inproj_plain",
        )(x, w_bf16)
    tab_spec = pl.BlockSpec((tm, LANES), lambda i, j, m: (i, 0))
    return pl.pallas_call(
        functools.partial(_inproj_rope_kernel, tn=tn),
        grid_spec=pltpu.PrefetchScalarGridSpec(
            num_scalar_prefetch=1,
            grid=grid,
            in_specs=[pl.BlockSpec((tm, d_model), lambda i, j, m: (i, 0)),
                      pl.BlockSpec((d_model, tn), lambda i, j, m: (0, j)),
                      tab_spec, tab_spec, tab_spec, tab_spec],
            out_specs=pl.BlockSpec((tm, tn), lambda i, j, m: (i, j)),
            scratch_shapes=scratch),
        out_shape=out_shape,
        compiler_params=_params(("parallel", "arbitrary")),
        name="inproj_rope",
    )(modes, x, w_bf16, *tables)


def _sortable_key(score):
    score = jnp.where(score == 0.0, 0.0, score)
    bits = lax.bitcast_convert_type(score, jnp.int32)
    return bits ^ ((bits >> 31) & 0x7FFFFFFF)


def _count(mask):
    return jnp.sum(mask.astype(F32), axis=1, keepdims=True)


def _topk_select(key_ref, vis, s_pos, topk, n_cols, thr_ref, need_ref, jb_ref):
    kf = float(topk)
    c0 = _count(key_ref[...] >= 0)
    ans0 = jnp.where(c0 >= kf, 0, INT_MIN).astype(jnp.int32)

    def body(i, ans):
        cand = ans | jnp.left_shift(jnp.int32(1), 30 - i)
        cnt = _count(key_ref[...] >= cand)
        return jnp.where(cnt >= kf, cand, ans)

    thr = lax.fori_loop(0, 31, body, ans0)
    key = key_ref[...]
    need = kf - _count(key > thr)
    n_eq = _count((key == thr) & vis)
    thr_ref[...] = thr
    need_ref[...] = need
    jb_ref[...] = jnp.full(jb_ref.shape, n_cols, jnp.int32)
    n_bits = int(np.ceil(np.log2(n_cols))) + 1

    @pl.when(jnp.max(n_eq - need) > 0.0)
    def _():
        def body2(i, ans):
            cand = ans | jnp.left_shift(jnp.int32(1), n_bits - 1 - i)
            eqv = (key_ref[...] == thr_ref[...]) & vis
            c = _count(eqv & (s_pos < cand))
            return jnp.where(c < need_ref[...], cand, ans)

        jb_ref[...] = lax.fori_loop(0, n_bits, body2, jnp.zeros(jb_ref.shape, jnp.int32))

    return vis & ((key > thr) | ((key == thr) & (s_pos <= jb_ref[...])))


def _causal_extents(tp, n=4):
    nb = tp // BLK
    return [BLK * e for e in sorted({_cdiv(nb * (k + 1), n) for k in range(n)})]


def _for_causal_extent(qi, tp, body):
    lo = 0
    for extent in _causal_extents(tp):
        need = (qi + 1) * BLK
        pl.when((need > lo) & (need <= extent))(functools.partial(body, extent))
        lo = extent


def _idx_prompt_kernel(iq_ref, iw_ref, ik_ref, bias_ref, ikd_ref, key_ref, thr_ref, need_ref, jb_ref,
                       *, topk, tp):
    qi = pl.program_id(1)

    @pl.when(qi == 0)
    def _():
        ik = ik_ref[...]
        ikd_ref[...] = (ik + pltpu.roll(ik, 64, 1)).astype(BF16)

    def select(n_keys):
        lane = lax.broadcasted_iota(jnp.int32, (1, LANES), 1)
        keys = ikd_ref[:n_keys, :]
        score = jnp.zeros((BLK, n_keys), F32)
        for h in range(H_IDX):
            pair = iq_ref[:, (h // 2) * LANES:(h // 2 + 1) * LANES]
            lo = (h % 2) * IDX_DIM
            qh = jnp.where((lane >= lo) & (lane < lo + IDX_DIM), pair, 0.0).astype(BF16)
            w = iw_ref[:, h:h + 1] * (H_IDX ** -0.5 * IDX_DIM ** -0.5)
            score = score + jnp.maximum(_dot_nt(qh, keys), 0.0) * w

        t_pos = qi * BLK + lax.broadcasted_iota(jnp.int32, (BLK, 1), 0)
        s_pos = lax.broadcasted_iota(jnp.int32, (1, n_keys), 1)
        vis = s_pos <= t_pos
        keys_view = key_ref.at[:, :n_keys]
        keys_view[...] = jnp.where(vis, _sortable_key(score), KEY_NEG_INF)
        sel = _topk_select(keys_view, vis, s_pos, topk, n_keys, thr_ref, need_ref, jb_ref)
        bias_ref[:, :n_keys] = jnp.where(sel, 0.0, NEG).astype(BF16)
        if n_keys < tp:
            bias_ref[:, n_keys:] = jnp.full((BLK, tp - n_keys), NEG, BF16)

    _for_causal_extent(qi, tp, select)


def _idx_prompt(p0, n_batch, tp, topk):
    nqb = tp // BLK
    return pl.pallas_call(
        functools.partial(_idx_prompt_kernel, topk=topk, tp=tp),
        grid=(n_batch, nqb),
        in_specs=[pl.BlockSpec((BLK, 1024), lambda b, q: (b * nqb + q, 1)),
                  pl.BlockSpec((BLK, LANES), lambda b, q: (b * nqb + q, 49)),
                  pl.BlockSpec((tp, LANES), lambda b, q: (b, 48))],
        out_specs=pl.BlockSpec((BLK, tp), lambda b, q: (b * nqb + q, 0)),
        out_shape=jax.ShapeDtypeStruct((n_batch * tp, tp), BF16),
        scratch_shapes=[pltpu.VMEM((tp, LANES), BF16),
                        pltpu.VMEM((BLK, tp), jnp.int32),
                        pltpu.VMEM((BLK, 1), jnp.int32),
                        pltpu.VMEM((BLK, 1), F32),
                        pltpu.VMEM((BLK, 1), jnp.int32)],
        compiler_params=_params(("parallel", "arbitrary")),
        name="dsa_index_prompt",
    )(p0, p0, p0)


def _attn_a_prompt_kernel(q_ref, k_ref, v_ref, bias_ref, o_ref, kb_ref, vb_ref, *, tp):
    qi = pl.program_id(2)

    @pl.when(qi == 0)
    def _():
        kb_ref[...] = k_ref[...].astype(BF16)
        vb_ref[...] = v_ref[...].astype(BF16)

    def attend(n_keys):
        bias = bias_ref[:, :n_keys].astype(F32)
        for g in range(H_A // HKV_A):
            sl = slice(g * HEAD_DIM, (g + 1) * HEAD_DIM)
            s = _dot_nt(q_ref[:, sl].astype(BF16), kb_ref[:n_keys, :]) * SCALE + bias
            m = jnp.max(s, axis=1, keepdims=True)
            p = jnp.exp(s - m)
            l = jnp.sum(p, axis=1, keepdims=True)
            o_ref[:, sl] = _dot(p.astype(BF16), vb_ref[:n_keys, :]) / l

    _for_causal_extent(qi, tp, attend)


def _attn_a_prompt(p0, bias, n_batch, tp):
    nqb = tp // BLK
    gw = (H_A // HKV_A) * HEAD_DIM
    return pl.pallas_call(
        functools.partial(_attn_a_prompt_kernel, tp=tp),
        grid=(n_batch, HKV_A, nqb),
        in_specs=[pl.BlockSpec((BLK, gw), lambda b, k, q: (b * nqb + q, k)),
                  pl.BlockSpec((tp, HEAD_DIM), lambda b, k, q: (b, 16 + k)),
                  pl.BlockSpec((tp, HEAD_DIM), lambda b, k, q: (b, 20 + k)),
                  pl.BlockSpec((BLK, tp), lambda b, k, q: (b * nqb + q, 0))],
        out_specs=pl.BlockSpec((BLK, gw), lambda b, k, q: (b * nqb + q, k)),
        out_shape=jax.ShapeDtypeStruct((n_batch * tp, H_A * HEAD_DIM), F32),
        scratch_shapes=[pltpu.VMEM((tp, HEAD_DIM), BF16), pltpu.VMEM((tp, HEAD_DIM), BF16)],
        compiler_params=_params(("parallel", "parallel", "arbitrary")),
        name="dsa_attend_prompt",
    )(p0, p0, p0, bias)


def _suffix_and_ones(group):
    r = lax.broadcasted_iota(jnp.int32, (LANES, 2 * LANES), 0)
    c = lax.broadcasted_iota(jnp.int32, (LANES, 2 * LANES), 1)
    return ((c >= LANES) | (r // group > c // group)).astype(BF16)


def _pick_tq(tp):
    return 384 if tp % 384 == 0 else BLK


def _stick_weights(z, strict_fn, tail, sums):
    rows = z.shape[0]
    pieces = _lane_chunks(z)
    n = len(pieces)
    lsn = [_neg_softplus_bulk(p) for p in pieces]
    ok = [strict_fn(i) for i in range(n)]
    keep = lambda i, x: x if ok[i] is None else jnp.where(ok[i], x, 0.0)
    hi, lo = _split2(jnp.concatenate([keep(i, lsn[i]) for i in range(n)], axis=0))
    ar = _dot(hi, sums) + _dot(lo, sums)
    w = [None] * n
    for i in reversed(range(n)):
        blk = ar[i * rows:(i + 1) * rows]
        w[i] = keep(i, jnp.exp(pieces[i] + lsn[i] + tail + blk[:, :LANES]))
        tail = tail + blk[:, LANES:]
    return jnp.concatenate(w, axis=1), tail


def _attn_b_prompt_kernel(q_ref, k_ref, v_ref, o_ref, kb_ref, vb_ref, acc_ref, carry_ref, *, tq):
    qi = pl.program_id(2)

    @pl.when(qi == 0)
    def _():
        kb_ref[...] = k_ref[...].astype(BF16)
        vb_ref[...] = v_ref[...].astype(BF16)

    q = q_ref[...].astype(BF16)
    acc_ref[...] = jnp.zeros_like(acc_ref)
    carry_ref[...] = jnp.zeros_like(carry_ref)
    r = lax.broadcasted_iota(jnp.int32, (tq, BLK), 0)
    c = lax.broadcasted_iota(jnp.int32, (tq, BLK), 1)
    sums = _suffix_and_ones(1)

    def block(kb, strict_fn):
        off = pl.multiple_of(kb * tq, BLK)
        z = _dot_nt(q, kb_ref[pl.ds(off, tq), :]) * SCALE
        w, tail = _stick_weights(z, strict_fn, carry_ref[...], sums)
        acc_ref[...] += _dot(w.astype(BF16), vb_ref[pl.ds(off, tq), :])
        carry_ref[...] = tail

    block(qi, lambda i: (i * BLK + c) < r)

    def body(it, carry_unused):
        block(qi - 1 - it, lambda i: None)
        return carry_unused

    lax.fori_loop(0, qi, body, 0)
    o_ref[...] = acc_ref[...]


def _attn_b_prompt(p0, n_batch, tp):
    tq = _pick_tq(tp)
    nqb = tp // tq
    return pl.pallas_call(
        functools.partial(_attn_b_prompt_kernel, tq=tq),
        grid=(n_batch, H_B, nqb),
        in_specs=[pl.BlockSpec((tq, HEAD_DIM), lambda b, h, q: (b * nqb + q, 24 + h)),
                  pl.BlockSpec((tp, HEAD_DIM), lambda b, h, q: (b, 32 + h)),
                  pl.BlockSpec((tp, HEAD_DIM), lambda b, h, q: (b, 40 + h))],
        out_specs=pl.BlockSpec((tq, HEAD_DIM), lambda b, h, q: (b * nqb + q, h)),
        out_shape=jax.ShapeDtypeStruct((n_batch * tp, H_B * HEAD_DIM), F32),
        scratch_shapes=[pltpu.VMEM((tp, HEAD_DIM), BF16), pltpu.VMEM((tp, HEAD_DIM), BF16),
                        pltpu.VMEM((tq, HEAD_DIM), F32), pltpu.VMEM((tq, LANES), F32)],
        compiler_params=_params(("parallel", "parallel", "arbitrary")),
        name="stickbreak_prompt",
    )(p0, p0, p0)


def _logf_kernel(f_ref, bf_ref, logf_ref, cum_ref, carry_ref, *, blocks_per_seq):
    i = pl.program_id(0)

    @pl.when(i % blocks_per_seq == 0)
    def _():
        carry_ref[...] = jnp.zeros_like(carry_ref)

    x = f_ref[...] + bf_ref[...]
    logf = _neg_softplus(-x)
    logf_ref[...] = logf
    r = lax.broadcasted_iota(jnp.int32, (BLK, BLK), 0)
    c = lax.broadcasted_iota(jnp.int32, (BLK, BLK), 1)
    cum = carry_ref[...] + _dot3((c <= r).astype(BF16), logf)
    cum_ref[...] = cum
    carry_ref[...] = cum[BLK - 1:BLK, :]


def _logf(p1, bf_row, n_rows, blocks_per_seq):
    return pl.pallas_call(
        functools.partial(_logf_kernel, blocks_per_seq=blocks_per_seq),
        grid=(_cdiv(n_rows, BLK),),
        in_specs=[pl.BlockSpec((BLK, LANES), lambda i: (i, 48)),
                  pl.BlockSpec((1, LANES), lambda i: (0, 0))],
        out_specs=[pl.BlockSpec((BLK, LANES), lambda i: (i, 0)),
                   pl.BlockSpec((BLK, LANES), lambda i: (i, 0))],
        out_shape=[jax.ShapeDtypeStruct((n_rows, LANES), F32),
                   jax.ShapeDtypeStruct((n_rows, LANES), F32)],
        scratch_shapes=[pltpu.VMEM((1, LANES), F32)],
        compiler_params=_params(("arbitrary",)),
        name="log_forget_cumsum",
    )(p1, bf_row)


HPS = 2


def _attn_c_prompt_kernel(q_ref, k_ref, v_ref, cq_ref, ck_ref, o_ref, kb_ref, vb_ref, s_ref, m_ref, l_ref,
                          acc_ref, *, tq):
    qi = pl.program_id(2)

    @pl.when(qi == 0)
    def _():
        kb_ref[...] = k_ref[...].astype(BF16)
        vb_ref[...] = v_ref[...].astype(BF16)

    head = lambda x, hh: x[:, hh * HEAD_DIM:(hh + 1) * HEAD_DIM]
    q = [head(q_ref, hh).astype(BF16) for hh in range(HPS)]
    cq = [cq_ref[0, hh] for hh in range(HPS)]
    m_ref[...] = jnp.full(m_ref.shape, NEG, F32)

    def logits(kb, hh):
        off = pl.multiple_of(kb * tq, LANES)
        ck = ck_ref[0, hh, :, pl.ds(off, tq)]
        keys = kb_ref[pl.ds(off, tq), hh * HEAD_DIM:(hh + 1) * HEAD_DIM]
        return off, _dot_nt(q[hh], keys) * SCALE + (cq[hh] - ck)

    def keep(off, s, hh):
        s_ref[hh, :, pl.ds(off, tq)] = s
        m = m_ref[hh]
        for piece in _lane_chunks(s):
            m = jnp.maximum(m, piece)
        m_ref[hh] = m

    def pass1(kb, carry_unused):
        for hh in range(HPS):
            keep(*logits(kb, hh), hh)
        return carry_unused

    lax.fori_loop(0, qi, pass1, 0)
    r = lax.broadcasted_iota(jnp.int32, (tq, tq), 0)
    c = lax.broadcasted_iota(jnp.int32, (tq, tq), 1)
    for hh in range(HPS):
        off, s = logits(qi, hh)
        keep(off, jnp.where(c <= r, s, NEG), hh)

    m = [jnp.max(m_ref[hh], axis=1, keepdims=True) for hh in range(HPS)]
    l_ref[...] = jnp.zeros_like(l_ref)
    acc_ref[...] = jnp.zeros_like(acc_ref)

    def pass2(kb, carry_unused):
        off = pl.multiple_of(kb * tq, LANES)
        for hh in range(HPS):
            p = jnp.exp(s_ref[hh, :, pl.ds(off, tq)] - m[hh])
            l = l_ref[hh]
            for piece in _lane_chunks(p):
                l = l + piece
            l_ref[hh] = l
            sl = slice(hh * HEAD_DIM, (hh + 1) * HEAD_DIM)
            acc_ref[:, sl] += _dot(p.astype(BF16), vb_ref[pl.ds(off, tq), sl])
        return carry_unused

    lax.fori_loop(0, qi + 1, pass2, 0)
    for hh in range(HPS):
        sl = slice(hh * HEAD_DIM, (hh + 1) * HEAD_DIM)
        o_ref[:, sl] = acc_ref[:, sl] / jnp.sum(l_ref[hh], axis=1, keepdims=True)


def _attn_c_prompt(p1, cum_col, cum_row, n_batch, tp):
    tq = _pick_tq(tp)
    nqb = tp // tq
    wide = HPS * HEAD_DIM
    k0 = H_C // HPS
    return pl.pallas_call(
        functools.partial(_attn_c_prompt_kernel, tq=tq),
        grid=(n_batch, H_C // HPS, nqb),
        in_specs=[pl.BlockSpec((tq, wide), lambda b, h, q: (b * nqb + q, h)),
                  pl.BlockSpec((tp, wide), lambda b, h, q: (b, k0 + h)),
                  pl.BlockSpec((tp, wide), lambda b, h, q: (b, 2 * k0 + h)),
                  pl.BlockSpec((1, HPS, tq, 1), lambda b, h, q: (b, h, q, 0)),
                  pl.BlockSpec((1, HPS, 1, tp), lambda b, h, q: (b, h, 0, 0))],
        out_specs=pl.BlockSpec((tq, wide), lambda b, h, q: (b * nqb + q, h)),
        out_shape=jax.ShapeDtypeStruct((n_batch * tp, H_C * HEAD_DIM), F32),
        scratch_shapes=[pltpu.VMEM((tp, wide), BF16), pltpu.VMEM((tp, wide), BF16),
                        pltpu.VMEM((HPS, tq, tp), F32), pltpu.VMEM((HPS, tq, LANES), F32),
                        pltpu.VMEM((HPS, tq, LANES), F32), pltpu.VMEM((tq, wide), F32)],
        compiler_params=_params(("parallel", "parallel", "arbitrary")),
        name="forget_attend_prompt",
    )(p1, p1, p1, cum_col, cum_row)


def _pages_per_step(n_pages, want):
    while n_pages % want:
        want //= 2
    return want


def _idx_sample_kernel(pt_ref, iq_ref, iw_ref, ikn_ref, *rest, topk, n_pages, pps, ds):
    page_refs = rest[:pps]
    bias_ref, score_ref, key_ref, thr_ref, need_ref, jb_ref = rest[pps:]
    j = pl.program_id(1)
    past = n_pages * BLK
    ncol = past + BLK
    nq = iq_ref.shape[1] // H_IDX
    q = iq_ref[0].astype(BF16)
    wgt = iw_ref[0] * (H_IDX ** -0.5)

    def scores(keys):
        dots = _dot_nt(q, keys.astype(BF16)) * (IDX_DIM ** -0.5)
        wd = jnp.maximum(dots, 0.0) * wgt
        return jnp.sum(wd.reshape(nq, H_IDX, BLK), axis=1)

    for i, page_ref in enumerate(page_refs):
        score_ref[:, pl.ds(pl.multiple_of((j * pps + i) * BLK, BLK), BLK)] = scores(page_ref[0])

    @pl.when(j == n_pages // pps - 1)
    def _():
        score_ref[:, past:] = scores(ikn_ref[0])
        t_idx = lax.broadcasted_iota(jnp.int32, (nq, 1), 0) % ds
        s_pos = lax.broadcasted_iota(jnp.int32, (1, ncol), 1)
        vis = s_pos <= past + t_idx
        key_ref[...] = jnp.where(vis, _sortable_key(score_ref[...]), KEY_NEG_INF)
        sel = _topk_select(key_ref, vis, s_pos, topk, ncol, thr_ref, need_ref, jb_ref)
        bias_ref[0] = jnp.where(sel, 0.0, NEG)


def _idx_sample(page_table, iq_th, iw_th, ik_new, cache_idx, topk, ds):
    db, n_pages = page_table.shape
    ncol = n_pages * BLK + BLK
    rows = iq_th.shape[1]
    nq = rows // H_IDX
    pps = _pages_per_step(n_pages, 16)

    def page_spec(i):
        return pl.BlockSpec((1, BLK, IDX_DIM), lambda b, j, pt: (pt[b, j * pps + i], 0, 0))

    return pl.pallas_call(
        functools.partial(_idx_sample_kernel, topk=topk, n_pages=n_pages, pps=pps, ds=ds),
        grid_spec=pltpu.PrefetchScalarGridSpec(
            num_scalar_prefetch=1,
            grid=(db, n_pages // pps),
            in_specs=[pl.BlockSpec((1, rows, IDX_DIM), lambda b, j, pt: (b, 0, 0)),
                      pl.BlockSpec((1, rows, 1), lambda b, j, pt: (b, 0, 0)),
                      pl.BlockSpec((1, BLK, IDX_DIM), lambda b, j, pt: (b, 0, 0))]
                     + [page_spec(i) for i in range(pps)],
            out_specs=pl.BlockSpec((1, nq, ncol), lambda b, j, pt: (b, 0, 0)),
            scratch_shapes=[pltpu.VMEM((nq, ncol), F32),
                            pltpu.VMEM((nq, ncol), jnp.int32),
                            pltpu.VMEM((nq, 1), jnp.int32),
                            pltpu.VMEM((nq, 1), F32),
                            pltpu.VMEM((nq, 1), jnp.int32)]),
        out_shape=jax.ShapeDtypeStruct((db, nq, ncol), F32),
        compiler_params=_params(("parallel", "arbitrary")),
        name="dsa_index_sample",
    )(page_table, iq_th, iw_th, ik_new, *([cache_idx] * pps))


def _softmax_update(s, pv_fn, m_ref, l_ref, acc_ref):
    m_old = m_ref[...]
    m_new = jnp.maximum(m_old, jnp.max(s, axis=1, keepdims=True))
    corr = jnp.exp(m_old - m_new)
    p = jnp.exp(s - m_new)
    l_ref[...] = l_ref[...] * corr + jnp.sum(p, axis=1, keepdims=True)
    acc_ref[...] = acc_ref[...] * corr + pv_fn(p)
    m_ref[...] = m_new


def _head_match(n_rows, heads_per_lane_group, rows_per_head):
    row = lax.broadcasted_iota(jnp.int32, (n_rows, LANES), 0)
    lane = lax.broadcasted_iota(jnp.int32, (n_rows, LANES), 1)
    return (lane % heads_per_lane_group) == (row // rows_per_head), lane, row


def _attn_a_sample_kernel(pt_ref, q_ref, bias_ref, new_ref, *rest, n_steps, pps, ds):
    page_refs = rest[:pps]
    o_ref, rep_ref, m_ref, l_ref, acc_ref = rest[pps:]
    b = pl.program_id(0)
    j = pl.program_id(1)
    n_slots = 2 * HKV_A
    n_rows = H_A * ds
    n_tile = n_rows // bias_ref.shape[1]

    @pl.when((b == 0) & (j == 0))
    def _():
        s_i = lax.broadcasted_iota(jnp.int32, rep_ref.shape, 0)
        c_i = lax.broadcasted_iota(jnp.int32, rep_ref.shape, 1)
        rep_ref[...] = (c_i // n_slots == s_i).astype(BF16)

    @pl.when(j == 0)
    def _():
        m_ref[...] = jnp.full(m_ref.shape, NEG, F32)
        l_ref[...] = jnp.zeros_like(l_ref)
        acc_ref[...] = jnp.zeros_like(acc_ref)

    q = q_ref[0].astype(BF16)
    match, _, _ = _head_match(n_rows, n_slots, ds * (H_A // HKV_A))

    def attend(blocks):
        pgs = [rows.astype(BF16) for rows, _ in blocks]
        pieces = []
        for pg, (_, picked) in zip(pgs, blocks):
            cols = pg.shape[0]
            pick = jnp.concatenate([picked.astype(BF16)] * n_tile, axis=0)
            pick = _dot(pick, rep_ref[:, :cols])
            pieces += [jnp.where(match & (pk > 0.5), sc, NEG)
                       for sc, pk in zip(_lane_chunks(_dot_nt(q, pg) * SCALE), _lane_chunks(pick))]

        def pv(p):
            out, at = 0.0, 0
            for pg in pgs:
                part = p[:, at:at + pg.shape[0]]
                moved = jnp.concatenate([pltpu.roll(x, HKV_A, 1) for x in _lane_chunks(part)], axis=1)
                out = out + _dot(moved.astype(BF16), pg)
                at += pg.shape[0]
            return out

        _softmax_update(jnp.concatenate(pieces, axis=1), pv, m_ref, l_ref, acc_ref)

    def picked(i):
        return jnp.where(bias_ref[0][:, i * BLK:(i + 1) * BLK] == 0.0, 1.0, 0.0)

    @pl.when(j < n_steps)
    def _():
        attend([(page_ref[0], picked(i)) for i, page_ref in enumerate(page_refs)])

    @pl.when(j == n_steps)
    def _():
        attend([(new_ref[0], picked(0))])
        o_ref[0] = acc_ref[...] / l_ref[...]


def _attn_a_sample(page_table, q_rows, bias, new_rows, cache_flat, ds):
    db, n_pages = page_table.shape
    n_slots = 2 * HKV_A
    n_rows = H_A * ds
    pps = _pages_per_step(n_pages, 8)
    n_steps = n_pages // pps
    last = n_steps - 1

    def page_spec(i):
        return pl.BlockSpec((1, BLK * n_slots, HEAD_DIM),
                            lambda b, j, pt: (pt[b, jnp.minimum(j, last) * pps + i], 0, 0))

    return pl.pallas_call(
        functools.partial(_attn_a_sample_kernel, n_steps=n_steps, pps=pps, ds=ds),
        grid_spec=pltpu.PrefetchScalarGridSpec(
            num_scalar_prefetch=1,
            grid=(db, n_steps + 1),
            in_specs=[pl.BlockSpec((1, n_rows, HEAD_DIM), lambda b, j, pt: (b, 0, 0)),
                      pl.BlockSpec((1, bias.shape[1], BLK * pps), lambda b, j, pt: (b, 0, j)),
                      pl.BlockSpec((1, LANES, HEAD_DIM), lambda b, j, pt: (b, 0, 0))]
                     + [page_spec(i) for i in range(pps)],
            out_specs=pl.BlockSpec((1, n_rows, HEAD_DIM), lambda b, j, pt: (b, 0, 0)),
            scratch_shapes=[pltpu.VMEM((BLK, BLK * n_slots), BF16),
                            pltpu.VMEM((n_rows, 1), F32), pltpu.VMEM((n_rows, 1), F32),
                            pltpu.VMEM((n_rows, HEAD_DIM), F32)]),
        out_shape=jax.ShapeDtypeStruct((db, n_rows, HEAD_DIM), F32),
        compiler_params=_params(("arbitrary", "arbitrary")),
        name="dsa_attend_sample",
    )(page_table, q_rows, bias, new_rows, *([cache_flat] * pps))


def _attn_b_sample_kernel(pt_ref, q_ref, knew_ref, vnew_ref, *rest, n_steps, pps, ds):
    page_refs = rest[:pps]
    o_ref, acc_ref, carry_ref = rest[pps:]
    j = pl.program_id(1)
    n_rows = H_B * ds
    q = q_ref[0].astype(BF16)
    match, lane, row = _head_match(n_rows, H_B, ds)
    sums = _suffix_and_ones(H_B)

    def attend(blocks, strict_fn):
        z = jnp.concatenate([_dot_nt(q, k.astype(BF16)) * SCALE for k, _ in blocks], axis=1)
        w, tail = _stick_weights(z, strict_fn, carry_ref[...], sums)
        carry_ref[...] = tail
        out, at = 0.0, 0
        for k, v in blocks:
            out = out + _dot(w[:, at:at + k.shape[0]].astype(BF16), v.astype(BF16))
            at += k.shape[0]
        acc_ref[...] += out

    @pl.when(j == 0)
    def _():
        acc_ref[...] = jnp.zeros_like(acc_ref)
        carry_ref[...] = jnp.zeros_like(carry_ref)
        s_new = lane // H_B
        attend([(knew_ref[0], vnew_ref[0])], lambda i: match & (s_new < row % ds))

    @pl.when(j > 0)
    def _():
        attend([(page_ref[0, :, 0].reshape(BLK * H_B, HEAD_DIM), page_ref[0, :, 1].reshape(BLK * H_B, HEAD_DIM))
                for page_ref in reversed(page_refs)], lambda i: match)

    @pl.when(j == n_steps)
    def _():
        o_ref[0] = acc_ref[...]


def _attn_b_sample(page_table, q_rows, k_new, v_new, cache, ds):
    db, n_pages = page_table.shape
    n_rows = H_B * ds
    pps = _pages_per_step(n_pages, 4)
    n_steps = n_pages // pps

    def page_spec(i):
        return pl.BlockSpec((1, BLK, 2, H_B, HEAD_DIM),
                            lambda b, j, pt: (pt[b, n_pages - 1 - (jnp.maximum(j, 1) - 1) * pps - i], 0, 0, 0, 0))

    new_spec = pl.BlockSpec((1, LANES, HEAD_DIM), lambda b, j, pt: (b, 0, 0))
    return pl.pallas_call(
        functools.partial(_attn_b_sample_kernel, n_steps=n_steps, pps=pps, ds=ds),
        grid_spec=pltpu.PrefetchScalarGridSpec(
            num_scalar_prefetch=1,
            grid=(db, n_steps + 1),
            in_specs=[pl.BlockSpec((1, n_rows, HEAD_DIM), lambda b, j, pt: (b, 0, 0)), new_spec, new_spec]
                     + [page_spec(i) for i in range(pps)],
            out_specs=pl.BlockSpec((1, n_rows, HEAD_DIM), lambda b, j, pt: (b, 0, 0)),
            scratch_shapes=[pltpu.VMEM((n_rows, HEAD_DIM), F32), pltpu.VMEM((n_rows, LANES), F32)]),
        out_shape=jax.ShapeDtypeStruct((db, n_rows, HEAD_DIM), F32),
        compiler_params=_params(("parallel", "arbitrary")),
        name="stickbreak_sample",
    )(page_table, q_rows, k_new, v_new, *([cache] * pps))


def _page_suffix_kernel(lf_ref, sfx_ref, tot_ref):
    x = lf_ref[...]
    r = lax.broadcasted_iota(jnp.int32, (BLK, BLK), 0)
    c = lax.broadcasted_iota(jnp.int32, (BLK, BLK), 1)
    sfx_ref[...] = _dot3((c > r).astype(BF16), x)
    tot_ref[...] = _dot3(jnp.ones((8, BLK), BF16), x)


def _page_suffix(lf_t, tn):
    n = lf_t.shape[1]
    return pl.pallas_call(
        _page_suffix_kernel,
        grid=(n // tn,),
        in_specs=[pl.BlockSpec((BLK, tn), lambda i: (0, i))],
        out_specs=[pl.BlockSpec((BLK, tn), lambda i: (0, i)), pl.BlockSpec((8, tn), lambda i: (0, i))],
        out_shape=[jax.ShapeDtypeStruct((BLK, n), F32), jax.ShapeDtypeStruct((8, n), F32)],
        compiler_params=_params(("parallel",)),
        name="log_forget_page_suffix",
    )(lf_t)


def _attn_c_sample_kernel(pt_ref, q_ref, lfnew_ref, knew_ref, vnew_ref, *rest, n_steps, pps, ds):
    sfx_refs, tot_refs, page_refs = rest[:pps], rest[pps:2 * pps], rest[2 * pps:3 * pps]
    o_ref, m_ref, l_ref, acc_ref, carry_ref, cn_ref = rest[3 * pps:]
    j = pl.program_id(1)
    n_rows = H_C * ds
    q = q_ref[0].astype(BF16)
    match, lane, row = _head_match(n_rows, H_C, ds)

    @pl.when(j == 0)
    def _():
        m_ref[...] = jnp.full(m_ref.shape, NEG, F32)
        l_ref[...] = jnp.zeros_like(l_ref)
        acc_ref[...] = jnp.zeros_like(acc_ref)
        carry_ref[...] = jnp.zeros_like(carry_ref)
        lf = lfnew_ref[0]
        s_new = lane // H_C
        cn_ref[...] = jnp.sum(jnp.where(match & (s_new <= row % ds), lf, 0.0), axis=1, keepdims=True)
        r2 = lax.broadcasted_iota(jnp.int32, (LANES, LANES), 0)
        c2 = lax.broadcasted_iota(jnp.int32, (LANES, LANES), 1)
        upto = ((r2 % H_C == c2 % H_C) & (r2 // H_C <= c2 // H_C)).astype(BF16)
        cum_keys = _dot3_left(jnp.broadcast_to(lf, (8, LANES)), upto)[0:1]
        k_rows = knew_ref[0].astype(BF16)
        s = _dot_nt(q, k_rows) * SCALE + (cn_ref[...] - cum_keys)
        s = jnp.where(match & (s_new <= row % ds), s, NEG)
        _softmax_update(s, lambda p: _dot(p.astype(BF16), vnew_ref[0].astype(BF16)), m_ref, l_ref, acc_ref)

    @pl.when(j > 0)
    def _():
        later = carry_ref[...]
        pieces, values = [], []
        for sfx_ref, tot_ref, page_ref in zip(sfx_refs, tot_refs, page_refs):
            k_rows = page_ref[0, :, 0].reshape(BLK * H_C, HEAD_DIM).astype(BF16)
            values.append(page_ref[0, :, 1].reshape(BLK * H_C, HEAD_DIM).astype(BF16))
            s = _dot_nt(q, k_rows) * SCALE + (sfx_ref[0] + (cn_ref[...] + later))
            pieces += [jnp.where(match, piece, NEG) for piece in _lane_chunks(s)]
            later = later + tot_ref[0]
        carry_ref[...] = later
        cols = BLK * H_C

        def pv(p):
            out = 0.0
            for i, v_rows in enumerate(values):
                out = out + _dot(p[:, i * cols:(i + 1) * cols].astype(BF16), v_rows)
            return out

        _softmax_update(jnp.concatenate(pieces, axis=1), pv, m_ref, l_ref, acc_ref)

    @pl.when(j == n_steps)
    def _():
        o_ref[0] = acc_ref[...] / l_ref[...]


def _attn_c_sample(page_table, q_rows, lf_new, k_new, v_new, sfx_flat, tot_col, cache, ds):
    db, n_pages = page_table.shape
    n_rows = H_C * ds
    pps = _pages_per_step(n_pages, 4)
    n_steps = n_pages // pps

    def page_of(i):
        return lambda b, j, pt: pt[b, n_pages - 1 - (jnp.maximum(j, 1) - 1) * pps - i]

    def specs(shape):
        zeros = (0,) * (len(shape) - 1)
        return [pl.BlockSpec(shape, (lambda f: lambda b, j, pt: (f(b, j, pt),) + zeros)(page_of(i)))
                for i in range(pps)]

    new_spec = pl.BlockSpec((1, LANES, HEAD_DIM), lambda b, j, pt: (b, 0, 0))
    return pl.pallas_call(
        functools.partial(_attn_c_sample_kernel, n_steps=n_steps, pps=pps, ds=ds),
        grid_spec=pltpu.PrefetchScalarGridSpec(
            num_scalar_prefetch=1,
            grid=(db, n_steps + 1),
            in_specs=[pl.BlockSpec((1, n_rows, HEAD_DIM), lambda b, j, pt: (b, 0, 0)),
                      pl.BlockSpec((1, 1, LANES), lambda b, j, pt: (b, 0, 0)),
                      new_spec, new_spec]
                     + specs((1, 1, BLK * H_C)) + specs((1, n_rows, 1)) + specs((1, BLK, 2, H_C, HEAD_DIM)),
            out_specs=pl.BlockSpec((1, n_rows, HEAD_DIM), lambda b, j, pt: (b, 0, 0)),
            scratch_shapes=[pltpu.VMEM((n_rows, 1), F32), pltpu.VMEM((n_rows, 1), F32),
                            pltpu.VMEM((n_rows, HEAD_DIM), F32), pltpu.VMEM((n_rows, 1), F32),
                            pltpu.VMEM((n_rows, 1), F32)]),
        out_shape=jax.ShapeDtypeStruct((db, n_rows, HEAD_DIM), F32),
        compiler_params=_params(("parallel", "arbitrary")),
        name="forget_attend_sample",
    )(page_table, q_rows, lf_new, k_new, v_new, *([sfx_flat] * pps), *([tot_col] * pps), *([cache] * pps))


def _layer_norm(xf, g, b):
    mu = jnp.mean(xf, axis=1, keepdims=True)
    d = xf - mu
    var = jnp.mean(d * d, axis=1, keepdims=True)
    return d * lax.rsqrt(var + LN_EPS) * g + b


def _route(y, wr_ref, br_ref):
    y_hi, y_lo = _split2(y)
    w = wr_ref[...]
    p1 = _dot(y_hi, w)
    p2 = _dot(y_lo, w)
    logits = p1 + pltpu.roll(p1, LANES - N_EXPERTS, 1) + p2
    lane = lax.broadcasted_iota(jnp.int32, logits.shape, 1)
    in_grp = lane < N_GROUPS
    s = [jax.nn.sigmoid(logits if j == 0 else pltpu.roll(logits, LANES - j * N_GROUPS, 1))
         for j in range(EPG)]
    sel = [jnp.where(in_grp, s[j] + br_ref[j:j + 1, :], NEG) for j in range(EPG)]
    top2 = None
    for a in range(EPG):
        for b in range(a + 1, EPG):
            pair = sel[a] + sel[b]
            top2 = pair if top2 is None else jnp.maximum(top2, pair)
    top2 = jnp.where(in_grp, top2, -jnp.inf)
    best_val = jnp.max(top2, axis=1, keepdims=True)
    g_best = jnp.min(jnp.where(top2 == best_val, lane, LANES), axis=1, keepdims=True)
    mine = lane == g_best
    picked = []
    for j in range(EPG):
        rank = jnp.zeros(logits.shape, F32)
        for i in range(EPG):
            if i == j:
                continue
            ahead = (sel[i] >= sel[j]) if i < j else (sel[i] > sel[j])
            rank = rank + ahead.astype(F32)
        picked.append(jnp.sum(jnp.where(mine & (rank < 2.0), s[j], 0.0), axis=1, keepdims=True))
    denom = picked[0] + picked[1] + picked[2] + picked[3]
    extra = jnp.where(lane == EPG, g_best.astype(F32), 0.0)
    for j in range(EPG):
        extra = jnp.where(lane == j, picked[j] / denom, extra)
    return extra


def _outproj_kernel(o1_ref, o2_ref, s1_ref, s2_ref, w1_ref, w2_ref, x_ref, g_ref, b_ref, wr_ref, br_ref, out_ref,
                    *, d_model, n_prompt_tiles):
    def finish(o1, o2, rows):
        mix = _dot(o1.astype(BF16), w1_ref[...]) + _dot(o2.astype(BF16), w2_ref[...])
        y = _layer_norm(ALPHA * x_ref[:rows, :] + mix, g_ref[...], b_ref[...])
        out_ref[:rows, :d_model] = y
        out_ref[:rows, d_model:] = _route(y, wr_ref, br_ref)

    @pl.when(pl.program_id(0) < n_prompt_tiles)
    def _():
        finish(o1_ref[...], o2_ref[...], o1_ref.shape[0])

    @pl.when(pl.program_id(0) >= n_prompt_tiles)
    def _():
        finish(s1_ref[...], s2_ref[...], s1_ref.shape[0])


def _outproj(o1, o2, s1, s2, o2_block, w_bf16, x, g, b, wr, br, d_model, tm):
    nf = x.shape[0]
    half = w_bf16.shape[0] // 2
    n_prompt, n_sample = o1.shape[0], s1.shape[0]
    assert n_prompt % tm == 0 and n_sample <= tm and n_prompt + n_sample == nf
    last = n_prompt // tm - 1
    return pl.pallas_call(
        functools.partial(_outproj_kernel, d_model=d_model, n_prompt_tiles=n_prompt // tm),
        grid=(n_prompt // tm + 1,),
        in_specs=[pl.BlockSpec((tm, half), lambda i: (jnp.minimum(i, last), 0)),
                  pl.BlockSpec((tm, half), lambda i: (jnp.minimum(i, last), o2_block)),
                  pl.BlockSpec((n_sample, half), lambda i: (0, 0)),
                  pl.BlockSpec((n_sample, half), lambda i: (0, o2_block)),
                  pl.BlockSpec((half, d_model), lambda i: (0, 0)),
                  pl.BlockSpec((half, d_model), lambda i: (1, 0)),
                  pl.BlockSpec((tm, d_model), lambda i: (i, 0)),
                  pl.BlockSpec((1, d_model), lambda i: (0, 0)),
                  pl.BlockSpec((1, d_model), lambda i: (0, 0)),
                  pl.BlockSpec((d_model, LANES), lambda i: (0, 0)),
                  pl.BlockSpec((EPG, LANES), lambda i: (0, 0))],
        out_specs=pl.BlockSpec((tm, d_model + XCOLS), lambda i: (i, 0)),
        out_shape=jax.ShapeDtypeStruct((nf, d_model + XCOLS), F32),
        compiler_params=_params(("parallel",)),
        name="outproj_norm_route",
    )(o1, o2, s1, s2, w_bf16, w_bf16, x, g, b, wr, br)


def _gather_rows(idx_ref, base, src_ref, dst_ref, sem, n_rows):
    def issue(r, c):
        pltpu.make_async_copy(src_ref.at[pl.ds(idx_ref[base + r], 1)], dst_ref.at[pl.ds(r, 1)], sem).start()
        return c

    lax.fori_loop(0, n_rows, issue, 0)

    def drain(r, c):
        pltpu.make_async_copy(src_ref.at[pl.ds(0, 1)], dst_ref.at[pl.ds(r, 1)], sem).wait()
        return c

    lax.fori_loop(0, n_rows, drain, 0)


def _moe_kernel(grp_ref, valid_ref, src_ref, xa_ref, wg_ref, wu_ref, wd_ref, g_ref, b_ref, o_ref,
                xs_ref, xb_ref, acc_ref, sem, *, d_model, tm):
    i = pl.program_id(0)
    e = pl.program_id(1)
    valid = valid_ref[i] == 1

    @pl.when(valid & (e == 0))
    def _():
        _gather_rows(src_ref, i * tm, xa_ref, xs_ref, sem, tm)
        xb_ref[...] = xs_ref[:, :d_model].astype(BF16)
        acc_ref[...] = jnp.zeros_like(acc_ref)

    @pl.when(valid)
    def _():
        xb = xb_ref[...]
        a = _dot(xb, wg_ref[0].astype(BF16))
        h = a * jax.nn.sigmoid(a) * _dot(xb, wu_ref[0].astype(BF16))
        extra = xs_ref[:, d_model:]
        lane = lax.broadcasted_iota(jnp.int32, extra.shape, 1)
        gate = jnp.sum(jnp.where(lane == e, extra, 0.0), axis=1, keepdims=True)
        acc_ref[...] += _dot((h * gate).astype(BF16), wd_ref[0].astype(BF16))

    @pl.when(valid & (e == EPG - 1))
    def _():
        o_ref[...] = _layer_norm(ALPHA * xs_ref[:, :d_model] + acc_ref[...], g_ref[...], b_ref[...])

    @pl.when(jnp.logical_not(valid) & (e == EPG - 1))
    def _():
        o_ref[...] = jnp.zeros_like(o_ref)


def _moe(tile_grp, tile_valid, src, xa, wg, wu, wd, g, b, d_model, tm):
    n_tiles = tile_grp.shape[0]
    d_exp = wg.shape[2]

    def w_idx(i, e, grp, valid, src):
        return (grp[i] * EPG + jnp.where(valid[i] == 1, e, EPG - 1), 0, 0)

    return pl.pallas_call(
        functools.partial(_moe_kernel, d_model=d_model, tm=tm),
        grid_spec=pltpu.PrefetchScalarGridSpec(
            num_scalar_prefetch=3,
            grid=(n_tiles, EPG),
            in_specs=[pl.BlockSpec(memory_space=pl.ANY),
                      pl.BlockSpec((1, d_model, d_exp), w_idx),
                      pl.BlockSpec((1, d_model, d_exp), w_idx),
                      pl.BlockSpec((1, d_exp, d_model), w_idx),
                      pl.BlockSpec((1, d_model), lambda i, e, *_: (0, 0)),
                      pl.BlockSpec((1, d_model), lambda i, e, *_: (0, 0))],
            out_specs=pl.BlockSpec((tm, d_model), lambda i, e, *_: (i, 0)),
            scratch_shapes=[pltpu.VMEM((tm, d_model + XCOLS), F32),
                            pltpu.VMEM((tm, d_model), BF16),
                            pltpu.VMEM((tm, d_model), F32),
                            pltpu.SemaphoreType.DMA(())]),
        out_shape=jax.ShapeDtypeStruct((n_tiles * tm, d_model), F32),
        compiler_params=_params(("arbitrary", "arbitrary")),
        name="grouped_moe",
    )(tile_grp, tile_valid, src, xa, wg, wu, wd, g, b)


def _unpermute_kernel(idx_ref, src_ref, o_ref, sem, *, tg):
    _gather_rows(idx_ref, pl.program_id(0) * tg, src_ref, o_ref, sem, tg)


def _unpermute(dest_padded, ys, n_rows, tg):
    d = ys.shape[1]
    return pl.pallas_call(
        functools.partial(_unpermute_kernel, tg=tg),
        grid_spec=pltpu.PrefetchScalarGridSpec(
            num_scalar_prefetch=1,
            grid=(_cdiv(n_rows, tg),),
            in_specs=[pl.BlockSpec(memory_space=pl.ANY)],
            out_specs=pl.BlockSpec((tg, d), lambda i, idx: (i, 0)),
            scratch_shapes=[pltpu.SemaphoreType.DMA(())]),
        out_shape=jax.ShapeDtypeStruct((n_rows, d), F32),
        compiler_params=_params(("arbitrary",)),
        name="unpermute_rows",
    )(dest_padded, ys)


def _routing_plan(grp, tm):
    nf = grp.shape[0]
    n_tiles = _cdiv(nf + N_GROUPS * (tm - 1), tm)
    onehot = (grp[:, None] == jnp.arange(N_GROUPS, dtype=jnp.int32)[None, :]).astype(jnp.int32)
    counts = jnp.sum(onehot, axis=0)
    rank = jnp.sum((jnp.cumsum(onehot, axis=0) - onehot) * onehot, axis=1)
    padded = ((counts + tm - 1) // tm) * tm
    ends = jnp.cumsum(padded)
    dest = (ends - padded)[grp] + rank
    src = jnp.zeros((n_tiles * tm,), jnp.int32).at[dest].set(jnp.arange(nf, dtype=jnp.int32))
    starts = jnp.arange(n_tiles, dtype=jnp.int32) * tm
    tile_valid = (starts < ends[-1]).astype(jnp.int32)
    tile_grp = jnp.minimum(jnp.searchsorted(ends, starts, side="right"), N_GROUPS - 1).astype(jnp.int32)
    last_grp = tile_grp[jnp.maximum(ends[-1] // tm - 1, 0)]
    tile_grp = jnp.where(tile_valid == 1, tile_grp, last_grp)
    return tile_grp, tile_valid, src, dest


def _row_tile(n):
    return next(t for t in range(1024, 7, -8) if n % t == 0)


def _ffn(xa, wg, wu, wd, g, b, d_model, tm, row_sets):
    grp = xa[:, d_model + EPG].astype(jnp.int32)
    tile_grp, tile_valid, src, dest = _routing_plan(grp, tm)
    ys = _moe(tile_grp, tile_valid, src, xa, wg, wu, wd, g, b, d_model, tm)
    return [_unpermute(dest[rows], ys, rows.shape[0], _row_tile(rows.shape[0])) for rows in row_sets]


def _rope_tables(pos):
    def table(dim):
        half = dim // 2
        inv = ROPE_THETA ** (-jnp.arange(half, dtype=F32) / half)
        ang = pos.astype(F32)[:, None] * inv[None, :]
        cos = jnp.cos(ang)
        sin = jnp.sin(ang)
        reps = LANES // dim
        return jnp.tile(jnp.concatenate([cos, cos], axis=1), (1, reps)), \
            jnp.tile(jnp.concatenate([-sin, sin], axis=1), (1, reps))

    c128, s128 = table(HEAD_DIM)
    c64, s64 = table(IDX_DIM)
    return c128, s128, c64, s64


def _pad_rows(a, rows):
    return jnp.pad(a, [(0, 0), (0, rows - a.shape[1])] + [(0, 0)] * (a.ndim - 2))


def _head_rows(a, db, ds, heads):
    return a.reshape(db, ds, heads, HEAD_DIM).transpose(0, 2, 1, 3).reshape(db, heads * ds, HEAD_DIM)


def _token_rows(a, db, ds, heads):
    return a.reshape(db, heads, ds, HEAD_DIM).transpose(0, 2, 1, 3).reshape(db * ds, heads * HEAD_DIM)


def _new_block(a, db):
    slots = a.shape[2]
    return _pad_rows(a, LANES // slots).reshape(db, LANES, HEAD_DIM)


def kernel(x_prompt, x_sample, cache_l0_a_kv, cache_l0_idx_k, cache_l0_b_kv, cache_l1_c_kv, cache_l1_logf,
           page_table, meta_tokens, w_in_l0, w_out_l0, w_in_l1, b_forget_l1, w_out_l1, ln_mix_g, ln_mix_b,
           ln_ffn_g, ln_ffn_b, w_router, b_router, w_gate, w_up, w_down):
    n_batch, seq, d_model = x_prompt.shape
    db, ds, _ = x_sample.shape
    assert ds <= 8 and d_model % LANES == 0
    t_len = seq + N_META
    tp = _cdiv(t_len, BLK) * BLK
    n_prompt = n_batch * tp
    n_sample = db * ds
    nf = n_prompt + n_sample
    n_pool = cache_l0_a_kv.shape[0]
    n_pages = page_table.shape[1]
    past = n_pages * BLK
    topk_prompt = min(TOPK_MAX, seq // 4)
    topk_sample = min(TOPK_MAX, (past + ds) // 4)
    tm_proj = 1072 if nf % 1072 == 0 else 128
    tm_out = 256
    tm_moe = 512
    all_rows = jnp.arange(nf, dtype=jnp.int32)
    prompt_out_rows = (jnp.arange(n_batch, dtype=jnp.int32)[:, None] * tp + N_META
                       + jnp.arange(seq, dtype=jnp.int32)[None, :]).reshape(-1)

    meta = jnp.broadcast_to(meta_tokens[None], (n_batch, N_META, d_model)).astype(x_prompt.dtype)
    hp = _pad_rows(jnp.concatenate([meta, x_prompt], axis=1), tp)
    x0 = jnp.concatenate([hp.reshape(n_prompt, d_model), x_sample.reshape(n_sample, d_model)], axis=0)
    pos = jnp.concatenate([jnp.tile(jnp.arange(tp), n_batch), jnp.tile(past + jnp.arange(ds), db)])
    tables = _rope_tables(pos)

    cuts = np.cumsum((0, H_A * HEAD_DIM, HKV_A * HEAD_DIM, HKV_A * HEAD_DIM, H_B * HEAD_DIM, H_B * HEAD_DIM,
                      H_B * HEAD_DIM, H_IDX * IDX_DIM, IDX_DIM, H_IDX))
    seg = [w_in_l0[:, cuts[i]:cuts[i + 1]] for i in range(9)]
    qa_w, ka_w, va_w, qb_w, kb_w, vb_w, iq_w, ik_w, iw_w = seg
    zeros = lambda n: jnp.zeros((d_model, n), w_in_l0.dtype)
    w0 = jnp.concatenate([qa_w, iq_w, ka_w, va_w, qb_w, kb_w, vb_w, ik_w, zeros(LANES - IDX_DIM),
                          iw_w, zeros(LANES - H_IDX)], axis=1).astype(BF16)
    modes0 = jnp.asarray([1] * 4 + [2] * 4 + [1] * 2 + [0] * 14 + [3], jnp.int32)
    n_qkv = 3 * H_C * HEAD_DIM
    w1 = jnp.concatenate([w_in_l1[:, :n_qkv], w_in_l1[:, n_qkv:], zeros(MODE_COLS - H_C)], axis=1).astype(BF16)
    bf_row = jnp.zeros((1, LANES), F32).at[0, :H_C].set(b_forget_l1)
    w_out0 = w_out_l0.astype(BF16)
    w_out1 = w_out_l1.astype(BF16)
    wr_perm = w_router.reshape(d_model, N_GROUPS, EPG).transpose(0, 2, 1).reshape(d_model, N_EXPERTS)
    wr_hi = wr_perm.astype(BF16)
    wr_lo = (wr_perm - wr_hi.astype(F32)).astype(BF16)
    wr = jnp.concatenate([wr_hi, wr_lo, jnp.zeros((d_model, LANES - 2 * N_EXPERTS), BF16)], axis=1)
    br = jnp.zeros((EPG, LANES), F32).at[:, :N_GROUPS].set(b_router.reshape(N_GROUPS, EPG).T)
    wg, wu, wd = w_gate, w_up, w_down
    row = lambda v: v.reshape(1, d_model)

    p0 = _inproj(x0, w0, d_model, tm_proj, modes0, tables)
    col = lambda blk0, n: slice(blk0 * LANES, (blk0 + n) * LANES)
    ps = p0[n_prompt:]

    bias_p = _idx_prompt(p0, n_batch, tp, topk_prompt)
    oa_p = _attn_a_prompt(p0, bias_p, n_batch, tp)
    ob_p = _attn_b_prompt(p0, n_batch, tp)

    grp_q = H_A // HKV_A
    iq_th = jnp.tile(ps[:, col(8, 8)].reshape(db, ds * H_IDX, IDX_DIM), (1, grp_q, 1))
    iw_th = jnp.tile(ps[:, 49 * LANES:49 * LANES + H_IDX].reshape(db, ds * H_IDX, 1), (1, grp_q, 1))
    ik_s = ps[:, 48 * LANES:48 * LANES + IDX_DIM].reshape(db, ds, IDX_DIM)
    bias_s = _idx_sample(page_table, iq_th, iw_th, _pad_rows(ik_s, BLK), cache_l0_idx_k, topk_sample, ds)

    ka_s = ps[:, col(16, 4)].reshape(db, ds, HKV_A, HEAD_DIM)
    va_s = ps[:, col(20, 4)].reshape(db, ds, HKV_A, HEAD_DIM)
    kv_a_s = jnp.stack([ka_s, va_s], axis=2)
    oa_s = _attn_a_sample(page_table, _head_rows(ps[:, col(0, 8)], db, ds, H_A), bias_s,
                          _new_block(kv_a_s.reshape(db, ds, 2 * HKV_A, HEAD_DIM), db),
                          cache_l0_a_kv.reshape(n_pool, BLK * 2 * HKV_A, HEAD_DIM), ds)
    oa_s = _token_rows(oa_s, db, ds, H_A)

    kb_s = ps[:, col(32, 8)].reshape(db, ds, H_B, HEAD_DIM)
    vb_s = ps[:, col(40, 8)].reshape(db, ds, H_B, HEAD_DIM)
    kv_b_s = jnp.stack([kb_s, vb_s], axis=2)
    ob_s = _attn_b_sample(page_table, _head_rows(ps[:, col(24, 8)], db, ds, H_B),
                          _new_block(kb_s, db), _new_block(vb_s, db), cache_l0_b_kv, ds)
    ob_s = _token_rows(ob_s, db, ds, H_B)

    xa1 = _outproj(oa_p, ob_p, oa_s, ob_s, 0, w_out0, x0, row(ln_mix_g[0]), row(ln_mix_b[0]), wr, br, d_model,
                   tm_out)
    x1, = _ffn(xa1, wg[0], wu[0], wd[0], row(ln_ffn_g[0]), row(ln_ffn_b[0]), d_model, tm_moe, [all_rows])

    p1 = _inproj(x1, w1, d_model, tm_proj)
    logf, cum = _logf(p1, bf_row, nf, tp // BLK)
    cum_t = cum[:n_prompt, :H_C].reshape(n_batch, tp, H_C).transpose(0, 2, 1)
    oc_p = _attn_c_prompt(p1, cum_t[..., None], cum_t[:, :, None, :], n_batch, tp)

    ps1 = p1[n_prompt:]
    kc_s = ps1[:, col(16, 16)].reshape(db, ds, H_C, HEAD_DIM)
    vc_s = ps1[:, col(32, 16)].reshape(db, ds, H_C, HEAD_DIM)
    kv_c_s = jnp.stack([kc_s, vc_s], axis=2)
    logf_s = logf[n_prompt:, :H_C].reshape(db, ds, H_C)
    lf_new = _pad_rows(logf_s, LANES // H_C).reshape(db, 1, LANES)
    n_lf = n_pool * H_C
    tn_lf = min(2048, _cdiv(n_lf, LANES) * LANES)
    n_lf_pad = _cdiv(n_lf, tn_lf) * tn_lf
    lf_t = jnp.pad(cache_l1_logf.astype(F32).transpose(1, 0, 2).reshape(BLK, n_lf), [(0, 0), (0, n_lf_pad - n_lf)])
    sfx_t, tot = _page_suffix(lf_t, tn_lf)
    sfx_flat = sfx_t[:, :n_lf].reshape(BLK, n_pool, H_C).transpose(1, 0, 2).reshape(n_pool, 1, BLK * H_C)
    tot_col = jnp.repeat(tot[0, :n_lf].reshape(n_pool, H_C), ds, axis=1)[..., None]
    oc_s = _attn_c_sample(page_table, _head_rows(ps1[:, col(0, 16)], db, ds, H_C), lf_new,
                          _new_block(kc_s, db), _new_block(vc_s, db), sfx_flat, tot_col, cache_l1_c_kv, ds)
    oc_s = _token_rows(oc_s, db, ds, H_C)

    xa2 = _outproj(oc_p, oc_p, oc_s, oc_s, 1, w_out1, x1, row(ln_mix_g[1]), row(ln_mix_b[1]), wr, br, d_model,
                   tm_out)
    y_prompt, y_sample = _ffn(xa2, wg[1], wu[1], wd[1], row(ln_ffn_g[1]), row(ln_ffn_b[1]), d_model, tm_moe,
                              [prompt_out_rows, all_rows[n_prompt:]])

    def prompt_rows(a, blk0, heads):
        return a[:n_prompt, col(blk0, heads)].reshape(n_batch, tp, heads, HEAD_DIM)[:, :t_len]

    y_prompt = y_prompt.reshape(n_batch, seq, d_model)
    y_sample = y_sample.reshape(db, ds, d_model)
    a_kv_p = jnp.stack([prompt_rows(p0, 16, HKV_A), prompt_rows(p0, 20, HKV_A)], axis=2)
    idx_k_p = p0[:n_prompt, 48 * LANES:48 * LANES + IDX_DIM].reshape(n_batch, tp, IDX_DIM)[:, :t_len]
    b_kv_p = jnp.stack([prompt_rows(p0, 32, H_B), prompt_rows(p0, 40, H_B)], axis=2)
    c_kv_p = jnp.stack([prompt_rows(p1, 16, H_C), prompt_rows(p1, 32, H_C)], axis=2)
    logf_p = logf[:n_prompt, :H_C].reshape(n_batch, tp, H_C)[:, :t_len]
    return (y_prompt, y_sample, a_kv_p, kv_a_s, idx_k_p, ik_s, b_kv_p, kv_b_s, c_kv_p, kv_c_s, logf_p, logf_s)
```

```python
import functools

import numpy as np
import jax
import jax.numpy as jnp
from jax import lax
from jax.experimental import pallas as pl
from jax.experimental.pallas import tpu as pltpu

HEAD_DIM = 128
H_A = 8
HKV_A = 4
H_B = 8
H_C = 16
H_IDX = 16
IDX_DIM = 64
TOPK_MAX = 256
N_META = 16
BLK = 128
ROPE_THETA = 10000.0
N_EXPERTS = 32
N_GROUPS = 8
EPG = N_EXPERTS // N_GROUPS
LN_EPS = 1e-5
DEPTH = 2
ALPHA = (2 * DEPTH) ** 0.25
NEG = -1e30
INT_MIN = -2 ** 31
KEY_NEG_INF = -2139095041
LANES = 128
TN = 1280
MODE_COLS = 256
XCOLS = 128
VMEM_LIMIT = 56 * 1024 * 1024
SCALE = HEAD_DIM ** -0.5

F32 = jnp.float32
BF16 = jnp.bfloat16
NT_DIMS = (((1,), (1,)), ((), ()))


def _cdiv(a, b):
    return (a + b - 1) // b


def _dot(a, b):
    return jnp.dot(a, b, preferred_element_type=F32)


def _dot_nt(a, b):
    return lax.dot_general(a, b, NT_DIMS, preferred_element_type=F32)


def _split2(x):
    hi = x.astype(BF16)
    lo = (x - hi.astype(F32)).astype(BF16)
    return hi, lo


def _split3(x):
    hi = x.astype(BF16)
    r = x - hi.astype(F32)
    mid = r.astype(BF16)
    lo = (r - mid.astype(F32)).astype(BF16)
    return hi, mid, lo


def _dot3(a_bf16, x):
    hi, mid, lo = _split3(x)
    return _dot(a_bf16, hi) + _dot(a_bf16, mid) + _dot(a_bf16, lo)


def _dot3_left(x, a_bf16):
    hi, mid, lo = _split3(x)
    return _dot(hi, a_bf16) + _dot(mid, a_bf16) + _dot(lo, a_bf16)


def _neg_softplus(z):
    return -(jnp.maximum(z, 0.0) + jnp.log1p(jnp.exp(-jnp.abs(z))))


def _neg_softplus_bulk(z):
    return -(jnp.maximum(z, 0.0) + jnp.log(1.0 + jnp.exp(-jnp.abs(z))))


def _lane_chunks(x):
    return [x[:, c * LANES:(c + 1) * LANES] for c in range(x.shape[1] // LANES)]


def _params(sem, vmem=VMEM_LIMIT):
    return pltpu.CompilerParams(dimension_semantics=sem, vmem_limit_bytes=vmem)


def _rope128(a, c, s):
    return a * c + pltpu.roll(a, 64, 1) * s


def _rope64(a, c, s):
    lane = lax.broadcasted_iota(jnp.int32, a.shape, 1)
    first = (lane % 64) < 32
    partner = jnp.where(first, pltpu.roll(a, 96, 1), pltpu.roll(a, 32, 1))
    return a * c + partner * s


def _inproj_rope_kernel(modes_ref, x_ref, w_ref, c128_ref, s128_ref, c64_ref, s64_ref, o_ref, xb_ref, *, tn):
    j = pl.program_id(1)

    @pl.when(j == 0)
    def _():
        xb_ref[...] = x_ref[...].astype(BF16)

    acc = _dot(xb_ref[...], w_ref[...])
    chunk = lambda c: slice(c * LANES, (c + 1) * LANES)
    per_tile = tn // MODE_COLS
    for t in range(per_tile):
        mode = modes_ref[j * per_tile + t]
        c0 = t * (MODE_COLS // LANES)
        span = slice(t * MODE_COLS, (t + 1) * MODE_COLS)

        @pl.when(mode == 0)
        def _():
            o_ref[:, span] = acc[:, span]

        @pl.when(mode == 1)
        def _():
            for c in range(c0, c0 + MODE_COLS // LANES):
                o_ref[:, chunk(c)] = _rope128(acc[:, chunk(c)], c128_ref[...], s128_ref[...])

        @pl.when(mode == 2)
        def _():
            for c in range(c0, c0 + MODE_COLS // LANES):
                o_ref[:, chunk(c)] = _rope64(acc[:, chunk(c)], c64_ref[...], s64_ref[...])

        @pl.when(mode == 3)
        def _():
            o_ref[:, chunk(c0)] = _rope64(acc[:, chunk(c0)], c64_ref[...], s64_ref[...])
            rest = slice((c0 + 1) * LANES, (t + 1) * MODE_COLS)
            o_ref[:, rest] = acc[:, rest]


def _inproj_plain_kernel(x_ref, w_ref, o_ref, xb_ref):
    @pl.when(pl.program_id(1) == 0)
    def _():
        xb_ref[...] = x_ref[...].astype(BF16)

    o_ref[...] = _dot(xb_ref[...], w_ref[...])


def _inproj(x, w_bf16, d_model, tm, modes=None, tables=None):
    nf = x.shape[0]
    ncols = w_bf16.shape[1]
    tn = TN if ncols % TN == 0 else MODE_COLS
    grid = (_cdiv(nf, tm), ncols // tn)
    scratch = [pltpu.VMEM((tm, d_model), BF16)]
    out_shape = jax.ShapeDtypeStruct((nf, ncols), F32)
    if modes is None:
        return pl.pallas_call(
            _inproj_plain_kernel,
            grid=grid,
            in_specs=[pl.BlockSpec((tm, d_model), lambda i, j: (i, 0)),
                      pl.BlockSpec((d_model, tn), lambda i, j: (0, j))],
            out_specs=pl.BlockSpec((tm, tn), lambda i, j: (i, j)),
            out_shape=out_shape,
            scratch_shapes=scratch,
            compiler_params=_params(("parallel", "arbitrary")),
            name="inproj_plain",
        )(x, w_bf16)
    tab_spec = pl.BlockSpec((tm, LANES), lambda i, j, m: (i, 0))
    return pl.pallas_call(
        functools.partial(_inproj_rope_kernel, tn=tn),
        grid_spec=pltpu.PrefetchScalarGridSpec(
            num_scalar_prefetch=1,
            grid=grid,
            in_specs=[pl.BlockSpec((tm, d_model), lambda i, j, m: (i, 0)),
                      pl.BlockSpec((d_model, tn), lambda i, j, m: (0, j)),
                      tab_spec, tab_spec, tab_spec, tab_spec],
            out_specs=pl.BlockSpec((tm, tn), lambda i, j, m: (i, j)),
            scratch_shapes=scratch),
        out_shape=out_shape,
        compiler_params=_params(("parallel", "arbitrary")),
        name="inproj_rope",
    )(modes, x, w_bf16, *tables)


def _sortable_key(score):
    score = jnp.where(score == 0.0, 0.0, score)
    bits = lax.bitcast_convert_type(score, jnp.int32)
    return bits ^ ((bits >> 31) & 0x7FFFFFFF)


def _count(mask):
    return jnp.sum(mask.astype(F32), axis=1, keepdims=True)


def _topk_select(key_ref, vis, s_pos, topk, n_cols, thr_ref, need_ref, jb_ref):
    kf = float(topk)
    c0 = _count(key_ref[...] >= 0)
    ans0 = jnp.where(c0 >= kf, 0, INT_MIN).astype(jnp.int32)

    def body(i, ans):
        cand = ans | jnp.left_shift(jnp.int32(1), 30 - i)
        cnt = _count(key_ref[...] >= cand)
        return jnp.where(cnt >= kf, cand, ans)

    thr = lax.fori_loop(0, 31, body, ans0)
    key = key_ref[...]
    need = kf - _count(key > thr)
    n_eq = _count((key == thr) & vis)
    thr_ref[...] = thr
    need_ref[...] = need
    jb_ref[...] = jnp.full(jb_ref.shape, n_cols, jnp.int32)
    n_bits = int(np.ceil(np.log2(n_cols))) + 1

    @pl.when(jnp.max(n_eq - need) > 0.0)
    def _():
        def body2(i, ans):
            cand = ans | jnp.left_shift(jnp.int32(1), n_bits - 1 - i)
            eqv = (key_ref[...] == thr_ref[...]) & vis
            c = _count(eqv & (s_pos < cand))
            return jnp.where(c < need_ref[...], cand, ans)

        jb_ref[...] = lax.fori_loop(0, n_bits, body2, jnp.zeros(jb_ref.shape, jnp.int32))

    return vis & ((key > thr) | ((key == thr) & (s_pos <= jb_ref[...])))


def _causal_extents(tp, n=8):
    nb = tp // BLK
    return [BLK * e for e in sorted({_cdiv(nb * (k + 1), n) for k in range(n)})]


def _for_causal_extent(qi, tp, body):
    lo = 0
    for extent in _causal_extents(tp):
        need = (qi + 1) * BLK
        pl.when((need > lo) & (need <= extent))(functools.partial(body, extent))
        lo = extent


def _idx_prompt_kernel(iq_ref, iw_ref, ik_ref, bias_ref, ikd_ref, key_ref, thr_ref, need_ref, jb_ref,
                       *, topk, tp):
    qi = pl.program_id(1)

    @pl.when(qi == 0)
    def _():
        ik = ik_ref[...]
        ikd_ref[...] = (ik + pltpu.roll(ik, 64, 1)).astype(BF16)

    def select(n_keys):
        lane = lax.broadcasted_iota(jnp.int32, (1, LANES), 1)
        keys = ikd_ref[:n_keys, :]
        score = jnp.zeros((BLK, n_keys), F32)
        for h in range(H_IDX):
            pair = iq_ref[:, (h // 2) * LANES:(h // 2 + 1) * LANES]
            lo = (h % 2) * IDX_DIM
            qh = jnp.where((lane >= lo) & (lane < lo + IDX_DIM), pair, 0.0).astype(BF16)
            w = iw_ref[:, h:h + 1] * (H_IDX ** -0.5 * IDX_DIM ** -0.5)
            score = score + jnp.maximum(_dot_nt(qh, keys), 0.0) * w

        t_pos = qi * BLK + lax.broadcasted_iota(jnp.int32, (BLK, 1), 0)
        s_pos = lax.broadcasted_iota(jnp.int32, (1, n_keys), 1)
        vis = s_pos <= t_pos
        keys_view = key_ref.at[:, :n_keys]
        keys_view[...] = jnp.where(vis, _sortable_key(score), KEY_NEG_INF)
        sel = _topk_select(keys_view, vis, s_pos, topk, n_keys, thr_ref, need_ref, jb_ref)
        bias_ref[:, :n_keys] = jnp.where(sel, 0.0, NEG).astype(BF16)
        if n_keys < tp:
            bias_ref[:, n_keys:] = jnp.full((BLK, tp - n_keys), NEG, BF16)

    _for_causal_extent(qi, tp, select)


def _idx_prompt(p0, n_batch, tp, topk):
    nqb = tp // BLK
    return pl.pallas_call(
        functools.partial(_idx_prompt_kernel, topk=topk, tp=tp),
        grid=(n_batch, nqb),
        in_specs=[pl.BlockSpec((BLK, 1024), lambda b, q: (b * nqb + q, 1)),
                  pl.BlockSpec((BLK, LANES), lambda b, q: (b * nqb + q, 49)),
                  pl.BlockSpec((tp, LANES), lambda b, q: (b, 48))],
        out_specs=pl.BlockSpec((BLK, tp), lambda b, q: (b * nqb + q, 0)),
        out_shape=jax.ShapeDtypeStruct((n_batch * tp, tp), BF16),
        scratch_shapes=[pltpu.VMEM((tp, LANES), BF16),
                        pltpu.VMEM((BLK, tp), jnp.int32),
                        pltpu.VMEM((BLK, 1), jnp.int32),
                        pltpu.VMEM((BLK, 1), F32),
                        pltpu.VMEM((BLK, 1), jnp.int32)],
        compiler_params=_params(("parallel", "arbitrary")),
        name="dsa_index_prompt",
    )(p0, p0, p0)


def _attn_a_prompt_kernel(q_ref, k_ref, v_ref, bias_ref, o_ref, kb_ref, vb_ref, *, tp):
    qi = pl.program_id(2)

    @pl.when(qi == 0)
    def _():
        kb_ref[...] = k_ref[...].astype(BF16)
        vb_ref[...] = v_ref[...].astype(BF16)

    def attend(n_keys):
        bias = bias_ref[:, :n_keys].astype(F32)
        for g in range(H_A // HKV_A):
            sl = slice(g * HEAD_DIM, (g + 1) * HEAD_DIM)
            s = _dot_nt(q_ref[:, sl].astype(BF16), kb_ref[:n_keys, :]) * SCALE + bias
            m = jnp.max(s, axis=1, keepdims=True)
            p = jnp.exp(s - m)
            l = jnp.sum(p, axis=1, keepdims=True)
            o_ref[:, sl] = _dot(p.astype(BF16), vb_ref[:n_keys, :]) / l

    _for_causal_extent(qi, tp, attend)


def _attn_a_prompt(p0, bias, n_batch, tp):
    nqb = tp // BLK
    gw = (H_A // HKV_A) * HEAD_DIM
    return pl.pallas_call(
        functools.partial(_attn_a_prompt_kernel, tp=tp),
        grid=(n_batch, HKV_A, nqb),
        in_specs=[pl.BlockSpec((BLK, gw), lambda b, k, q: (b * nqb + q, k)),
                  pl.BlockSpec((tp, HEAD_DIM), lambda b, k, q: (b, 16 + k)),
                  pl.BlockSpec((tp, HEAD_DIM), lambda b, k, q: (b, 20 + k)),
                  pl.BlockSpec((BLK, tp), lambda b, k, q: (b * nqb + q, 0))],
        out_specs=pl.BlockSpec((BLK, gw), lambda b, k, q: (b * nqb + q, k)),
        out_shape=jax.ShapeDtypeStruct((n_batch * tp, H_A * HEAD_DIM), F32),
        scratch_shapes=[pltpu.VMEM((tp, HEAD_DIM), BF16), pltpu.VMEM((tp, HEAD_DIM), BF16)],
        compiler_params=_params(("parallel", "parallel", "arbitrary")),
        name="dsa_attend_prompt",
    )(p0, p0, p0, bias)


def _suffix_and_ones(group):
    r = lax.broadcasted_iota(jnp.int32, (LANES, 2 * LANES), 0)
    c = lax.broadcasted_iota(jnp.int32, (LANES, 2 * LANES), 1)
    return ((c >= LANES) | (r // group > c // group)).astype(BF16)


def _pick_tq(tp):
    return 384 if tp % 384 == 0 else BLK


def _stick_weights(z, strict_fn, tail, sums):
    rows = z.shape[0]
    pieces = _lane_chunks(z)
    n = len(pieces)
    lsn = [_neg_softplus_bulk(p) for p in pieces]
    ok = [strict_fn(i) for i in range(n)]
    keep = lambda i, x: x if ok[i] is None else jnp.where(ok[i], x, 0.0)
    hi, lo = _split2(jnp.concatenate([keep(i, lsn[i]) for i in range(n)], axis=0))
    ar = _dot(hi, sums) + _dot(lo, sums)
    w = [None] * n
    for i in reversed(range(n)):
        blk = ar[i * rows:(i + 1) * rows]
        w[i] = keep(i, jnp.exp(pieces[i] + lsn[i] + tail + blk[:, :LANES]))
        tail = tail + blk[:, LANES:]
    return jnp.concatenate(w, axis=1), tail


def _attn_b_prompt_kernel(q_ref, k_ref, v_ref, o_ref, kb_ref, vb_ref, acc_ref, carry_ref, *, tq):
    qi = pl.program_id(2)

    @pl.when(qi == 0)
    def _():
        kb_ref[...] = k_ref[...].astype(BF16)
        vb_ref[...] = v_ref[...].astype(BF16)

    q = q_ref[...].astype(BF16)
    acc_ref[...] = jnp.zeros_like(acc_ref)
    carry_ref[...] = jnp.zeros_like(carry_ref)
    r = lax.broadcasted_iota(jnp.int32, (tq, BLK), 0)
    c = lax.broadcasted_iota(jnp.int32, (tq, BLK), 1)
    sums = _suffix_and_ones(1)

    def block(kb, strict_fn):
        off = pl.multiple_of(kb * tq, BLK)
        z = _dot_nt(q, kb_ref[pl.ds(off, tq), :]) * SCALE
        w, tail = _stick_weights(z, strict_fn, carry_ref[...], sums)
        acc_ref[...] += _dot(w.astype(BF16), vb_ref[pl.ds(off, tq), :])
        carry_ref[...] = tail

    block(qi, lambda i: (i * BLK + c) < r)

    def body(it, carry_unused):
        block(qi - 1 - it, lambda i: None)
        return carry_unused

    lax.fori_loop(0, qi, body, 0)
    o_ref[...] = acc_ref[...]


def _attn_b_prompt(p0, n_batch, tp):
    tq = _pick_tq(tp)
    nqb = tp // tq
    return pl.pallas_call(
        functools.partial(_attn_b_prompt_kernel, tq=tq),
        grid=(n_batch, H_B, nqb),
        in_specs=[pl.BlockSpec((tq, HEAD_DIM), lambda b, h, q: (b * nqb + q, 24 + h)),
                  pl.BlockSpec((tp, HEAD_DIM), lambda b, h, q: (b, 32 + h)),
                  pl.BlockSpec((tp, HEAD_DIM), lambda b, h, q: (b, 40 + h))],
        out_specs=pl.BlockSpec((tq, HEAD_DIM), lambda b, h, q: (b * nqb + q, h)),
        out_shape=jax.ShapeDtypeStruct((n_batch * tp, H_B * HEAD_DIM), F32),
        scratch_shapes=[pltpu.VMEM((tp, HEAD_DIM), BF16), pltpu.VMEM((tp, HEAD_DIM), BF16),
                        pltpu.VMEM((tq, HEAD_DIM), F32), pltpu.VMEM((tq, LANES), F32)],
        compiler_params=_params(("parallel", "parallel", "arbitrary")),
        name="stickbreak_prompt",
    )(p0, p0, p0)


def _logf_kernel(f_ref, bf_ref, logf_ref, cum_ref, carry_ref, *, blocks_per_seq):
    i = pl.program_id(0)

    @pl.when(i % blocks_per_seq == 0)
    def _():
        carry_ref[...] = jnp.zeros_like(carry_ref)

    x = f_ref[...] + bf_ref[...]
    logf = _neg_softplus(-x)
    logf_ref[...] = logf
    r = lax.broadcasted_iota(jnp.int32, (BLK, BLK), 0)
    c = lax.broadcasted_iota(jnp.int32, (BLK, BLK), 1)
    cum = carry_ref[...] + _dot3((c <= r).astype(BF16), logf)
    cum_ref[...] = cum
    carry_ref[...] = cum[BLK - 1:BLK, :]


def _logf(p1, bf_row, n_rows, blocks_per_seq):
    return pl.pallas_call(
        functools.partial(_logf_kernel, blocks_per_seq=blocks_per_seq),
        grid=(_cdiv(n_rows, BLK),),
        in_specs=[pl.BlockSpec((BLK, LANES), lambda i: (i, 48)),
                  pl.BlockSpec((1, LANES), lambda i: (0, 0))],
        out_specs=[pl.BlockSpec((BLK, LANES), lambda i: (i, 0)),
                   pl.BlockSpec((BLK, LANES), lambda i: (i, 0))],
        out_shape=[jax.ShapeDtypeStruct((n_rows, LANES), F32),
                   jax.ShapeDtypeStruct((n_rows, LANES), F32)],
        scratch_shapes=[pltpu.VMEM((1, LANES), F32)],
        compiler_params=_params(("arbitrary",)),
        name="log_forget_cumsum",
    )(p1, bf_row)


HPS = 2


def _attn_c_prompt_kernel(q_ref, k_ref, v_ref, cq_ref, ck_ref, o_ref, kb_ref, vb_ref, s_ref, m_ref, l_ref,
                          acc_ref, *, tq):
    qi = pl.program_id(2)

    @pl.when(qi == 0)
    def _():
        kb_ref[...] = k_ref[...].astype(BF16)
        vb_ref[...] = v_ref[...].astype(BF16)

    head = lambda x, hh: x[:, hh * HEAD_DIM:(hh + 1) * HEAD_DIM]
    q = [head(q_ref, hh).astype(BF16) for hh in range(HPS)]
    cq = [cq_ref[0, hh] for hh in range(HPS)]
    m_ref[...] = jnp.full(m_ref.shape, NEG, F32)

    def logits(kb, hh):
        off = pl.multiple_of(kb * tq, LANES)
        ck = ck_ref[0, hh, :, pl.ds(off, tq)]
        keys = kb_ref[pl.ds(off, tq), hh * HEAD_DIM:(hh + 1) * HEAD_DIM]
        return off, _dot_nt(q[hh], keys) * SCALE + (cq[hh] - ck)

    def keep(off, s, hh):
        s_ref[hh, :, pl.ds(off, tq)] = s
        m = m_ref[hh]
        for piece in _lane_chunks(s):
            m = jnp.maximum(m, piece)
        m_ref[hh] = m

    def pass1(kb, carry_unused):
        for hh in range(HPS):
            keep(*logits(kb, hh), hh)
        return carry_unused

    lax.fori_loop(0, qi, pass1, 0)
    r = lax.broadcasted_iota(jnp.int32, (tq, tq), 0)
    c = lax.broadcasted_iota(jnp.int32, (tq, tq), 1)
    for hh in range(HPS):
        off, s = logits(qi, hh)
        keep(off, jnp.where(c <= r, s, NEG), hh)

    m = [jnp.max(m_ref[hh], axis=1, keepdims=True) for hh in range(HPS)]
    l_ref[...] = jnp.zeros_like(l_ref)
    acc_ref[...] = jnp.zeros_like(acc_ref)

    def pass2(kb, carry_unused):
        off = pl.multiple_of(kb * tq, LANES)
        for hh in range(HPS):
            p = jnp.exp(s_ref[hh, :, pl.ds(off, tq)] - m[hh])
            l = l_ref[hh]
            for piece in _lane_chunks(p):
                l = l + piece
            l_ref[hh] = l
            sl = slice(hh * HEAD_DIM, (hh + 1) * HEAD_DIM)
            acc_ref[:, sl] += _dot(p.astype(BF16), vb_ref[pl.ds(off, tq), sl])
        return carry_unused

    lax.fori_loop(0, qi + 1, pass2, 0)
    for hh in range(HPS):
        sl = slice(hh * HEAD_DIM, (hh + 1) * HEAD_DIM)
        o_ref[:, sl] = acc_ref[:, sl] / jnp.sum(l_ref[hh], axis=1, keepdims=True)


def _attn_c_prompt(p1, cum_col, cum_row, n_batch, tp):
    tq = _pick_tq(tp)
    nqb = tp // tq
    wide = HPS * HEAD_DIM
    k0 = H_C // HPS
    return pl.pallas_call(
        functools.partial(_attn_c_prompt_kernel, tq=tq),
        grid=(n_batch, H_C // HPS, nqb),
        in_specs=[pl.BlockSpec((tq, wide), lambda b, h, q: (b * nqb + q, h)),
                  pl.BlockSpec((tp, wide), lambda b, h, q: (b, k0 + h)),
                  pl.BlockSpec((tp, wide), lambda b, h, q: (b, 2 * k0 + h)),
                  pl.BlockSpec((1, HPS, tq, 1), lambda b, h, q: (b, h, q, 0)),
                  pl.BlockSpec((1, HPS, 1, tp), lambda b, h, q: (b, h, 0, 0))],
        out_specs=pl.BlockSpec((tq, wide), lambda b, h, q: (b * nqb + q, h)),
        out_shape=jax.ShapeDtypeStruct((n_batch * tp, H_C * HEAD_DIM), F32),
        scratch_shapes=[pltpu.VMEM((tp, wide), BF16), pltpu.VMEM((tp, wide), BF16),
                        pltpu.VMEM((HPS, tq, tp), F32), pltpu.VMEM((HPS, tq, LANES), F32),
                        pltpu.VMEM((HPS, tq, LANES), F32), pltpu.VMEM((tq, wide), F32)],
        compiler_params=_params(("parallel", "parallel", "arbitrary")),
        name="forget_attend_prompt",
    )(p1, p1, p1, cum_col, cum_row)


def _pages_per_step(n_pages, want):
    while n_pages % want:
        want //= 2
    return want


def _idx_sample_kernel(pt_ref, iq_ref, iw_ref, ikn_ref, *rest, topk, n_pages, pps, ds):
    page_refs = rest[:pps]
    bias_ref, score_ref, key_ref, thr_ref, need_ref, jb_ref = rest[pps:]
    j = pl.program_id(1)
    past = n_pages * BLK
    ncol = past + BLK
    nq = iq_ref.shape[1] // H_IDX
    q = iq_ref[0].astype(BF16)
    wgt = iw_ref[0] * (H_IDX ** -0.5)

    def scores(keys):
        dots = _dot_nt(q, keys.astype(BF16)) * (IDX_DIM ** -0.5)
        wd = jnp.maximum(dots, 0.0) * wgt
        return jnp.sum(wd.reshape(nq, H_IDX, BLK), axis=1)

    for i, page_ref in enumerate(page_refs):
        score_ref[:, pl.ds(pl.multiple_of((j * pps + i) * BLK, BLK), BLK)] = scores(page_ref[0])

    @pl.when(j == n_pages // pps - 1)
    def _():
        score_ref[:, past:] = scores(ikn_ref[0])
        t_idx = lax.broadcasted_iota(jnp.int32, (nq, 1), 0) % ds
        s_pos = lax.broadcasted_iota(jnp.int32, (1, ncol), 1)
        vis = s_pos <= past + t_idx
        key_ref[...] = jnp.where(vis, _sortable_key(score_ref[...]), KEY_NEG_INF)
        sel = _topk_select(key_ref, vis, s_pos, topk, ncol, thr_ref, need_ref, jb_ref)
        bias_ref[0] = jnp.where(sel, 0.0, NEG)


def _idx_sample(page_table, iq_th, iw_th, ik_new, cache_idx, topk, ds):
    db, n_pages = page_table.shape
    ncol = n_pages * BLK + BLK
    rows = iq_th.shape[1]
    nq = rows // H_IDX
    pps = _pages_per_step(n_pages, 16)

    def page_spec(i):
        return pl.BlockSpec((1, BLK, IDX_DIM), lambda b, j, pt: (pt[b, j * pps + i], 0, 0))

    return pl.pallas_call(
        functools.partial(_idx_sample_kernel, topk=topk, n_pages=n_pages, pps=pps, ds=ds),
        grid_spec=pltpu.PrefetchScalarGridSpec(
            num_scalar_prefetch=1,
            grid=(db, n_pages // pps),
            in_specs=[pl.BlockSpec((1, rows, IDX_DIM), lambda b, j, pt: (b, 0, 0)),
                      pl.BlockSpec((1, rows, 1), lambda b, j, pt: (b, 0, 0)),
                      pl.BlockSpec((1, BLK, IDX_DIM), lambda b, j, pt: (b, 0, 0))]
                     + [page_spec(i) for i in range(pps)],
            out_specs=pl.BlockSpec((1, nq, ncol), lambda b, j, pt: (b, 0, 0)),
            scratch_shapes=[pltpu.VMEM((nq, ncol), F32),
                            pltpu.VMEM((nq, ncol), jnp.int32),
                            pltpu.VMEM((nq, 1), jnp.int32),
                            pltpu.VMEM((nq, 1), F32),
                            pltpu.VMEM((nq, 1), jnp.int32)]),
        out_shape=jax.ShapeDtypeStruct((db, nq, ncol), F32),
        compiler_params=_params(("parallel", "arbitrary")),
        name="dsa_index_sample",
    )(page_table, iq_th, iw_th, ik_new, *([cache_idx] * pps))


def _softmax_update(s, pv_fn, m_ref, l_ref, acc_ref):
    m_old = m_ref[...]
    m_new = jnp.maximum(m_old, jnp.max(s, axis=1, keepdims=True))
    corr = jnp.exp(m_old - m_new)
    p = jnp.exp(s - m_new)
    l_ref[...] = l_ref[...] * corr + jnp.sum(p, axis=1, keepdims=True)
    acc_ref[...] = acc_ref[...] * corr + pv_fn(p)
    m_ref[...] = m_new


def _head_match(n_rows, heads_per_lane_group, rows_per_head):
    row = lax.broadcasted_iota(jnp.int32, (n_rows, LANES), 0)
    lane = lax.broadcasted_iota(jnp.int32, (n_rows, LANES), 1)
    return (lane % heads_per_lane_group) == (row // rows_per_head), lane, row


def _attn_a_sample_kernel(pt_ref, q_ref, bias_ref, new_ref, *rest, n_steps, pps, ds):
    page_refs = rest[:pps]
    o_ref, rep_ref, m_ref, l_ref, acc_ref = rest[pps:]
    b = pl.program_id(0)
    j = pl.program_id(1)
    n_slots = 2 * HKV_A
    n_rows = H_A * ds
    n_tile = n_rows // bias_ref.shape[1]

    @pl.when((b == 0) & (j == 0))
    def _():
        s_i = lax.broadcasted_iota(jnp.int32, rep_ref.shape, 0)
        c_i = lax.broadcasted_iota(jnp.int32, rep_ref.shape, 1)
        rep_ref[...] = (c_i // n_slots == s_i).astype(BF16)

    @pl.when(j == 0)
    def _():
        m_ref[...] = jnp.full(m_ref.shape, NEG, F32)
        l_ref[...] = jnp.zeros_like(l_ref)
        acc_ref[...] = jnp.zeros_like(acc_ref)

    q = q_ref[0].astype(BF16)
    match, _, _ = _head_match(n_rows, n_slots, ds * (H_A // HKV_A))

    def attend(blocks):
        pgs = [rows.astype(BF16) for rows, _ in blocks]
        pieces = []
        for pg, (_, picked) in zip(pgs, blocks):
            cols = pg.shape[0]
            pick = jnp.concatenate([picked.astype(BF16)] * n_tile, axis=0)
            pick = _dot(pick, rep_ref[:, :cols])
            pieces += [jnp.where(match & (pk > 0.5), sc, NEG)
                       for sc, pk in zip(_lane_chunks(_dot_nt(q, pg) * SCALE), _lane_chunks(pick))]

        def pv(p):
            out, at = 0.0, 0
            for pg in pgs:
                part = p[:, at:at + pg.shape[0]]
                moved = jnp.concatenate([pltpu.roll(x, HKV_A, 1) for x in _lane_chunks(part)], axis=1)
                out = out + _dot(moved.astype(BF16), pg)
                at += pg.shape[0]
            return out

        _softmax_update(jnp.concatenate(pieces, axis=1), pv, m_ref, l_ref, acc_ref)

    def picked(i):
        return jnp.where(bias_ref[0][:, i * BLK:(i + 1) * BLK] == 0.0, 1.0, 0.0)

    @pl.when(j < n_steps)
    def _():
        attend([(page_ref[0], picked(i)) for i, page_ref in enumerate(page_refs)])

    @pl.when(j == n_steps)
    def _():
        attend([(new_ref[0], picked(0))])
        o_ref[0] = acc_ref[...] / l_ref[...]


def _attn_a_sample(page_table, q_rows, bias, new_rows, cache_flat, ds):
    db, n_pages = page_table.shape
    n_slots = 2 * HKV_A
    n_rows = H_A * ds
    pps = _pages_per_step(n_pages, 16)
    n_steps = n_pages // pps
    last = n_steps - 1

    def page_spec(i):
        return pl.BlockSpec((1, BLK * n_slots, HEAD_DIM),
                            lambda b, j, pt: (pt[b, jnp.minimum(j, last) * pps + i], 0, 0))

    return pl.pallas_call(
        functools.partial(_attn_a_sample_kernel, n_steps=n_steps, pps=pps, ds=ds),
        grid_spec=pltpu.PrefetchScalarGridSpec(
            num_scalar_prefetch=1,
            grid=(db, n_steps + 1),
            in_specs=[pl.BlockSpec((1, n_rows, HEAD_DIM), lambda b, j, pt: (b, 0, 0)),
                      pl.BlockSpec((1, bias.shape[1], BLK * pps), lambda b, j, pt: (b, 0, j)),
                      pl.BlockSpec((1, LANES, HEAD_DIM), lambda b, j, pt: (b, 0, 0))]
                     + [page_spec(i) for i in range(pps)],
            out_specs=pl.BlockSpec((1, n_rows, HEAD_DIM), lambda b, j, pt: (b, 0, 0)),
            scratch_shapes=[pltpu.VMEM((BLK, BLK * n_slots), BF16),
                            pltpu.VMEM((n_rows, 1), F32), pltpu.VMEM((n_rows, 1), F32),
                            pltpu.VMEM((n_rows, HEAD_DIM), F32)]),
        out_shape=jax.ShapeDtypeStruct((db, n_rows, HEAD_DIM), F32),
        compiler_params=_params(("arbitrary", "arbitrary")),
        name="dsa_attend_sample",
    )(page_table, q_rows, bias, new_rows, *([cache_flat] * pps))


def _attn_b_sample_kernel(pt_ref, q_ref, knew_ref, vnew_ref, *rest, n_steps, pps, ds):
    page_refs = rest[:pps]
    o_ref, acc_ref, carry_ref = rest[pps:]
    j = pl.program_id(1)
    n_rows = H_B * ds
    q = q_ref[0].astype(BF16)
    match, lane, row = _head_match(n_rows, H_B, ds)
    sums = _suffix_and_ones(H_B)

    def attend(blocks, strict_fn):
        z = jnp.concatenate([_dot_nt(q, k.astype(BF16)) * SCALE for k, _ in blocks], axis=1)
        w, tail = _stick_weights(z, strict_fn, carry_ref[...], sums)
        carry_ref[...] = tail
        out, at = 0.0, 0
        for k, v in blocks:
            out = out + _dot(w[:, at:at + k.shape[0]].astype(BF16), v.astype(BF16))
            at += k.shape[0]
        acc_ref[...] += out

    @pl.when(j == 0)
    def _():
        acc_ref[...] = jnp.zeros_like(acc_ref)
        carry_ref[...] = jnp.zeros_like(carry_ref)
        s_new = lane // H_B
        attend([(knew_ref[0], vnew_ref[0])], lambda i: match & (s_new < row % ds))

    @pl.when(j > 0)
    def _():
        attend([(page_ref[0, :, 0].reshape(BLK * H_B, HEAD_DIM), page_ref[0, :, 1].reshape(BLK * H_B, HEAD_DIM))
                for page_ref in reversed(page_refs)], lambda i: match)

    @pl.when(j == n_steps)
    def _():
        o_ref[0] = acc_ref[...]


def _attn_b_sample(page_table, q_rows, k_new, v_new, cache, ds):
    db, n_pages = page_table.shape
    n_rows = H_B * ds
    pps = _pages_per_step(n_pages, 8)
    n_steps = n_pages // pps

    def page_spec(i):
        return pl.BlockSpec((1, BLK, 2, H_B, HEAD_DIM),
                            lambda b, j, pt: (pt[b, n_pages - 1 - (jnp.maximum(j, 1) - 1) * pps - i], 0, 0, 0, 0))

    new_spec = pl.BlockSpec((1, LANES, HEAD_DIM), lambda b, j, pt: (b, 0, 0))
    return pl.pallas_call(
        functools.partial(_attn_b_sample_kernel, n_steps=n_steps, pps=pps, ds=ds),
        grid_spec=pltpu.PrefetchScalarGridSpec(
            num_scalar_prefetch=1,
            grid=(db, n_steps + 1),
            in_specs=[pl.BlockSpec((1, n_rows, HEAD_DIM), lambda b, j, pt: (b, 0, 0)), new_spec, new_spec]
                     + [page_spec(i) for i in range(pps)],
            out_specs=pl.BlockSpec((1, n_rows, HEAD_DIM), lambda b, j, pt: (b, 0, 0)),
            scratch_shapes=[pltpu.VMEM((n_rows, HEAD_DIM), F32), pltpu.VMEM((n_rows, LANES), F32)]),
        out_shape=jax.ShapeDtypeStruct((db, n_rows, HEAD_DIM), F32),
        compiler_params=_params(("parallel", "arbitrary")),
        name="stickbreak_sample",
    )(page_table, q_rows, k_new, v_new, *([cache] * pps))


def _page_suffix_kernel(lf_ref, sfx_ref, tot_ref):
    x = lf_ref[...]
    r = lax.broadcasted_iota(jnp.int32, (BLK, BLK), 0)
    c = lax.broadcasted_iota(jnp.int32, (BLK, BLK), 1)
    sfx_ref[...] = _dot3((c > r).astype(BF16), x)
    tot_ref[...] = _dot3(jnp.ones((8, BLK), BF16), x)


def _page_suffix(lf_t, tn):
    n = lf_t.shape[1]
    return pl.pallas_call(
        _page_suffix_kernel,
        grid=(n // tn,),
        in_specs=[pl.BlockSpec((BLK, tn), lambda i: (0, i))],
        out_specs=[pl.BlockSpec((BLK, tn), lambda i: (0, i)), pl.BlockSpec((8, tn), lambda i: (0, i))],
        out_shape=[jax.ShapeDtypeStruct((BLK, n), F32), jax.ShapeDtypeStruct((8, n), F32)],
        compiler_params=_params(("parallel",)),
        name="log_forget_page_suffix",
    )(lf_t)


def _attn_c_sample_kernel(pt_ref, q_ref, lfnew_ref, knew_ref, vnew_ref, *rest, n_steps, pps, ds):
    sfx_refs, tot_refs, page_refs = rest[:pps], rest[pps:2 * pps], rest[2 * pps:3 * pps]
    o_ref, m_ref, l_ref, acc_ref, carry_ref, cn_ref = rest[3 * pps:]
    j = pl.program_id(1)
    n_rows = H_C * ds
    q = q_ref[0].astype(BF16)
    match, lane, row = _head_match(n_rows, H_C, ds)

    @pl.when(j == 0)
    def _():
        m_ref[...] = jnp.full(m_ref.shape, NEG, F32)
        l_ref[...] = jnp.zeros_like(l_ref)
        acc_ref[...] = jnp.zeros_like(acc_ref)
        carry_ref[...] = jnp.zeros_like(carry_ref)
        lf = lfnew_ref[0]
        s_new = lane // H_C
        cn_ref[...] = jnp.sum(jnp.where(match & (s_new <= row % ds), lf, 0.0), axis=1, keepdims=True)
        r2 = lax.broadcasted_iota(jnp.int32, (LANES, LANES), 0)
        c2 = lax.broadcasted_iota(jnp.int32, (LANES, LANES), 1)
        upto = ((r2 % H_C == c2 % H_C) & (r2 // H_C <= c2 // H_C)).astype(BF16)
        cum_keys = _dot3_left(jnp.broadcast_to(lf, (8, LANES)), upto)[0:1]
        k_rows = knew_ref[0].astype(BF16)
        s = _dot_nt(q, k_rows) * SCALE + (cn_ref[...] - cum_keys)
        s = jnp.where(match & (s_new <= row % ds), s, NEG)
        _softmax_update(s, lambda p: _dot(p.astype(BF16), vnew_ref[0].astype(BF16)), m_ref, l_ref, acc_ref)

    @pl.when(j > 0)
    def _():
        later = carry_ref[...]
        pieces, values = [], []
        for sfx_ref, tot_ref, page_ref in zip(sfx_refs, tot_refs, page_refs):
            k_rows = page_ref[0, :, 0].reshape(BLK * H_C, HEAD_DIM).astype(BF16)
            values.append(page_ref[0, :, 1].reshape(BLK * H_C, HEAD_DIM).astype(BF16))
            s = _dot_nt(q, k_rows) * SCALE + (sfx_ref[0] + (cn_ref[...] + later))
            pieces += [jnp.where(match, piece, NEG) for piece in _lane_chunks(s)]
            later = later + tot_ref[0]
        carry_ref[...] = later
        cols = BLK * H_C

        def pv(p):
            out = 0.0
            for i, v_rows in enumerate(values):
                out = out + _dot(p[:, i * cols:(i + 1) * cols].astype(BF16), v_rows)
            return out

        _softmax_update(jnp.concatenate(pieces, axis=1), pv, m_ref, l_ref, acc_ref)

    @pl.when(j == n_steps)
    def _():
        o_ref[0] = acc_ref[...] / l_ref[...]


def _attn_c_sample(page_table, q_rows, lf_new, k_new, v_new, sfx_flat, tot_col, cache, ds):
    db, n_pages = page_table.shape
    n_rows = H_C * ds
    pps = _pages_per_step(n_pages, 4)
    n_steps = n_pages // pps

    def page_of(i):
        return lambda b, j, pt: pt[b, n_pages - 1 - (jnp.maximum(j, 1) - 1) * pps - i]

    def specs(shape):
        zeros = (0,) * (len(shape) - 1)
        return [pl.BlockSpec(shape, (lambda f: lambda b, j, pt: (f(b, j, pt),) + zeros)(page_of(i)))
                for i in range(pps)]

    new_spec = pl.BlockSpec((1, LANES, HEAD_DIM), lambda b, j, pt: (b, 0, 0))
    return pl.pallas_call(
        functools.partial(_attn_c_sample_kernel, n_steps=n_steps, pps=pps, ds=ds),
        grid_spec=pltpu.PrefetchScalarGridSpec(
            num_scalar_prefetch=1,
            grid=(db, n_steps + 1),
            in_specs=[pl.BlockSpec((1, n_rows, HEAD_DIM), lambda b, j, pt: (b, 0, 0)),
                      pl.BlockSpec((1, 1, LANES), lambda b, j, pt: (b, 0, 0)),
                      new_spec, new_spec]
                     + specs((1, 1, BLK * H_C)) + specs((1, n_rows, 1)) + specs((1, BLK, 2, H_C, HEAD_DIM)),
            out_specs=pl.BlockSpec((1, n_rows, HEAD_DIM), lambda b, j, pt: (b, 0, 0)),
            scratch_shapes=[pltpu.VMEM((n_rows, 1), F32), pltpu.VMEM((n_rows, 1), F32),
                            pltpu.VMEM((n_rows, HEAD_DIM), F32), pltpu.VMEM((n_rows, 1), F32),
                            pltpu.VMEM((n_rows, 1), F32)]),
        out_shape=jax.ShapeDtypeStruct((db, n_rows, HEAD_DIM), F32),
        compiler_params=_params(("parallel", "arbitrary")),
        name="forget_attend_sample",
    )(page_table, q_rows, lf_new, k_new, v_new, *([sfx_flat] * pps), *([tot_col] * pps), *([cache] * pps))


def _layer_norm(xf, g, b):
    mu = jnp.mean(xf, axis=1, keepdims=True)
    d = xf - mu
    var = jnp.mean(d * d, axis=1, keepdims=True)
    return d * lax.rsqrt(var + LN_EPS) * g + b


def _route(y, wr_ref, br_ref):
    y_hi, y_lo = _split2(y)
    w = wr_ref[...]
    p1 = _dot(y_hi, w)
    p2 = _dot(y_lo, w)
    logits = p1 + pltpu.roll(p1, LANES - N_EXPERTS, 1) + p2
    lane = lax.broadcasted_iota(jnp.int32, logits.shape, 1)
    in_grp = lane < N_GROUPS
    s = [jax.nn.sigmoid(logits if j == 0 else pltpu.roll(logits, LANES - j * N_GROUPS, 1))
         for j in range(EPG)]
    sel = [jnp.where(in_grp, s[j] + br_ref[j:j + 1, :], NEG) for j in range(EPG)]
    top2 = None
    for a in range(EPG):
        for b in range(a + 1, EPG):
            pair = sel[a] + sel[b]
            top2 = pair if top2 is None else jnp.maximum(top2, pair)
    top2 = jnp.where(in_grp, top2, -jnp.inf)
    best_val = jnp.max(top2, axis=1, keepdims=True)
    g_best = jnp.min(jnp.where(top2 == best_val, lane, LANES), axis=1, keepdims=True)
    mine = lane == g_best
    picked = []
    for j in range(EPG):
        rank = jnp.zeros(logits.shape, F32)
        for i in range(EPG):
            if i == j:
                continue
            ahead = (sel[i] >= sel[j]) if i < j else (sel[i] > sel[j])
            rank = rank + ahead.astype(F32)
        picked.append(jnp.sum(jnp.where(mine & (rank < 2.0), s[j], 0.0), axis=1, keepdims=True))
    denom = picked[0] + picked[1] + picked[2] + picked[3]
    extra = jnp.where(lane == EPG, g_best.astype(F32), 0.0)
    for j in range(EPG):
        extra = jnp.where(lane == j, picked[j] / denom, extra)
    return extra


def _outproj_kernel(o1_ref, o2_ref, s1_ref, s2_ref, w1_ref, w2_ref, x_ref, g_ref, b_ref, wr_ref, br_ref, out_ref,
                    *, d_model, n_prompt_tiles):
    def finish(o1, o2, rows):
        mix = _dot(o1.astype(BF16), w1_ref[...]) + _dot(o2.astype(BF16), w2_ref[...])
        y = _layer_norm(ALPHA * x_ref[:rows, :] + mix, g_ref[...], b_ref[...])
        out_ref[:rows, :d_model] = y
        out_ref[:rows, d_model:] = _route(y, wr_ref, br_ref)

    @pl.when(pl.program_id(0) < n_prompt_tiles)
    def _():
        finish(o1_ref[...], o2_ref[...], o1_ref.shape[0])

    @pl.when(pl.program_id(0) >= n_prompt_tiles)
    def _():
        finish(s1_ref[...], s2_ref[...], s1_ref.shape[0])


def _outproj(o1, o2, s1, s2, o2_block, w_bf16, x, g, b, wr, br, d_model, tm):
    nf = x.shape[0]
    half = w_bf16.shape[0] // 2
    n_prompt, n_sample = o1.shape[0], s1.shape[0]
    assert n_prompt % tm == 0 and n_sample <= tm and n_prompt + n_sample == nf
    last = n_prompt // tm - 1
    return pl.pallas_call(
        functools.partial(_outproj_kernel, d_model=d_model, n_prompt_tiles=n_prompt // tm),
        grid=(n_prompt // tm + 1,),
        in_specs=[pl.BlockSpec((tm, half), lambda i: (jnp.minimum(i, last), 0)),
                  pl.BlockSpec((tm, half), lambda i: (jnp.minimum(i, last), o2_block)),
                  pl.BlockSpec((n_sample, half), lambda i: (0, 0)),
                  pl.BlockSpec((n_sample, half), lambda i: (0, o2_block)),
                  pl.BlockSpec((half, d_model), lambda i: (0, 0)),
                  pl.BlockSpec((half, d_model), lambda i: (1, 0)),
                  pl.BlockSpec((tm, d_model), lambda i: (i, 0)),
                  pl.BlockSpec((1, d_model), lambda i: (0, 0)),
                  pl.BlockSpec((1, d_model), lambda i: (0, 0)),
                  pl.BlockSpec((d_model, LANES), lambda i: (0, 0)),
                  pl.BlockSpec((EPG, LANES), lambda i: (0, 0))],
        out_specs=pl.BlockSpec((tm, d_model + XCOLS), lambda i: (i, 0)),
        out_shape=jax.ShapeDtypeStruct((nf, d_model + XCOLS), F32),
        compiler_params=_params(("parallel",)),
        name="outproj_norm_route",
    )(o1, o2, s1, s2, w_bf16, w_bf16, x, g, b, wr, br)


def _gather_rows(idx_ref, base, src_ref, dst_ref, sem, n_rows):
    def issue(r, c):
        pltpu.make_async_copy(src_ref.at[pl.ds(idx_ref[base + r], 1)], dst_ref.at[pl.ds(r, 1)], sem).start()
        return c

    lax.fori_loop(0, n_rows, issue, 0, unroll=8)

    def drain(r, c):
        pltpu.make_async_copy(src_ref.at[pl.ds(0, 1)], dst_ref.at[pl.ds(r, 1)], sem).wait()
        return c

    lax.fori_loop(0, n_rows, drain, 0, unroll=8)


def _moe_kernel(grp_ref, valid_ref, src_ref, xa_ref, wg_ref, wu_ref, wd_ref, g_ref, b_ref, o_ref,
                xs_ref, xb_ref, acc_ref, sem, *, d_model, tm):
    i = pl.program_id(0)
    e = pl.program_id(1)
    valid = valid_ref[i] == 1

    @pl.when(valid & (e == 0))
    def _():
        _gather_rows(src_ref, i * tm, xa_ref, xs_ref, sem, tm)
        xb_ref[...] = xs_ref[:, :d_model].astype(BF16)
        acc_ref[...] = jnp.zeros_like(acc_ref)

    @pl.when(valid)
    def _():
        xb = xb_ref[...]
        a = _dot(xb, wg_ref[0].astype(BF16))
        h = a * jax.nn.sigmoid(a) * _dot(xb, wu_ref[0].astype(BF16))
        extra = xs_ref[:, d_model:]
        lane = lax.broadcasted_iota(jnp.int32, extra.shape, 1)
        gate = jnp.sum(jnp.where(lane == e, extra, 0.0), axis=1, keepdims=True)
        acc_ref[...] += _dot((h * gate).astype(BF16), wd_ref[0].astype(BF16))

    @pl.when(valid & (e == EPG - 1))
    def _():
        o_ref[...] = _layer_norm(ALPHA * xs_ref[:, :d_model] + acc_ref[...], g_ref[...], b_ref[...])

    @pl.when(jnp.logical_not(valid) & (e == EPG - 1))
    def _():
        o_ref[...] = jnp.zeros_like(o_ref)


def _moe(tile_grp, tile_valid, src, xa, wg, wu, wd, g, b, d_model, tm):
    n_tiles = tile_grp.shape[0]
    d_exp = wg.shape[2]

    def w_idx(i, e, grp, valid, src):
        return (grp[i] * EPG + jnp.where(valid[i] == 1, e, EPG - 1), 0, 0)

    return pl.pallas_call(
        functools.partial(_moe_kernel, d_model=d_model, tm=tm),
        grid_spec=pltpu.PrefetchScalarGridSpec(
            num_scalar_prefetch=3,
            grid=(n_tiles, EPG),
            in_specs=[pl.BlockSpec(memory_space=pl.ANY),
                      pl.BlockSpec((1, d_model, d_exp), w_idx),
                      pl.BlockSpec((1, d_model, d_exp), w_idx),
                      pl.BlockSpec((1, d_exp, d_model), w_idx),
                      pl.BlockSpec((1, d_model), lambda i, e, *_: (0, 0)),
                      pl.BlockSpec((1, d_model), lambda i, e, *_: (0, 0))],
            out_specs=pl.BlockSpec((tm, d_model), lambda i, e, *_: (i, 0)),
            scratch_shapes=[pltpu.VMEM((tm, d_model + XCOLS), F32),
                            pltpu.VMEM((tm, d_model), BF16),
                            pltpu.VMEM((tm, d_model), F32),
                            pltpu.SemaphoreType.DMA(())]),
        out_shape=jax.ShapeDtypeStruct((n_tiles * tm, d_model), F32),
        compiler_params=_params(("arbitrary", "arbitrary")),
        name="grouped_moe",
    )(tile_grp, tile_valid, src, xa, wg, wu, wd, g, b)


def _unpermute_kernel(idx_ref, src_ref, o_ref, sem, *, tg):
    _gather_rows(idx_ref, pl.program_id(0) * tg, src_ref, o_ref, sem, tg)


def _unpermute(dest_padded, ys, n_rows, tg):
    d = ys.shape[1]
    return pl.pallas_call(
        functools.partial(_unpermute_kernel, tg=tg),
        grid_spec=pltpu.PrefetchScalarGridSpec(
            num_scalar_prefetch=1,
            grid=(_cdiv(n_rows, tg),),
            in_specs=[pl.BlockSpec(memory_space=pl.ANY)],
            out_specs=pl.BlockSpec((tg, d), lambda i, idx: (i, 0)),
            scratch_shapes=[pltpu.SemaphoreType.DMA(())]),
        out_shape=jax.ShapeDtypeStruct((n_rows, d), F32),
        compiler_params=_params(("arbitrary",)),
        name="unpermute_rows",
    )(dest_padded, ys)


def _routing_plan(grp, tm):
    nf = grp.shape[0]
    n_tiles = _cdiv(nf + N_GROUPS * (tm - 1), tm)
    onehot = (grp[:, None] == jnp.arange(N_GROUPS, dtype=jnp.int32)[None, :]).astype(jnp.int32)
    counts = jnp.sum(onehot, axis=0)
    rank = jnp.sum((jnp.cumsum(onehot, axis=0) - onehot) * onehot, axis=1)
    padded = ((counts + tm - 1) // tm) * tm
    ends = jnp.cumsum(padded)
    dest = (ends - padded)[grp] + rank
    src = jnp.zeros((n_tiles * tm,), jnp.int32).at[dest].set(jnp.arange(nf, dtype=jnp.int32))
    starts = jnp.arange(n_tiles, dtype=jnp.int32) * tm
    tile_valid = (starts < ends[-1]).astype(jnp.int32)
    tile_grp = jnp.minimum(jnp.searchsorted(ends, starts, side="right"), N_GROUPS - 1).astype(jnp.int32)
    last_grp = tile_grp[jnp.maximum(ends[-1] // tm - 1, 0)]
    tile_grp = jnp.where(tile_valid == 1, tile_grp, last_grp)
    return tile_grp, tile_valid, src, dest


def _row_tile(n):
    return next(t for t in range(1024, 7, -8) if n % t == 0)


def _ffn(xa, wg, wu, wd, g, b, d_model, tm, row_sets):
    grp = xa[:, d_model + EPG].astype(jnp.int32)
    tile_grp, tile_valid, src, dest = _routing_plan(grp, tm)
    ys = _moe(tile_grp, tile_valid, src, xa, wg, wu, wd, g, b, d_model, tm)
    return [_unpermute(dest[rows], ys, rows.shape[0], _row_tile(rows.shape[0])) for rows in row_sets]


def _rope_tables(pos):
    def table(dim):
        half = dim // 2
        inv = ROPE_THETA ** (-jnp.arange(half, dtype=F32) / half)
        ang = pos.astype(F32)[:, None] * inv[None, :]
        cos = jnp.cos(ang)
        sin = jnp.sin(ang)
        reps = LANES // dim
        return jnp.tile(jnp.concatenate([cos, cos], axis=1), (1, reps)), \
            jnp.tile(jnp.concatenate([-sin, sin], axis=1), (1, reps))

    c128, s128 = table(HEAD_DIM)
    c64, s64 = table(IDX_DIM)
    return c128, s128, c64, s64


def _pad_rows(a, rows):
    return jnp.pad(a, [(0, 0), (0, rows - a.shape[1])] + [(0, 0)] * (a.ndim - 2))


def _head_rows(a, db, ds, heads):
    return a.reshape(db, ds, heads, HEAD_DIM).transpose(0, 2, 1, 3).reshape(db, heads * ds, HEAD_DIM)


def _token_rows(a, db, ds, heads):
    return a.reshape(db, heads, ds, HEAD_DIM).transpose(0, 2, 1, 3).reshape(db * ds, heads * HEAD_DIM)


def _new_block(a, db):
    slots = a.shape[2]
    return _pad_rows(a, LANES // slots).reshape(db, LANES, HEAD_DIM)


def kernel(x_prompt, x_sample, cache_l0_a_kv, cache_l0_idx_k, cache_l0_b_kv, cache_l1_c_kv, cache_l1_logf,
           page_table, meta_tokens, w_in_l0, w_out_l0, w_in_l1, b_forget_l1, w_out_l1, ln_mix_g, ln_mix_b,
           ln_ffn_g, ln_ffn_b, w_router, b_router, w_gate, w_up, w_down):
    n_batch, seq, d_model = x_prompt.shape
    db, ds, _ = x_sample.shape
    assert ds <= 8 and d_model % LANES == 0
    t_len = seq + N_META
    tp = _cdiv(t_len, BLK) * BLK
    n_prompt = n_batch * tp
    n_sample = db * ds
    nf = n_prompt + n_sample
    n_pool = cache_l0_a_kv.shape[0]
    n_pages = page_table.shape[1]
    past = n_pages * BLK
    topk_prompt = min(TOPK_MAX, seq // 4)
    topk_sample = min(TOPK_MAX, (past + ds) // 4)
    tm_proj = 1072 if nf % 1072 == 0 else 128
    tm_out = 256
    tm_moe = 512
    all_rows = jnp.arange(nf, dtype=jnp.int32)
    prompt_out_rows = (jnp.arange(n_batch, dtype=jnp.int32)[:, None] * tp + N_META
                       + jnp.arange(seq, dtype=jnp.int32)[None, :]).reshape(-1)

    meta = jnp.broadcast_to(meta_tokens[None], (n_batch, N_META, d_model)).astype(x_prompt.dtype)
    hp = _pad_rows(jnp.concatenate([meta, x_prompt], axis=1), tp)
    x0 = jnp.concatenate([hp.reshape(n_prompt, d_model), x_sample.reshape(n_sample, d_model)], axis=0)
    pos = jnp.concatenate([jnp.tile(jnp.arange(tp), n_batch), jnp.tile(past + jnp.arange(ds), db)])
    tables = _rope_tables(pos)

    cuts = np.cumsum((0, H_A * HEAD_DIM, HKV_A * HEAD_DIM, HKV_A * HEAD_DIM, H_B * HEAD_DIM, H_B * HEAD_DIM,
                      H_B * HEAD_DIM, H_IDX * IDX_DIM, IDX_DIM, H_IDX))
    seg = [w_in_l0[:, cuts[i]:cuts[i + 1]] for i in range(9)]
    qa_w, ka_w, va_w, qb_w, kb_w, vb_w, iq_w, ik_w, iw_w = seg
    zeros = lambda n: jnp.zeros((d_model, n), w_in_l0.dtype)
    w0 = jnp.concatenate([qa_w, iq_w, ka_w, va_w, qb_w, kb_w, vb_w, ik_w, zeros(LANES - IDX_DIM),
                          iw_w, zeros(LANES - H_IDX)], axis=1).astype(BF16)
    modes0 = jnp.asarray([1] * 4 + [2] * 4 + [1] * 2 + [0] * 14 + [3], jnp.int32)
    n_qkv = 3 * H_C * HEAD_DIM
    w1 = jnp.concatenate([w_in_l1[:, :n_qkv], w_in_l1[:, n_qkv:], zeros(MODE_COLS - H_C)], axis=1).astype(BF16)
    bf_row = jnp.zeros((1, LANES), F32).at[0, :H_C].set(b_forget_l1)
    w_out0 = w_out_l0.astype(BF16)
    w_out1 = w_out_l1.astype(BF16)
    wr_perm = w_router.reshape(d_model, N_GROUPS, EPG).transpose(0, 2, 1).reshape(d_model, N_EXPERTS)
    wr_hi = wr_perm.astype(BF16)
    wr_lo = (wr_perm - wr_hi.astype(F32)).astype(BF16)
    wr = jnp.concatenate([wr_hi, wr_lo, jnp.zeros((d_model, LANES - 2 * N_EXPERTS), BF16)], axis=1)
    br = jnp.zeros((EPG, LANES), F32).at[:, :N_GROUPS].set(b_router.reshape(N_GROUPS, EPG).T)
    wg, wu, wd = w_gate, w_up, w_down
    row = lambda v: v.reshape(1, d_model)

    p0 = _inproj(x0, w0, d_model, tm_proj, modes0, tables)
    col = lambda blk0, n: slice(blk0 * LANES, (blk0 + n) * LANES)
    ps = p0[n_prompt:]

    bias_p = _idx_prompt(p0, n_batch, tp, topk_prompt)
    oa_p = _attn_a_prompt(p0, bias_p, n_batch, tp)
    ob_p = _attn_b_prompt(p0, n_batch, tp)

    grp_q = H_A // HKV_A
    iq_th = jnp.tile(ps[:, col(8, 8)].reshape(db, ds * H_IDX, IDX_DIM), (1, grp_q, 1))
    iw_th = jnp.tile(ps[:, 49 * LANES:49 * LANES + H_IDX].reshape(db, ds * H_IDX, 1), (1, grp_q, 1))
    ik_s = ps[:, 48 * LANES:48 * LANES + IDX_DIM].reshape(db, ds, IDX_DIM)
    bias_s = _idx_sample(page_table, iq_th, iw_th, _pad_rows(ik_s, BLK), cache_l0_idx_k, topk_sample, ds)

    ka_s = ps[:, col(16, 4)].reshape(db, ds, HKV_A, HEAD_DIM)
    va_s = ps[:, col(20, 4)].reshape(db, ds, HKV_A, HEAD_DIM)
    kv_a_s = jnp.stack([ka_s, va_s], axis=2)
    oa_s = _attn_a_sample(page_table, _head_rows(ps[:, col(0, 8)], db, ds, H_A), bias_s,
                          _new_block(kv_a_s.reshape(db, ds, 2 * HKV_A, HEAD_DIM), db),
                          cache_l0_a_kv.reshape(n_pool, BLK * 2 * HKV_A, HEAD_DIM), ds)
    oa_s = _token_rows(oa_s, db, ds, H_A)

    kb_s = ps[:, col(32, 8)].reshape(db, ds, H_B, HEAD_DIM)
    vb_s = ps[:, col(40, 8)].reshape(db, ds, H_B, HEAD_DIM)
    kv_b_s = jnp.stack([kb_s, vb_s], axis=2)
    ob_s = _attn_b_sample(page_table, _head_rows(ps[:, col(24, 8)], db, ds, H_B),
                          _new_block(kb_s, db), _new_block(vb_s, db), cache_l0_b_kv, ds)
    ob_s = _token_rows(ob_s, db, ds, H_B)

    xa1 = _outproj(oa_p, ob_p, oa_s, ob_s, 0, w_out0, x0, row(ln_mix_g[0]), row(ln_mix_b[0]), wr, br, d_model,
                   tm_out)
    x1, = _ffn(xa1, wg[0], wu[0], wd[0], row(ln_ffn_g[0]), row(ln_ffn_b[0]), d_model, tm_moe, [all_rows])

    p1 = _inproj(x1, w1, d_model, tm_proj)
    logf, cum = _logf(p1, bf_row, nf, tp // BLK)
    cum_t = cum[:n_prompt, :H_C].reshape(n_batch, tp, H_C).transpose(0, 2, 1)
    oc_p = _attn_c_prompt(p1, cum_t[..., None], cum_t[:, :, None, :], n_batch, tp)

    ps1 = p1[n_prompt:]
    kc_s = ps1[:, col(16, 16)].reshape(db, ds, H_C, HEAD_DIM)
    vc_s = ps1[:, col(32, 16)].reshape(db, ds, H_C, HEAD_DIM)
    kv_c_s = jnp.stack([kc_s, vc_s], axis=2)
    logf_s = logf[n_prompt:, :H_C].reshape(db, ds, H_C)
    lf_new = _pad_rows(logf_s, LANES // H_C).reshape(db, 1, LANES)
    n_lf = n_pool * H_C
    tn_lf = min(2048, _cdiv(n_lf, LANES) * LANES)
    n_lf_pad = _cdiv(n_lf, tn_lf) * tn_lf
    lf_t = jnp.pad(cache_l1_logf.astype(F32).transpose(1, 0, 2).reshape(BLK, n_lf), [(0, 0), (0, n_lf_pad - n_lf)])
    sfx_t, tot = _page_suffix(lf_t, tn_lf)
    sfx_flat = sfx_t[:, :n_lf].reshape(BLK, n_pool, H_C).transpose(1, 0, 2).reshape(n_pool, 1, BLK * H_C)
    tot_col = jnp.repeat(tot[0, :n_lf].reshape(n_pool, H_C), ds, axis=1)[..., None]
    oc_s = _attn_c_sample(page_table, _head_rows(ps1[:, col(0, 16)], db, ds, H_C), lf_new,
                          _new_block(kc_s, db), _new_block(vc_s, db), sfx_flat, tot_col, cache_l1_c_kv, ds)
    oc_s = _token_rows(oc_s, db, ds, H_C)

    xa2 = _outproj(oc_p, oc_p, oc_s, oc_s, 1, w_out1, x1, row(ln_mix_g[1]), row(ln_mix_b[1]), wr, br, d_model,
                   tm_out)
    y_prompt, y_sample = _ffn(xa2, wg[1], wu[1], wd[1], row(ln_ffn_g[1]), row(ln_ffn_b[1]), d_model, tm_moe,
                              [prompt_out_rows, all_rows[n_prompt:]])

    def prompt_rows(a, blk0, heads):
        return a[:n_prompt, col(blk0, heads)].reshape(n_batch, tp, heads, HEAD_DIM)[:, :t_len]

    y_prompt = y_prompt.reshape(n_batch, seq, d_model)
    y_sample = y_sample.reshape(db, ds, d_model)
    a_kv_p = jnp.stack([prompt_rows(p0, 16, HKV_A), prompt_rows(p0, 20, HKV_A)], axis=2)
    idx_k_p = p0[:n_prompt, 48 * LANES:48 * LANES + IDX_DIM].reshape(n_batch, tp, IDX_DIM)[:, :t_len]
    b_kv_p = jnp.stack([prompt_rows(p0, 32, H_B), prompt_rows(p0, 40, H_B)], axis=2)
    c_kv_p = jnp.stack([prompt_rows(p1, 16, H_C), prompt_rows(p1, 32, H_C)], axis=2)
    logf_p = logf[:n_prompt, :H_C].reshape(n_batch, tp, H_C)[:, :t_len]
    return (y_prompt, y_sample, a_kv_p, kv_a_s, idx_k_p, ik_s, b_kv_p, kv_b_s, c_kv_p, kv_c_s, logf_p, logf_s)
```

```python
import functools

import numpy as np
import jax
import jax.numpy as jnp
from jax import lax
from jax.experimental import pallas as pl
from jax.experimental.pallas import tpu as pltpu

HEAD_DIM = 128
H_A = 8
HKV_A = 4
H_B = 8
H_C = 16
H_IDX = 16
IDX_DIM = 64
TOPK_MAX = 256
N_META = 16
BLK = 128
ROPE_THETA = 10000.0
N_EXPERTS = 32
N_GROUPS = 8
EPG = N_EXPERTS // N_GROUPS
LN_EPS = 1e-5
DEPTH = 2
ALPHA = (2 * DEPTH) ** 0.25
NEG = -1e30
INT_MIN = -2 ** 31
KEY_NEG_INF = -2139095041
LANES = 128
TN = 1280
MODE_COLS = 256
XCOLS = 128
VMEM_LIMIT = 56 * 1024 * 1024
SCALE = HEAD_DIM ** -0.5

F32 = jnp.float32
BF16 = jnp.bfloat16
NT_DIMS = (((1,), (1,)), ((), ()))


def _cdiv(a, b):
    return (a + b - 1) // b


def _dot(a, b):
    return jnp.dot(a, b, preferred_element_type=F32)


def _dot_nt(a, b):
    return lax.dot_general(a, b, NT_DIMS, preferred_element_type=F32)


def _split2(x):
    hi = x.astype(BF16)
    lo = (x - hi.astype(F32)).astype(BF16)
    return hi, lo


def _split3(x):
    hi = x.astype(BF16)
    r = x - hi.astype(F32)
    mid = r.astype(BF16)
    lo = (r - mid.astype(F32)).astype(BF16)
    return hi, mid, lo


def _dot3(a_bf16, x):
    hi, mid, lo = _split3(x)
    return _dot(a_bf16, hi) + _dot(a_bf16, mid) + _dot(a_bf16, lo)


def _dot3_left(x, a_bf16):
    hi, mid, lo = _split3(x)
    return _dot(hi, a_bf16) + _dot(mid, a_bf16) + _dot(lo, a_bf16)


def _neg_softplus(z):
    return -(jnp.maximum(z, 0.0) + jnp.log1p(jnp.exp(-jnp.abs(z))))


def _neg_softplus_bulk(z):
    return -(jnp.maximum(z, 0.0) + jnp.log(1.0 + jnp.exp(-jnp.abs(z))))


def _lane_chunks(x):
    return [x[:, c * LANES:(c + 1) * LANES] for c in range(x.shape[1] // LANES)]


def _params(sem, vmem=VMEM_LIMIT):
    return pltpu.CompilerParams(dimension_semantics=sem, vmem_limit_bytes=vmem)


def _rope128(a, c, s):
    return a * c + pltpu.roll(a, 64, 1) * s


def _rope64(a, c, s):
    lane = lax.broadcasted_iota(jnp.int32, a.shape, 1)
    first = (lane % 64) < 32
    partner = jnp.where(first, pltpu.roll(a, 96, 1), pltpu.roll(a, 32, 1))
    return a * c + partner * s


def _inproj_rope_kernel(modes_ref, x_ref, w_ref, c128_ref, s128_ref, c64_ref, s64_ref, o_ref, xb_ref, *, tn):
    j = pl.program_id(1)

    @pl.when(j == 0)
    def _():
        xb_ref[...] = x_ref[...].astype(BF16)

    acc = _dot(xb_ref[...], w_ref[...])
    chunk = lambda c: slice(c * LANES, (c + 1) * LANES)
    per_tile = tn // MODE_COLS
    for t in range(per_tile):
        mode = modes_ref[j * per_tile + t]
        c0 = t * (MODE_COLS // LANES)
        span = slice(t * MODE_COLS, (t + 1) * MODE_COLS)

        @pl.when(mode == 0)
        def _():
            o_ref[:, span] = acc[:, span]

        @pl.when(mode == 1)
        def _():
            for c in range(c0, c0 + MODE_COLS // LANES):
                o_ref[:, chunk(c)] = _rope128(acc[:, chunk(c)], c128_ref[...], s128_ref[...])

        @pl.when(mode == 2)
        def _():
            for c in range(c0, c0 + MODE_COLS // LANES):
                o_ref[:, chunk(c)] = _rope64(acc[:, chunk(c)], c64_ref[...], s64_ref[...])

        @pl.when(mode == 3)
        def _():
            o_ref[:, chunk(c0)] = _rope64(acc[:, chunk(c0)], c64_ref[...], s64_ref[...])
            rest = slice((c0 + 1) * LANES, (t + 1) * MODE_COLS)
            o_ref[:, rest] = acc[:, rest]


def _inproj_plain_kernel(x_ref, w_ref, o_ref, xb_ref):
    @pl.when(pl.program_id(1) == 0)
    def _():
        xb_ref[...] = x_ref[...].astype(BF16)

    o_ref[...] = _dot(xb_ref[...], w_ref[...])


def _inproj(x, w_bf16, d_model, tm, modes=None, tables=None):
    nf = x.shape[0]
    ncols = w_bf16.shape[1]
    tn = TN if ncols % TN == 0 else MODE_COLS
    grid = (_cdiv(nf, tm), ncols // tn)
    scratch = [pltpu.VMEM((tm, d_model), BF16)]
    out_shape = jax.ShapeDtypeStruct((nf, ncols), F32)
    if modes is None:
        return pl.pallas_call(
            _inproj_plain_kernel,
            grid=grid,
            in_specs=[pl.BlockSpec((tm, d_model), lambda i, j: (i, 0)),
                      pl.BlockSpec((d_model, tn), lambda i, j: (0, j))],
            out_specs=pl.BlockSpec((tm, tn), lambda i, j: (i, j)),
            out_shape=out_shape,
            scratch_shapes=scratch,
            compiler_params=_params(("parallel", "arbitrary")),
            name="inproj_plain",
        )(x, w_bf16)
    tab_spec = pl.BlockSpec((tm, LANES), lambda i, j, m: (i, 0))
    return pl.pallas_call(
        functools.partial(_inproj_rope_kernel, tn=tn),
        grid_spec=pltpu.PrefetchScalarGridSpec(
            num_scalar_prefetch=1,
            grid=grid,
            in_specs=[pl.BlockSpec((tm, d_model), lambda i, j, m: (i, 0)),
                      pl.BlockSpec((d_model, tn), lambda i, j, m: (0, j)),
                      tab_spec, tab_spec, tab_spec, tab_spec],
            out_specs=pl.BlockSpec((tm, tn), lambda i, j, m: (i, j)),
            scratch_shapes=scratch),
        out_shape=out_shape,
        compiler_params=_params(("parallel", "arbitrary")),
        name="inproj_rope",
    )(modes, x, w_bf16, *tables)


def _sortable_key(score):
    score = jnp.where(score == 0.0, 0.0, score)
    bits = lax.bitcast_convert_type(score, jnp.int32)
    return bits ^ ((bits >> 31) & 0x7FFFFFFF)


def _count(mask):
    return jnp.sum(mask.astype(F32), axis=1, keepdims=True)


def _topk_select(key_ref, vis, s_pos, topk, n_cols, thr_ref, need_ref, jb_ref):
    kf = float(topk)
    c0 = _count(key_ref[...] >= 0)
    ans0 = jnp.where(c0 >= kf, 0, INT_MIN).astype(jnp.int32)

    def body(i, ans):
        cand = ans | jnp.left_shift(jnp.int32(1), 30 - i)
        cnt = _count(key_ref[...] >= cand)
        return jnp.where(cnt >= kf, cand, ans)

    thr = lax.fori_loop(0, 31, body, ans0)
    key = key_ref[...]
    need = kf - _count(key > thr)
    n_eq = _count((key == thr) & vis)
    thr_ref[...] = thr
    need_ref[...] = need
    jb_ref[...] = jnp.full(jb_ref.shape, n_cols, jnp.int32)
    n_bits = int(np.ceil(np.log2(n_cols))) + 1

    @pl.when(jnp.max(n_eq - need) > 0.0)
    def _():
        def body2(i, ans):
            cand = ans | jnp.left_shift(jnp.int32(1), n_bits - 1 - i)
            eqv = (key_ref[...] == thr_ref[...]) & vis
            c = _count(eqv & (s_pos < cand))
            return jnp.where(c < need_ref[...], cand, ans)

        jb_ref[...] = lax.fori_loop(0, n_bits, body2, jnp.zeros(jb_ref.shape, jnp.int32))

    return vis & ((key > thr) | ((key == thr) & (s_pos <= jb_ref[...])))


def _causal_extents(tp, n):
    nb = tp // BLK
    return [BLK * e for e in sorted({_cdiv(nb * (k + 1), n) for k in range(n)})]


def _for_causal_extent(qi, tp, body, n):
    lo = 0
    for extent in _causal_extents(tp, n):
        need = (qi + 1) * BLK
        pl.when((need > lo) & (need <= extent))(functools.partial(body, extent))
        lo = extent


def _idx_prompt_kernel(iq_ref, iw_ref, ik_ref, bias_ref, ikd_ref, key_ref, thr_ref, need_ref, jb_ref,
                       *, topk, tp):
    qi = pl.program_id(1)

    @pl.when(qi == 0)
    def _():
        ik = ik_ref[...]
        ikd_ref[...] = (ik + pltpu.roll(ik, 64, 1)).astype(BF16)

    def select(n_keys):
        lane = lax.broadcasted_iota(jnp.int32, (1, LANES), 1)
        keys = ikd_ref[:n_keys, :]
        score = jnp.zeros((BLK, n_keys), F32)
        for h in range(H_IDX):
            pair = iq_ref[:, (h // 2) * LANES:(h // 2 + 1) * LANES]
            lo = (h % 2) * IDX_DIM
            qh = jnp.where((lane >= lo) & (lane < lo + IDX_DIM), pair, 0.0).astype(BF16)
            w = iw_ref[:, h:h + 1] * (H_IDX ** -0.5 * IDX_DIM ** -0.5)
            score = score + jnp.maximum(_dot_nt(qh, keys), 0.0) * w

        t_pos = qi * BLK + lax.broadcasted_iota(jnp.int32, (BLK, 1), 0)
        s_pos = lax.broadcasted_iota(jnp.int32, (1, n_keys), 1)
        vis = s_pos <= t_pos
        keys_view = key_ref.at[:, :n_keys]
        keys_view[...] = jnp.where(vis, _sortable_key(score), KEY_NEG_INF)
        sel = _topk_select(keys_view, vis, s_pos, topk, n_keys, thr_ref, need_ref, jb_ref)
        bias_ref[:, :n_keys] = jnp.where(sel, 0.0, NEG).astype(BF16)
        if n_keys < tp:
            bias_ref[:, n_keys:] = jnp.full((BLK, tp - n_keys), NEG, BF16)

    _for_causal_extent(qi, tp, select, 4)


def _idx_prompt(p0, n_batch, tp, topk):
    nqb = tp // BLK
    return pl.pallas_call(
        functools.partial(_idx_prompt_kernel, topk=topk, tp=tp),
        grid=(n_batch, nqb),
        in_specs=[pl.BlockSpec((BLK, 1024), lambda b, q: (b * nqb + q, 1)),
                  pl.BlockSpec((BLK, LANES), lambda b, q: (b * nqb + q, 49)),
                  pl.BlockSpec((tp, LANES), lambda b, q: (b, 48))],
        out_specs=pl.BlockSpec((BLK, tp), lambda b, q: (b * nqb + q, 0)),
        out_shape=jax.ShapeDtypeStruct((n_batch * tp, tp), BF16),
        scratch_shapes=[pltpu.VMEM((tp, LANES), BF16),
                        pltpu.VMEM((BLK, tp), jnp.int32),
                        pltpu.VMEM((BLK, 1), jnp.int32),
                        pltpu.VMEM((BLK, 1), F32),
                        pltpu.VMEM((BLK, 1), jnp.int32)],
        compiler_params=_params(("parallel", "arbitrary")),
        name="dsa_index_prompt",
    )(p0, p0, p0)


def _attn_a_prompt_kernel(q_ref, k_ref, v_ref, bias_ref, o_ref, kb_ref, vb_ref, *, tp):
    qi = pl.program_id(2)

    @pl.when(qi == 0)
    def _():
        kb_ref[...] = k_ref[...].astype(BF16)
        vb_ref[...] = v_ref[...].astype(BF16)

    def attend(n_keys):
        bias = bias_ref[:, :n_keys].astype(F32)
        for g in range(H_A // HKV_A):
            sl = slice(g * HEAD_DIM, (g + 1) * HEAD_DIM)
            s = _dot_nt(q_ref[:, sl].astype(BF16), kb_ref[:n_keys, :]) * SCALE + bias
            m = jnp.max(s, axis=1, keepdims=True)
            p = jnp.exp(s - m)
            l = jnp.sum(p, axis=1, keepdims=True)
            o_ref[:, sl] = _dot(p.astype(BF16), vb_ref[:n_keys, :]) / l

    _for_causal_extent(qi, tp, attend, 8)


def _attn_a_prompt(p0, bias, n_batch, tp):
    nqb = tp // BLK
    gw = (H_A // HKV_A) * HEAD_DIM
    return pl.pallas_call(
        functools.partial(_attn_a_prompt_kernel, tp=tp),
        grid=(n_batch, HKV_A, nqb),
        in_specs=[pl.BlockSpec((BLK, gw), lambda b, k, q: (b * nqb + q, k)),
                  pl.BlockSpec((tp, HEAD_DIM), lambda b, k, q: (b, 16 + k)),
                  pl.BlockSpec((tp, HEAD_DIM), lambda b, k, q: (b, 20 + k)),
                  pl.BlockSpec((BLK, tp), lambda b, k, q: (b * nqb + q, 0))],
        out_specs=pl.BlockSpec((BLK, gw), lambda b, k, q: (b * nqb + q, k)),
        out_shape=jax.ShapeDtypeStruct((n_batch * tp, H_A * HEAD_DIM), F32),
        scratch_shapes=[pltpu.VMEM((tp, HEAD_DIM), BF16), pltpu.VMEM((tp, HEAD_DIM), BF16)],
        compiler_params=_params(("parallel", "parallel", "arbitrary")),
        name="dsa_attend_prompt",
    )(p0, p0, p0, bias)


def _suffix_and_ones(group):
    r = lax.broadcasted_iota(jnp.int32, (LANES, 2 * LANES), 0)
    c = lax.broadcasted_iota(jnp.int32, (LANES, 2 * LANES), 1)
    return ((c >= LANES) | (r // group > c // group)).astype(BF16)


def _pick_tq(tp):
    return 384 if tp % 384 == 0 else BLK


def _stick_weights(z, strict_fn, tail, sums):
    rows = z.shape[0]
    pieces = _lane_chunks(z)
    n = len(pieces)
    lsn = [_neg_softplus_bulk(p) for p in pieces]
    ok = [strict_fn(i) for i in range(n)]
    keep = lambda i, x: x if ok[i] is None else jnp.where(ok[i], x, 0.0)
    hi, lo = _split2(jnp.concatenate([keep(i, lsn[i]) for i in range(n)], axis=0))
    ar = _dot(hi, sums) + _dot(lo, sums)
    w = [None] * n
    for i in reversed(range(n)):
        blk = ar[i * rows:(i + 1) * rows]
        w[i] = keep(i, jnp.exp(pieces[i] + lsn[i] + tail + blk[:, :LANES]))
        tail = tail + blk[:, LANES:]
    return jnp.concatenate(w, axis=1), tail


def _attn_b_prompt_kernel(q_ref, k_ref, v_ref, o_ref, kb_ref, vb_ref, acc_ref, carry_ref, *, tq):
    qi = pl.program_id(2)

    @pl.when(qi == 0)
    def _():
        kb_ref[...] = k_ref[...].astype(BF16)
        vb_ref[...] = v_ref[...].astype(BF16)

    q = q_ref[...].astype(BF16)
    acc_ref[...] = jnp.zeros_like(acc_ref)
    carry_ref[...] = jnp.zeros_like(carry_ref)
    r = lax.broadcasted_iota(jnp.int32, (tq, BLK), 0)
    c = lax.broadcasted_iota(jnp.int32, (tq, BLK), 1)
    sums = _suffix_and_ones(1)

    def block(kb, strict_fn):
        off = pl.multiple_of(kb * tq, BLK)
        z = _dot_nt(q, kb_ref[pl.ds(off, tq), :]) * SCALE
        w, tail = _stick_weights(z, strict_fn, carry_ref[...], sums)
        acc_ref[...] += _dot(w.astype(BF16), vb_ref[pl.ds(off, tq), :])
        carry_ref[...] = tail

    block(qi, lambda i: (i * BLK + c) < r)

    def body(it, carry_unused):
        block(qi - 1 - it, lambda i: None)
        return carry_unused

    lax.fori_loop(0, qi, body, 0)
    o_ref[...] = acc_ref[...]


def _attn_b_prompt(p0, n_batch, tp):
    tq = _pick_tq(tp)
    nqb = tp // tq
    return pl.pallas_call(
        functools.partial(_attn_b_prompt_kernel, tq=tq),
        grid=(n_batch, H_B, nqb),
        in_specs=[pl.BlockSpec((tq, HEAD_DIM), lambda b, h, q: (b * nqb + q, 24 + h)),
                  pl.BlockSpec((tp, HEAD_DIM), lambda b, h, q: (b, 32 + h)),
                  pl.BlockSpec((tp, HEAD_DIM), lambda b, h, q: (b, 40 + h))],
        out_specs=pl.BlockSpec((tq, HEAD_DIM), lambda b, h, q: (b * nqb + q, h)),
        out_shape=jax.ShapeDtypeStruct((n_batch * tp, H_B * HEAD_DIM), F32),
        scratch_shapes=[pltpu.VMEM((tp, HEAD_DIM), BF16), pltpu.VMEM((tp, HEAD_DIM), BF16),
                        pltpu.VMEM((tq, HEAD_DIM), F32), pltpu.VMEM((tq, LANES), F32)],
        compiler_params=_params(("parallel", "parallel", "arbitrary")),
        name="stickbreak_prompt",
    )(p0, p0, p0)


def _logf_kernel(f_ref, bf_ref, logf_ref, cum_ref, carry_ref, *, blocks_per_seq):
    i = pl.program_id(0)

    @pl.when(i % blocks_per_seq == 0)
    def _():
        carry_ref[...] = jnp.zeros_like(carry_ref)

    x = f_ref[...] + bf_ref[...]
    logf = _neg_softplus(-x)
    logf_ref[...] = logf
    r = lax.broadcasted_iota(jnp.int32, (BLK, BLK), 0)
    c = lax.broadcasted_iota(jnp.int32, (BLK, BLK), 1)
    cum = carry_ref[...] + _dot3((c <= r).astype(BF16), logf)
    cum_ref[...] = cum
    carry_ref[...] = cum[BLK - 1:BLK, :]


def _logf(p1, bf_row, n_rows, blocks_per_seq):
    return pl.pallas_call(
        functools.partial(_logf_kernel, blocks_per_seq=blocks_per_seq),
        grid=(_cdiv(n_rows, BLK),),
        in_specs=[pl.BlockSpec((BLK, LANES), lambda i: (i, 48)),
                  pl.BlockSpec((1, LANES), lambda i: (0, 0))],
        out_specs=[pl.BlockSpec((BLK, LANES), lambda i: (i, 0)),
                   pl.BlockSpec((BLK, LANES), lambda i: (i, 0))],
        out_shape=[jax.ShapeDtypeStruct((n_rows, LANES), F32),
                   jax.ShapeDtypeStruct((n_rows, LANES), F32)],
        scratch_shapes=[pltpu.VMEM((1, LANES), F32)],
        compiler_params=_params(("arbitrary",)),
        name="log_forget_cumsum",
    )(p1, bf_row)


HPS = 2


def _attn_c_prompt_kernel(q_ref, k_ref, v_ref, cq_ref, ck_ref, o_ref, kb_ref, vb_ref, s_ref, m_ref, l_ref,
                          acc_ref, *, tq):
    qi = pl.program_id(2)

    @pl.when(qi == 0)
    def _():
        kb_ref[...] = k_ref[...].astype(BF16)
        vb_ref[...] = v_ref[...].astype(BF16)

    head = lambda x, hh: x[:, hh * HEAD_DIM:(hh + 1) * HEAD_DIM]
    q = [head(q_ref, hh).astype(BF16) for hh in range(HPS)]
    cq = [cq_ref[0, hh] for hh in range(HPS)]
    m_ref[...] = jnp.full(m_ref.shape, NEG, F32)

    def logits(kb, hh):
        off = pl.multiple_of(kb * tq, LANES)
        ck = ck_ref[0, hh, :, pl.ds(off, tq)]
        keys = kb_ref[pl.ds(off, tq), hh * HEAD_DIM:(hh + 1) * HEAD_DIM]
        return off, _dot_nt(q[hh], keys) * SCALE + (cq[hh] - ck)

    def keep(off, s, hh):
        s_ref[hh, :, pl.ds(off, tq)] = s
        m = m_ref[hh]
        for piece in _lane_chunks(s):
            m = jnp.maximum(m, piece)
        m_ref[hh] = m

    def pass1(kb, carry_unused):
        for hh in range(HPS):
            keep(*logits(kb, hh), hh)
        return carry_unused

    lax.fori_loop(0, qi, pass1, 0)
    r = lax.broadcasted_iota(jnp.int32, (tq, tq), 0)
    c = lax.broadcasted_iota(jnp.int32, (tq, tq), 1)
    for hh in range(HPS):
        off, s = logits(qi, hh)
        keep(off, jnp.where(c <= r, s, NEG), hh)

    m = [jnp.max(m_ref[hh], axis=1, keepdims=True) for hh in range(HPS)]
    l_ref[...] = jnp.zeros_like(l_ref)
    acc_ref[...] = jnp.zeros_like(acc_ref)

    def pass2(kb, carry_unused):
        off = pl.multiple_of(kb * tq, LANES)
        for hh in range(HPS):
            p = jnp.exp(s_ref[hh, :, pl.ds(off, tq)] - m[hh])
            l = l_ref[hh]
            for piece in _lane_chunks(p):
                l = l + piece
            l_ref[hh] = l
            sl = slice(hh * HEAD_DIM, (hh + 1) * HEAD_DIM)
            acc_ref[:, sl] += _dot(p.astype(BF16), vb_ref[pl.ds(off, tq), sl])
        return carry_unused

    lax.fori_loop(0, qi + 1, pass2, 0)
    for hh in range(HPS):
        sl = slice(hh * HEAD_DIM, (hh + 1) * HEAD_DIM)
        o_ref[:, sl] = acc_ref[:, sl] / jnp.sum(l_ref[hh], axis=1, keepdims=True)


def _attn_c_prompt(p1, cum_col, cum_row, n_batch, tp):
    tq = _pick_tq(tp)
    nqb = tp // tq
    wide = HPS * HEAD_DIM
    k0 = H_C // HPS
    return pl.pallas_call(
        functools.partial(_attn_c_prompt_kernel, tq=tq),
        grid=(n_batch, H_C // HPS, nqb),
        in_specs=[pl.BlockSpec((tq, wide), lambda b, h, q: (b * nqb + q, h)),
                  pl.BlockSpec((tp, wide), lambda b, h, q: (b, k0 + h)),
                  pl.BlockSpec((tp, wide), lambda b, h, q: (b, 2 * k0 + h)),
                  pl.BlockSpec((1, HPS, tq, 1), lambda b, h, q: (b, h, q, 0)),
                  pl.BlockSpec((1, HPS, 1, tp), lambda b, h, q: (b, h, 0, 0))],
        out_specs=pl.BlockSpec((tq, wide), lambda b, h, q: (b * nqb + q, h)),
        out_shape=jax.ShapeDtypeStruct((n_batch * tp, H_C * HEAD_DIM), F32),
        scratch_shapes=[pltpu.VMEM((tp, wide), BF16), pltpu.VMEM((tp, wide), BF16),
                        pltpu.VMEM((HPS, tq, tp), F32), pltpu.VMEM((HPS, tq, LANES), F32),
                        pltpu.VMEM((HPS, tq, LANES), F32), pltpu.VMEM((tq, wide), F32)],
        compiler_params=_params(("parallel", "parallel", "arbitrary")),
        name="forget_attend_prompt",
    )(p1, p1, p1, cum_col, cum_row)


def _pages_per_step(n_pages, want):
    while n_pages % want:
        want //= 2
    return want


def _idx_sample_kernel(pt_ref, iq_ref, iw_ref, ikn_ref, *rest, topk, n_pages, pps, ds):
    page_refs = rest[:pps]
    bias_ref, score_ref, key_ref, thr_ref, need_ref, jb_ref = rest[pps:]
    j = pl.program_id(1)
    past = n_pages * BLK
    ncol = past + BLK
    nq = iq_ref.shape[1] // H_IDX
    q = iq_ref[0].astype(BF16)
    wgt = iw_ref[0] * (H_IDX ** -0.5)

    def scores(keys):
        dots = _dot_nt(q, keys.astype(BF16)) * (IDX_DIM ** -0.5)
        wd = jnp.maximum(dots, 0.0) * wgt
        return jnp.sum(wd.reshape(nq, H_IDX, BLK), axis=1)

    for i, page_ref in enumerate(page_refs):
        score_ref[:, pl.ds(pl.multiple_of((j * pps + i) * BLK, BLK), BLK)] = scores(page_ref[0])

    @pl.when(j == n_pages // pps - 1)
    def _():
        score_ref[:, past:] = scores(ikn_ref[0])
        t_idx = lax.broadcasted_iota(jnp.int32, (nq, 1), 0) % ds
        s_pos = lax.broadcasted_iota(jnp.int32, (1, ncol), 1)
        vis = s_pos <= past + t_idx
        key_ref[...] = jnp.where(vis, _sortable_key(score_ref[...]), KEY_NEG_INF)
        sel = _topk_select(key_ref, vis, s_pos, topk, ncol, thr_ref, need_ref, jb_ref)
        bias_ref[0] = jnp.where(sel, 0.0, NEG)


def _idx_sample(page_table, iq_th, iw_th, ik_new, cache_idx, topk, ds):
    db, n_pages = page_table.shape
    ncol = n_pages * BLK + BLK
    rows = iq_th.shape[1]
    nq = rows // H_IDX
    pps = _pages_per_step(n_pages, 16)

    def page_spec(i):
        return pl.BlockSpec((1, BLK, IDX_DIM), lambda b, j, pt: (pt[b, j * pps + i], 0, 0))

    return pl.pallas_call(
        functools.partial(_idx_sample_kernel, topk=topk, n_pages=n_pages, pps=pps, ds=ds),
        grid_spec=pltpu.PrefetchScalarGridSpec(
            num_scalar_prefetch=1,
            grid=(db, n_pages // pps),
            in_specs=[pl.BlockSpec((1, rows, IDX_DIM), lambda b, j, pt: (b, 0, 0)),
                      pl.BlockSpec((1, rows, 1), lambda b, j, pt: (b, 0, 0)),
                      pl.BlockSpec((1, BLK, IDX_DIM), lambda b, j, pt: (b, 0, 0))]
                     + [page_spec(i) for i in range(pps)],
            out_specs=pl.BlockSpec((1, nq, ncol), lambda b, j, pt: (b, 0, 0)),
            scratch_shapes=[pltpu.VMEM((nq, ncol), F32),
                            pltpu.VMEM((nq, ncol), jnp.int32),
                            pltpu.VMEM((nq, 1), jnp.int32),
                            pltpu.VMEM((nq, 1), F32),
                            pltpu.VMEM((nq, 1), jnp.int32)]),
        out_shape=jax.ShapeDtypeStruct((db, nq, ncol), F32),
        compiler_params=_params(("parallel", "arbitrary")),
        name="dsa_index_sample",
    )(page_table, iq_th, iw_th, ik_new, *([cache_idx] * pps))


def _softmax_update(s, pv_fn, m_ref, l_ref, acc_ref):
    m_old = m_ref[...]
    m_new = jnp.maximum(m_old, jnp.max(s, axis=1, keepdims=True))
    corr = jnp.exp(m_old - m_new)
    p = jnp.exp(s - m_new)
    l_ref[...] = l_ref[...] * corr + jnp.sum(p, axis=1, keepdims=True)
    acc_ref[...] = acc_ref[...] * corr + pv_fn(p)
    m_ref[...] = m_new


def _head_match(n_rows, heads_per_lane_group, rows_per_head):
    row = lax.broadcasted_iota(jnp.int32, (n_rows, LANES), 0)
    lane = lax.broadcasted_iota(jnp.int32, (n_rows, LANES), 1)
    return (lane % heads_per_lane_group) == (row // rows_per_head), lane, row


def _attn_a_sample_kernel(pt_ref, q_ref, bias_ref, new_ref, *rest, n_steps, pps, ds):
    page_refs = rest[:pps]
    o_ref, rep_ref, m_ref, l_ref, acc_ref = rest[pps:]
    b = pl.program_id(0)
    j = pl.program_id(1)
    n_slots = 2 * HKV_A
    n_rows = H_A * ds
    n_tile = n_rows // bias_ref.shape[1]

    @pl.when((b == 0) & (j == 0))
    def _():
        s_i = lax.broadcasted_iota(jnp.int32, rep_ref.shape, 0)
        c_i = lax.broadcasted_iota(jnp.int32, rep_ref.shape, 1)
        rep_ref[...] = (c_i // n_slots == s_i).astype(BF16)

    @pl.when(j == 0)
    def _():
        m_ref[...] = jnp.full(m_ref.shape, NEG, F32)
        l_ref[...] = jnp.zeros_like(l_ref)
        acc_ref[...] = jnp.zeros_like(acc_ref)

    q = q_ref[0].astype(BF16)
    match, _, _ = _head_match(n_rows, n_slots, ds * (H_A // HKV_A))

    def attend(blocks):
        pgs = [rows.astype(BF16) for rows, _ in blocks]
        pieces = []
        for pg, (_, picked) in zip(pgs, blocks):
            cols = pg.shape[0]
            pick = jnp.concatenate([picked.astype(BF16)] * n_tile, axis=0)
            pick = _dot(pick, rep_ref[:, :cols])
            pieces += [jnp.where(match & (pk > 0.5), sc, NEG)
                       for sc, pk in zip(_lane_chunks(_dot_nt(q, pg) * SCALE), _lane_chunks(pick))]

        def pv(p):
            out, at = 0.0, 0
            for pg in pgs:
                part = p[:, at:at + pg.shape[0]]
                moved = jnp.concatenate([pltpu.roll(x, HKV_A, 1) for x in _lane_chunks(part)], axis=1)
                out = out + _dot(moved.astype(BF16), pg)
                at += pg.shape[0]
            return out

        _softmax_update(jnp.concatenate(pieces, axis=1), pv, m_ref, l_ref, acc_ref)

    def picked(i):
        return jnp.where(bias_ref[0][:, i * BLK:(i + 1) * BLK] == 0.0, 1.0, 0.0)

    @pl.when(j < n_steps)
    def _():
        attend([(page_ref[0], picked(i)) for i, page_ref in enumerate(page_refs)])

    @pl.when(j == n_steps)
    def _():
        attend([(new_ref[0], picked(0))])
        o_ref[0] = acc_ref[...] / l_ref[...]


def _attn_a_sample(page_table, q_rows, bias, new_rows, cache_flat, ds):
    db, n_pages = page_table.shape
    n_slots = 2 * HKV_A
    n_rows = H_A * ds
    pps = _pages_per_step(n_pages, 16)
    n_steps = n_pages // pps
    last = n_steps - 1

    def page_spec(i):
        return pl.BlockSpec((1, BLK * n_slots, HEAD_DIM),
                            lambda b, j, pt: (pt[b, jnp.minimum(j, last) * pps + i], 0, 0))

    return pl.pallas_call(
        functools.partial(_attn_a_sample_kernel, n_steps=n_steps, pps=pps, ds=ds),
        grid_spec=pltpu.PrefetchScalarGridSpec(
            num_scalar_prefetch=1,
            grid=(db, n_steps + 1),
            in_specs=[pl.BlockSpec((1, n_rows, HEAD_DIM), lambda b, j, pt: (b, 0, 0)),
                      pl.BlockSpec((1, bias.shape[1], BLK * pps), lambda b, j, pt: (b, 0, j)),
                      pl.BlockSpec((1, LANES, HEAD_DIM), lambda b, j, pt: (b, 0, 0))]
                     + [page_spec(i) for i in range(pps)],
            out_specs=pl.BlockSpec((1, n_rows, HEAD_DIM), lambda b, j, pt: (b, 0, 0)),
            scratch_shapes=[pltpu.VMEM((BLK, BLK * n_slots), BF16),
                            pltpu.VMEM((n_rows, 1), F32), pltpu.VMEM((n_rows, 1), F32),
                            pltpu.VMEM((n_rows, HEAD_DIM), F32)]),
        out_shape=jax.ShapeDtypeStruct((db, n_rows, HEAD_DIM), F32),
        compiler_params=_params(("arbitrary", "arbitrary")),
        name="dsa_attend_sample",
    )(page_table, q_rows, bias, new_rows, *([cache_flat] * pps))


def _attn_b_sample_kernel(pt_ref, q_ref, knew_ref, vnew_ref, *rest, n_steps, pps, ds):
    page_refs = rest[:pps]
    o_ref, acc_ref, carry_ref = rest[pps:]
    j = pl.program_id(1)
    n_rows = H_B * ds
    q = q_ref[0].astype(BF16)
    match, lane, row = _head_match(n_rows, H_B, ds)
    sums = _suffix_and_ones(H_B)

    def attend(blocks, strict_fn):
        z = jnp.concatenate([_dot_nt(q, k.astype(BF16)) * SCALE for k, _ in blocks], axis=1)
        w, tail = _stick_weights(z, strict_fn, carry_ref[...], sums)
        carry_ref[...] = tail
        out, at = 0.0, 0
        for k, v in blocks:
            out = out + _dot(w[:, at:at + k.shape[0]].astype(BF16), v.astype(BF16))
            at += k.shape[0]
        acc_ref[...] += out

    @pl.when(j == 0)
    def _():
        acc_ref[...] = jnp.zeros_like(acc_ref)
        carry_ref[...] = jnp.zeros_like(carry_ref)
        s_new = lane // H_B
        attend([(knew_ref[0], vnew_ref[0])], lambda i: match & (s_new < row % ds))

    @pl.when(j > 0)
    def _():
        attend([(page_ref[0, :, 0].reshape(BLK * H_B, HEAD_DIM), page_ref[0, :, 1].reshape(BLK * H_B, HEAD_DIM))
                for page_ref in reversed(page_refs)], lambda i: match)

    @pl.when(j == n_steps)
    def _():
        o_ref[0] = acc_ref[...]


def _attn_b_sample(page_table, q_rows, k_new, v_new, cache, ds):
    db, n_pages = page_table.shape
    n_rows = H_B * ds
    pps = _pages_per_step(n_pages, 8)
    n_steps = n_pages // pps

    def page_spec(i):
        return pl.BlockSpec((1, BLK, 2, H_B, HEAD_DIM),
                            lambda b, j, pt: (pt[b, n_pages - 1 - (jnp.maximum(j, 1) - 1) * pps - i], 0, 0, 0, 0))

    new_spec = pl.BlockSpec((1, LANES, HEAD_DIM), lambda b, j, pt: (b, 0, 0))
    return pl.pallas_call(
        functools.partial(_attn_b_sample_kernel, n_steps=n_steps, pps=pps, ds=ds),
        grid_spec=pltpu.PrefetchScalarGridSpec(
            num_scalar_prefetch=1,
            grid=(db, n_steps + 1),
            in_specs=[pl.BlockSpec((1, n_rows, HEAD_DIM), lambda b, j, pt: (b, 0, 0)), new_spec, new_spec]
                     + [page_spec(i) for i in range(pps)],
            out_specs=pl.BlockSpec((1, n_rows, HEAD_DIM), lambda b, j, pt: (b, 0, 0)),
            scratch_shapes=[pltpu.VMEM((n_rows, HEAD_DIM), F32), pltpu.VMEM((n_rows, LANES), F32)]),
        out_shape=jax.ShapeDtypeStruct((db, n_rows, HEAD_DIM), F32),
        compiler_params=_params(("parallel", "arbitrary")),
        name="stickbreak_sample",
    )(page_table, q_rows, k_new, v_new, *([cache] * pps))


def _page_suffix_kernel(lf_ref, sfx_ref, tot_ref):
    x = lf_ref[...]
    r = lax.broadcasted_iota(jnp.int32, (BLK, BLK), 0)
    c = lax.broadcasted_iota(jnp.int32, (BLK, BLK), 1)
    sfx_ref[...] = _dot3((c > r).astype(BF16), x)
    tot_ref[...] = _dot3(jnp.ones((8, BLK), BF16), x)


def _page_suffix(lf_t, tn):
    n = lf_t.shape[1]
    return pl.pallas_call(
        _page_suffix_kernel,
        grid=(n // tn,),
        in_specs=[pl.BlockSpec((BLK, tn), lambda i: (0, i))],
        out_specs=[pl.BlockSpec((BLK, tn), lambda i: (0, i)), pl.BlockSpec((8, tn), lambda i: (0, i))],
        out_shape=[jax.ShapeDtypeStruct((BLK, n), F32), jax.ShapeDtypeStruct((8, n), F32)],
        compiler_params=_params(("parallel",)),
        name="log_forget_page_suffix",
    )(lf_t)


def _attn_c_sample_kernel(pt_ref, q_ref, lfnew_ref, knew_ref, vnew_ref, *rest, n_steps, pps, ds):
    sfx_refs, tot_refs, page_refs = rest[:pps], rest[pps:2 * pps], rest[2 * pps:3 * pps]
    o_ref, m_ref, l_ref, acc_ref, carry_ref, cn_ref = rest[3 * pps:]
    j = pl.program_id(1)
    n_rows = H_C * ds
    q = q_ref[0].astype(BF16)
    match, lane, row = _head_match(n_rows, H_C, ds)

    @pl.when(j == 0)
    def _():
        m_ref[...] = jnp.full(m_ref.shape, NEG, F32)
        l_ref[...] = jnp.zeros_like(l_ref)
        acc_ref[...] = jnp.zeros_like(acc_ref)
        carry_ref[...] = jnp.zeros_like(carry_ref)
        lf = lfnew_ref[0]
        s_new = lane // H_C
        cn_ref[...] = jnp.sum(jnp.where(match & (s_new <= row % ds), lf, 0.0), axis=1, keepdims=True)
        r2 = lax.broadcasted_iota(jnp.int32, (LANES, LANES), 0)
        c2 = lax.broadcasted_iota(jnp.int32, (LANES, LANES), 1)
        upto = ((r2 % H_C == c2 % H_C) & (r2 // H_C <= c2 // H_C)).astype(BF16)
        cum_keys = _dot3_left(jnp.broadcast_to(lf, (8, LANES)), upto)[0:1]
        k_rows = knew_ref[0].astype(BF16)
        s = _dot_nt(q, k_rows) * SCALE + (cn_ref[...] - cum_keys)
        s = jnp.where(match & (s_new <= row % ds), s, NEG)
        _softmax_update(s, lambda p: _dot(p.astype(BF16), vnew_ref[0].astype(BF16)), m_ref, l_ref, acc_ref)

    @pl.when(j > 0)
    def _():
        later = carry_ref[...]
        pieces, values = [], []
        for sfx_ref, tot_ref, page_ref in zip(sfx_refs, tot_refs, page_refs):
            k_rows = page_ref[0, :, 0].reshape(BLK * H_C, HEAD_DIM).astype(BF16)
            values.append(page_ref[0, :, 1].reshape(BLK * H_C, HEAD_DIM).astype(BF16))
            s = _dot_nt(q, k_rows) * SCALE + (sfx_ref[0] + (cn_ref[...] + later))
            pieces += [jnp.where(match, piece, NEG) for piece in _lane_chunks(s)]
            later = later + tot_ref[0]
        carry_ref[...] = later
        cols = BLK * H_C

        def pv(p):
            out = 0.0
            for i, v_rows in enumerate(values):
                out = out + _dot(p[:, i * cols:(i + 1) * cols].astype(BF16), v_rows)
            return out

        _softmax_update(jnp.concatenate(pieces, axis=1), pv, m_ref, l_ref, acc_ref)

    @pl.when(j == n_steps)
    def _():
        o_ref[0] = acc_ref[...] / l_ref[...]


def _attn_c_sample(page_table, q_rows, lf_new, k_new, v_new, sfx_flat, tot_col, cache, ds):
    db, n_pages = page_table.shape
    n_rows = H_C * ds
    pps = _pages_per_step(n_pages, 4)
    n_steps = n_pages // pps

    def page_of(i):
        return lambda b, j, pt: pt[b, n_pages - 1 - (jnp.maximum(j, 1) - 1) * pps - i]

    def specs(shape):
        zeros = (0,) * (len(shape) - 1)
        return [pl.BlockSpec(shape, (lambda f: lambda b, j, pt: (f(b, j, pt),) + zeros)(page_of(i)))
                for i in range(pps)]

    new_spec = pl.BlockSpec((1, LANES, HEAD_DIM), lambda b, j, pt: (b, 0, 0))
    return pl.pallas_call(
        functools.partial(_attn_c_sample_kernel, n_steps=n_steps, pps=pps, ds=ds),
        grid_spec=pltpu.PrefetchScalarGridSpec(
            num_scalar_prefetch=1,
            grid=(db, n_steps + 1),
            in_specs=[pl.BlockSpec((1, n_rows, HEAD_DIM), lambda b, j, pt: (b, 0, 0)),
                      pl.BlockSpec((1, 1, LANES), lambda b, j, pt: (b, 0, 0)),
                      new_spec, new_spec]
                     + specs((1, 1, BLK * H_C)) + specs((1, n_rows, 1)) + specs((1, BLK, 2, H_C, HEAD_DIM)),
            out_specs=pl.BlockSpec((1, n_rows, HEAD_DIM), lambda b, j, pt: (b, 0, 0)),
            scratch_shapes=[pltpu.VMEM((n_rows, 1), F32), pltpu.VMEM((n_rows, 1), F32),
                            pltpu.VMEM((n_rows, HEAD_DIM), F32), pltpu.VMEM((n_rows, 1), F32),
                            pltpu.VMEM((n_rows, 1), F32)]),
        out_shape=jax.ShapeDtypeStruct((db, n_rows, HEAD_DIM), F32),
        compiler_params=_params(("parallel", "arbitrary")),
        name="forget_attend_sample",
    )(page_table, q_rows, lf_new, k_new, v_new, *([sfx_flat] * pps), *([tot_col] * pps), *([cache] * pps))


def _layer_norm(xf, g, b):
    mu = jnp.mean(xf, axis=1, keepdims=True)
    d = xf - mu
    var = jnp.mean(d * d, axis=1, keepdims=True)
    return d * lax.rsqrt(var + LN_EPS) * g + b


def _route(y, wr_ref, br_ref):
    y_hi, y_lo = _split2(y)
    w = wr_ref[...]
    p1 = _dot(y_hi, w)
    p2 = _dot(y_lo, w)
    logits = p1 + pltpu.roll(p1, LANES - N_EXPERTS, 1) + p2
    lane = lax.broadcasted_iota(jnp.int32, logits.shape, 1)
    in_grp = lane < N_GROUPS
    s = [jax.nn.sigmoid(logits if j == 0 else pltpu.roll(logits, LANES - j * N_GROUPS, 1))
         for j in range(EPG)]
    sel = [jnp.where(in_grp, s[j] + br_ref[j:j + 1, :], NEG) for j in range(EPG)]
    top2 = None
    for a in range(EPG):
        for b in range(a + 1, EPG):
            pair = sel[a] + sel[b]
            top2 = pair if top2 is None else jnp.maximum(top2, pair)
    top2 = jnp.where(in_grp, top2, -jnp.inf)
    best_val = jnp.max(top2, axis=1, keepdims=True)
    g_best = jnp.min(jnp.where(top2 == best_val, lane, LANES), axis=1, keepdims=True)
    mine = lane == g_best
    picked = []
    for j in range(EPG):
        rank = jnp.zeros(logits.shape, F32)
        for i in range(EPG):
            if i == j:
                continue
            ahead = (sel[i] >= sel[j]) if i < j else (sel[i] > sel[j])
            rank = rank + ahead.astype(F32)
        picked.append(jnp.sum(jnp.where(mine & (rank < 2.0), s[j], 0.0), axis=1, keepdims=True))
    denom = picked[0] + picked[1] + picked[2] + picked[3]
    extra = jnp.where(lane == EPG, g_best.astype(F32), 0.0)
    for j in range(EPG):
        extra = jnp.where(lane == j, picked[j] / denom, extra)
    return extra


def _outproj_kernel(o1_ref, o2_ref, s1_ref, s2_ref, w1_ref, w2_ref, x_ref, g_ref, b_ref, wr_ref, br_ref, out_ref,
                    *, d_model, n_prompt_tiles):
    def finish(o1, o2, rows):
        mix = _dot(o1.astype(BF16), w1_ref[...]) + _dot(o2.astype(BF16), w2_ref[...])
        y = _layer_norm(ALPHA * x_ref[:rows, :] + mix, g_ref[...], b_ref[...])
        out_ref[:rows, :d_model] = y
        out_ref[:rows, d_model:] = _route(y, wr_ref, br_ref)

    @pl.when(pl.program_id(0) < n_prompt_tiles)
    def _():
        finish(o1_ref[...], o2_ref[...], o1_ref.shape[0])

    @pl.when(pl.program_id(0) >= n_prompt_tiles)
    def _():
        finish(s1_ref[...], s2_ref[...], s1_ref.shape[0])


def _outproj(o1, o2, s1, s2, o2_block, w_bf16, x, g, b, wr, br, d_model, tm):
    nf = x.shape[0]
    half = w_bf16.shape[0] // 2
    n_prompt, n_sample = o1.shape[0], s1.shape[0]
    assert n_prompt % tm == 0 and n_sample <= tm and n_prompt + n_sample == nf
    last = n_prompt // tm - 1
    return pl.pallas_call(
        functools.partial(_outproj_kernel, d_model=d_model, n_prompt_tiles=n_prompt // tm),
        grid=(n_prompt // tm + 1,),
        in_specs=[pl.BlockSpec((tm, half), lambda i: (jnp.minimum(i, last), 0)),
                  pl.BlockSpec((tm, half), lambda i: (jnp.minimum(i, last), o2_block)),
                  pl.BlockSpec((n_sample, half), lambda i: (0, 0)),
                  pl.BlockSpec((n_sample, half), lambda i: (0, o2_block)),
                  pl.BlockSpec((half, d_model), lambda i: (0, 0)),
                  pl.BlockSpec((half, d_model), lambda i: (1, 0)),
                  pl.BlockSpec((tm, d_model), lambda i: (i, 0)),
                  pl.BlockSpec((1, d_model), lambda i: (0, 0)),
                  pl.BlockSpec((1, d_model), lambda i: (0, 0)),
                  pl.BlockSpec((d_model, LANES), lambda i: (0, 0)),
                  pl.BlockSpec((EPG, LANES), lambda i: (0, 0))],
        out_specs=pl.BlockSpec((tm, d_model + XCOLS), lambda i: (i, 0)),
        out_shape=jax.ShapeDtypeStruct((nf, d_model + XCOLS), F32),
        compiler_params=_params(("parallel",)),
        name="outproj_norm_route",
    )(o1, o2, s1, s2, w_bf16, w_bf16, x, g, b, wr, br)


def _gather_rows(idx_ref, base, src_ref, dst_ref, sem, n_rows):
    def issue(r, c):
        pltpu.make_async_copy(src_ref.at[pl.ds(idx_ref[base + r], 1)], dst_ref.at[pl.ds(r, 1)], sem).start()
        return c

    lax.fori_loop(0, n_rows, issue, 0, unroll=8)

    def drain(r, c):
        pltpu.make_async_copy(src_ref.at[pl.ds(0, 1)], dst_ref.at[pl.ds(r, 1)], sem).wait()
        return c

    lax.fori_loop(0, n_rows, drain, 0, unroll=8)


def _moe_kernel(grp_ref, valid_ref, src_ref, xa_ref, wg_ref, wu_ref, wd_ref, g_ref, b_ref, o_ref,
                xs_ref, xb_ref, acc_ref, sem, *, d_model, tm):
    i = pl.program_id(0)
    e = pl.program_id(1)
    valid = valid_ref[i] == 1

    @pl.when(valid & (e == 0))
    def _():
        _gather_rows(src_ref, i * tm, xa_ref, xs_ref, sem, tm)
        xb_ref[...] = xs_ref[:, :d_model].astype(BF16)
        acc_ref[...] = jnp.zeros_like(acc_ref)

    @pl.when(valid)
    def _():
        xb = xb_ref[...]
        a = _dot(xb, wg_ref[0].astype(BF16))
        h = a * jax.nn.sigmoid(a) * _dot(xb, wu_ref[0].astype(BF16))
        extra = xs_ref[:, d_model:]
        lane = lax.broadcasted_iota(jnp.int32, extra.shape, 1)
        gate = jnp.sum(jnp.where(lane == e, extra, 0.0), axis=1, keepdims=True)
        acc_ref[...] += _dot((h * gate).astype(BF16), wd_ref[0].astype(BF16))

    @pl.when(valid & (e == EPG - 1))
    def _():
        o_ref[...] = _layer_norm(ALPHA * xs_ref[:, :d_model] + acc_ref[...], g_ref[...], b_ref[...])

    @pl.when(jnp.logical_not(valid) & (e == EPG - 1))
    def _():
        o_ref[...] = jnp.zeros_like(o_ref)


def _moe(tile_grp, tile_valid, src, xa, wg, wu, wd, g, b, d_model, tm):
    n_tiles = tile_grp.shape[0]
    d_exp = wg.shape[2]

    def w_idx(i, e, grp, valid, src):
        return (grp[i] * EPG + jnp.where(valid[i] == 1, e, EPG - 1), 0, 0)

    return pl.pallas_call(
        functools.partial(_moe_kernel, d_model=d_model, tm=tm),
        grid_spec=pltpu.PrefetchScalarGridSpec(
            num_scalar_prefetch=3,
            grid=(n_tiles, EPG),
            in_specs=[pl.BlockSpec(memory_space=pl.ANY),
                      pl.BlockSpec((1, d_model, d_exp), w_idx),
                      pl.BlockSpec((1, d_model, d_exp), w_idx),
                      pl.BlockSpec((1, d_exp, d_model), w_idx),
                      pl.BlockSpec((1, d_model), lambda i, e, *_: (0, 0)),
                      pl.BlockSpec((1, d_model), lambda i, e, *_: (0, 0))],
            out_specs=pl.BlockSpec((tm, d_model), lambda i, e, *_: (i, 0)),
            scratch_shapes=[pltpu.VMEM((tm, d_model + XCOLS), F32),
                            pltpu.VMEM((tm, d_model), BF16),
                            pltpu.VMEM((tm, d_model), F32),
                            pltpu.SemaphoreType.DMA(())]),
        out_shape=jax.ShapeDtypeStruct((n_tiles * tm, d_model), F32),
        compiler_params=_params(("arbitrary", "arbitrary")),
        name="grouped_moe",
    )(tile_grp, tile_valid, src, xa, wg, wu, wd, g, b)


def _unpermute_kernel(idx_ref, src_ref, o_ref, sem, *, tg):
    _gather_rows(idx_ref, pl.program_id(0) * tg, src_ref, o_ref, sem, tg)


def _unpermute(dest_padded, ys, n_rows, tg):
    d = ys.shape[1]
    return pl.pallas_call(
        functools.partial(_unpermute_kernel, tg=tg),
        grid_spec=pltpu.PrefetchScalarGridSpec(
            num_scalar_prefetch=1,
            grid=(_cdiv(n_rows, tg),),
            in_specs=[pl.BlockSpec(memory_space=pl.ANY)],
            out_specs=pl.BlockSpec((tg, d), lambda i, idx: (i, 0)),
            scratch_shapes=[pltpu.SemaphoreType.DMA(())]),
        out_shape=jax.ShapeDtypeStruct((n_rows, d), F32),
        compiler_params=_params(("arbitrary",)),
        name="unpermute_rows",
    )(dest_padded, ys)


def _routing_plan(grp, tm):
    nf = grp.shape[0]
    n_tiles = _cdiv(nf + N_GROUPS * (tm - 1), tm)
    onehot = (grp[:, None] == jnp.arange(N_GROUPS, dtype=jnp.int32)[None, :]).astype(jnp.int32)
    counts = jnp.sum(onehot, axis=0)
    rank = jnp.sum((jnp.cumsum(onehot, axis=0) - onehot) * onehot, axis=1)
    padded = ((counts + tm - 1) // tm) * tm
    ends = jnp.cumsum(padded)
    dest = (ends - padded)[grp] + rank
    src = jnp.zeros((n_tiles * tm,), jnp.int32).at[dest].set(jnp.arange(nf, dtype=jnp.int32))
    starts = jnp.arange(n_tiles, dtype=jnp.int32) * tm
    tile_valid = (starts < ends[-1]).astype(jnp.int32)
    tile_grp = jnp.minimum(jnp.searchsorted(ends, starts, side="right"), N_GROUPS - 1).astype(jnp.int32)
    last_grp = tile_grp[jnp.maximum(ends[-1] // tm - 1, 0)]
    tile_grp = jnp.where(tile_valid == 1, tile_grp, last_grp)
    return tile_grp, tile_valid, src, dest


def _row_tile(n):
    return next(t for t in range(1024, 7, -8) if n % t == 0)


def _ffn(xa, wg, wu, wd, g, b, d_model, tm, row_sets):
    grp = xa[:, d_model + EPG].astype(jnp.int32)
    tile_grp, tile_valid, src, dest = _routing_plan(grp, tm)
    ys = _moe(tile_grp, tile_valid, src, xa, wg, wu, wd, g, b, d_model, tm)
    return [_unpermute(dest[rows], ys, rows.shape[0], _row_tile(rows.shape[0])) for rows in row_sets]


def _rope_tables(pos):
    def table(dim):
        half = dim // 2
        inv = ROPE_THETA ** (-jnp.arange(half, dtype=F32) / half)
        ang = pos.astype(F32)[:, None] * inv[None, :]
        cos = jnp.cos(ang)
        sin = jnp.sin(ang)
        reps = LANES // dim
        return jnp.tile(jnp.concatenate([cos, cos], axis=1), (1, reps)), \
            jnp.tile(jnp.concatenate([-sin, sin], axis=1), (1, reps))

    c128, s128 = table(HEAD_DIM)
    c64, s64 = table(IDX_DIM)
    return c128, s128, c64, s64


def _pad_rows(a, rows):
    return jnp.pad(a, [(0, 0), (0, rows - a.shape[1])] + [(0, 0)] * (a.ndim - 2))


def _head_rows(a, db, ds, heads):
    return a.reshape(db, ds, heads, HEAD_DIM).transpose(0, 2, 1, 3).reshape(db, heads * ds, HEAD_DIM)


def _token_rows(a, db, ds, heads):
    return a.reshape(db, heads, ds, HEAD_DIM).transpose(0, 2, 1, 3).reshape(db * ds, heads * HEAD_DIM)


def _new_block(a, db):
    slots = a.shape[2]
    return _pad_rows(a, LANES // slots).reshape(db, LANES, HEAD_DIM)


def kernel(x_prompt, x_sample, cache_l0_a_kv, cache_l0_idx_k, cache_l0_b_kv, cache_l1_c_kv, cache_l1_logf,
           page_table, meta_tokens, w_in_l0, w_out_l0, w_in_l1, b_forget_l1, w_out_l1, ln_mix_g, ln_mix_b,
           ln_ffn_g, ln_ffn_b, w_router, b_router, w_gate, w_up, w_down):
    n_batch, seq, d_model = x_prompt.shape
    db, ds, _ = x_sample.shape
    assert ds <= 8 and d_model % LANES == 0
    t_len = seq + N_META
    tp = _cdiv(t_len, BLK) * BLK
    n_prompt = n_batch * tp
    n_sample = db * ds
    nf = n_prompt + n_sample
    n_pool = cache_l0_a_kv.shape[0]
    n_pages = page_table.shape[1]
    past = n_pages * BLK
    topk_prompt = min(TOPK_MAX, seq // 4)
    topk_sample = min(TOPK_MAX, (past + ds) // 4)
    tm_proj = 1072 if nf % 1072 == 0 else 128
    tm_out = 256
    tm_moe = 512
    all_rows = jnp.arange(nf, dtype=jnp.int32)
    prompt_out_rows = (jnp.arange(n_batch, dtype=jnp.int32)[:, None] * tp + N_META
                       + jnp.arange(seq, dtype=jnp.int32)[None, :]).reshape(-1)

    meta = jnp.broadcast_to(meta_tokens[None], (n_batch, N_META, d_model)).astype(x_prompt.dtype)
    hp = _pad_rows(jnp.concatenate([meta, x_prompt], axis=1), tp)
    x0 = jnp.concatenate([hp.reshape(n_prompt, d_model), x_sample.reshape(n_sample, d_model)], axis=0)
    pos = jnp.concatenate([jnp.tile(jnp.arange(tp), n_batch), jnp.tile(past + jnp.arange(ds), db)])
    tables = _rope_tables(pos)

    cuts = np.cumsum((0, H_A * HEAD_DIM, HKV_A * HEAD_DIM, HKV_A * HEAD_DIM, H_B * HEAD_DIM, H_B * HEAD_DIM,
                      H_B * HEAD_DIM, H_IDX * IDX_DIM, IDX_DIM, H_IDX))
    seg = [w_in_l0[:, cuts[i]:cuts[i + 1]] for i in range(9)]
    qa_w, ka_w, va_w, qb_w, kb_w, vb_w, iq_w, ik_w, iw_w = seg
    zeros = lambda n: jnp.zeros((d_model, n), w_in_l0.dtype)
    w0 = jnp.concatenate([qa_w, iq_w, ka_w, va_w, qb_w, kb_w, vb_w, ik_w, zeros(LANES - IDX_DIM),
                          iw_w, zeros(LANES - H_IDX)], axis=1).astype(BF16)
    modes0 = jnp.asarray([1] * 4 + [2] * 4 + [1] * 2 + [0] * 14 + [3], jnp.int32)
    n_qkv = 3 * H_C * HEAD_DIM
    w1 = jnp.concatenate([w_in_l1[:, :n_qkv], w_in_l1[:, n_qkv:], zeros(MODE_COLS - H_C)], axis=1).astype(BF16)
    bf_row = jnp.zeros((1, LANES), F32).at[0, :H_C].set(b_forget_l1)
    w_out0 = w_out_l0.astype(BF16)
    w_out1 = w_out_l1.astype(BF16)
    wr_perm = w_router.reshape(d_model, N_GROUPS, EPG).transpose(0, 2, 1).reshape(d_model, N_EXPERTS)
    wr_hi = wr_perm.astype(BF16)
    wr_lo = (wr_perm - wr_hi.astype(F32)).astype(BF16)
    wr = jnp.concatenate([wr_hi, wr_lo, jnp.zeros((d_model, LANES - 2 * N_EXPERTS), BF16)], axis=1)
    br = jnp.zeros((EPG, LANES), F32).at[:, :N_GROUPS].set(b_router.reshape(N_GROUPS, EPG).T)
    wg, wu, wd = w_gate, w_up, w_down
    row = lambda v: v.reshape(1, d_model)

    p0 = _inproj(x0, w0, d_model, tm_proj, modes0, tables)
    col = lambda blk0, n: slice(blk0 * LANES, (blk0 + n) * LANES)
    ps = p0[n_prompt:]

    bias_p = _idx_prompt(p0, n_batch, tp, topk_prompt)
    oa_p = _attn_a_prompt(p0, bias_p, n_batch, tp)
    ob_p = _attn_b_prompt(p0, n_batch, tp)

    grp_q = H_A // HKV_A
    iq_th = jnp.tile(ps[:, col(8, 8)].reshape(db, ds * H_IDX, IDX_DIM), (1, grp_q, 1))
    iw_th = jnp.tile(ps[:, 49 * LANES:49 * LANES + H_IDX].reshape(db, ds * H_IDX, 1), (1, grp_q, 1))
    ik_s = ps[:, 48 * LANES:48 * LANES + IDX_DIM].reshape(db, ds, IDX_DIM)
    bias_s = _idx_sample(page_table, iq_th, iw_th, _pad_rows(ik_s, BLK), cache_l0_idx_k, topk_sample, ds)

    ka_s = ps[:, col(16, 4)].reshape(db, ds, HKV_A, HEAD_DIM)
    va_s = ps[:, col(20, 4)].reshape(db, ds, HKV_A, HEAD_DIM)
    kv_a_s = jnp.stack([ka_s, va_s], axis=2)
    oa_s = _attn_a_sample(page_table, _head_rows(ps[:, col(0, 8)], db, ds, H_A), bias_s,
                          _new_block(kv_a_s.reshape(db, ds, 2 * HKV_A, HEAD_DIM), db),
                          cache_l0_a_kv.reshape(n_pool, BLK * 2 * HKV_A, HEAD_DIM), ds)
    oa_s = _token_rows(oa_s, db, ds, H_A)

    kb_s = ps[:, col(32, 8)].reshape(db, ds, H_B, HEAD_DIM)
    vb_s = ps[:, col(40, 8)].reshape(db, ds, H_B, HEAD_DIM)
    kv_b_s = jnp.stack([kb_s, vb_s], axis=2)
    ob_s = _attn_b_sample(page_table, _head_rows(ps[:, col(24, 8)], db, ds, H_B),
                          _new_block(kb_s, db), _new_block(vb_s, db), cache_l0_b_kv, ds)
    ob_s = _token_rows(ob_s, db, ds, H_B)

    xa1 = _outproj(oa_p, ob_p, oa_s, ob_s, 0, w_out0, x0, row(ln_mix_g[0]), row(ln_mix_b[0]), wr, br, d_model,
                   tm_out)
    x1, = _ffn(xa1, wg[0], wu[0], wd[0], row(ln_ffn_g[0]), row(ln_ffn_b[0]), d_model, tm_moe, [all_rows])

    p1 = _inproj(x1, w1, d_model, tm_proj)
    logf, cum = _logf(p1, bf_row, nf, tp // BLK)
    cum_t = cum[:n_prompt, :H_C].reshape(n_batch, tp, H_C).transpose(0, 2, 1)
    oc_p = _attn_c_prompt(p1, cum_t[..., None], cum_t[:, :, None, :], n_batch, tp)

    ps1 = p1[n_prompt:]
    kc_s = ps1[:, col(16, 16)].reshape(db, ds, H_C, HEAD_DIM)
    vc_s = ps1[:, col(32, 16)].reshape(db, ds, H_C, HEAD_DIM)
    kv_c_s = jnp.stack([kc_s, vc_s], axis=2)
    logf_s = logf[n_prompt:, :H_C].reshape(db, ds, H_C)
    lf_new = _pad_rows(logf_s, LANES // H_C).reshape(db, 1, LANES)
    n_lf = n_pool * H_C
    tn_lf = min(2048, _cdiv(n_lf, LANES) * LANES)
    n_lf_pad = _cdiv(n_lf, tn_lf) * tn_lf
    lf_t = jnp.pad(cache_l1_logf.astype(F32).transpose(1, 0, 2).reshape(BLK, n_lf), [(0, 0), (0, n_lf_pad - n_lf)])
    sfx_t, tot = _page_suffix(lf_t, tn_lf)
    sfx_flat = sfx_t[:, :n_lf].reshape(BLK, n_pool, H_C).transpose(1, 0, 2).reshape(n_pool, 1, BLK * H_C)
    tot_col = jnp.repeat(tot[0, :n_lf].reshape(n_pool, H_C), ds, axis=1)[..., None]
    oc_s = _attn_c_sample(page_table, _head_rows(ps1[:, col(0, 16)], db, ds, H_C), lf_new,
                          _new_block(kc_s, db), _new_block(vc_s, db), sfx_flat, tot_col, cache_l1_c_kv, ds)
    oc_s = _token_rows(oc_s, db, ds, H_C)

    xa2 = _outproj(oc_p, oc_p, oc_s, oc_s, 1, w_out1, x1, row(ln_mix_g[1]), row(ln_mix_b[1]), wr, br, d_model,
                   tm_out)
    y_prompt, y_sample = _ffn(xa2, wg[1], wu[1], wd[1], row(ln_ffn_g[1]), row(ln_ffn_b[1]), d_model, tm_moe,
                              [prompt_out_rows, all_rows[n_prompt:]])

    def prompt_rows(a, blk0, heads):
        return a[:n_prompt, col(blk0, heads)].reshape(n_batch, tp, heads, HEAD_DIM)[:, :t_len]

    y_prompt = y_prompt.reshape(n_batch, seq, d_model)
    y_sample = y_sample.reshape(db, ds, d_model)
    a_kv_p = jnp.stack([prompt_rows(p0, 16, HKV_A), prompt_rows(p0, 20, HKV_A)], axis=2)
    idx_k_p = p0[:n_prompt, 48 * LANES:48 * LANES + IDX_DIM].reshape(n_batch, tp, IDX_DIM)[:, :t_len]
    b_kv_p = jnp.stack([prompt_rows(p0, 32, H_B), prompt_rows(p0, 40, H_B)], axis=2)
    c_kv_p = jnp.stack([prompt_rows(p1, 16, H_C), prompt_rows(p1, 32, H_C)], axis=2)
    logf_p = logf[:n_prompt, :H_C].reshape(n_batch, tp, H_C)[:, :t_len]
    return (y_prompt, y_sample, a_kv_p, kv_a_s, idx_k_p, ik_s, b_kv_p, kv_b_s, c_kv_p, kv_c_s, logf_p, logf_s)
```
